```python
import jax
import jax.numpy as jnp
from jax import lax
import numpy as np

D_MODEL = 4096
BATCH = 2
SEQ = 8192
DEPTH = 4
DEC_BATCH = 8
DEC_SEQ = 32
PAST_LEN = 2048

CHUNK = 64
N_A = DEPTH // 2
N_B = DEPTH - N_A
ROPE_THETA = 500000.0
EPS = 1e-6
NEG_INF = -1e30
MLA_HEADS = D_MODEL // 128
Q_LORA = D_MODEL // 4
KV_LORA = 512
QK_NOPE = 128
QK_ROPE = 64
V_DIM = 128
Q_BLOCK = 128
MLA_SCALE = (QK_NOPE + QK_ROPE) ** -0.5
SWA_HEAD_DIM = 64
SWA_HEADS = D_MODEL // SWA_HEAD_DIM
SWA_KV_HEADS = 8
SWA_GROUP = SWA_HEADS // SWA_KV_HEADS
WINDOW = 128
WIN_CHUNKS = WINDOW // CHUNK
ROPE_DIM_B = SWA_HEAD_DIM // 4
SWA_SCALE = SWA_HEAD_DIM ** -0.5
D_FF = 256 * ((8 * D_MODEL // 3 + 255) // 256)
CONV_W = 3

kernel_name = "yoco_mla_swa_sink_convffn_step"


def rmsnorm(x, g):
    xf = x.astype(jnp.float32)
    y = xf * lax.rsqrt(jnp.mean(xf * xf, axis=-1, keepdims=True) + EPS)
    return (y * g.astype(jnp.float32)).astype(x.dtype)


def rope(x, pos, rot_dim):
    half = rot_dim // 2
    inv = jnp.power(jnp.float32(ROPE_THETA), -jnp.arange(half, dtype=jnp.float32) * (2.0 / rot_dim))
    ang = pos.astype(jnp.float32)[:, None] * inv[None, :]
    cos = jnp.cos(ang)[None, :, None, :].astype(x.dtype)
    sin = jnp.sin(ang)[None, :, None, :].astype(x.dtype)
    x1 = x[..., :half]
    x2 = x[..., half:rot_dim]
    return jnp.concatenate([x1 * cos - x2 * sin, x2 * cos + x1 * sin, x[..., rot_dim:]], axis=-1)


def mla_attend(q_nope, q_rope, q_pos, k_nope, k_rope, v, k_pos):
    s = (jnp.einsum('bqhd,bkhd->bhqk', q_nope, k_nope)
         + jnp.einsum('bqhr,bkr->bhqk', q_rope, k_rope)).astype(jnp.float32) * MLA_SCALE
    visible = (q_pos // CHUNK)[:, None] >= (k_pos // CHUNK)[None, :]
    s = jnp.where(visible[None, None], s, NEG_INF)
    pr = jax.nn.softmax(s, axis=-1).astype(v.dtype)
    return jnp.einsum('bhqk,bkhd->bqhd', pr, v)


def mla_attend_blocked(q_nope, q_rope, q_pos, k_nope, k_rope, v, k_pos):
    B, S, H, _ = q_nope.shape
    nb = S // Q_BLOCK

    def blocks(t):
        return jnp.moveaxis(t.reshape((B, nb, Q_BLOCK) + t.shape[2:]), 1, 0)

    o = lax.map(lambda a: mla_attend(a[0], a[1], a[2], k_nope, k_rope, v, k_pos),
                (blocks(q_nope), blocks(q_rope), q_pos.reshape(nb, Q_BLOCK)))
    return jnp.moveaxis(o, 0, 1).reshape(B, S, H, V_DIM)


def mla_layer(hn, pos, p, i, cache):
    B, S, _ = hn.shape
    a = hn @ p['mla_w_a'][i]
    cq = rmsnorm(a[..., :Q_LORA], p['mla_g_q'][i])
    ckv = rmsnorm(a[..., Q_LORA:Q_LORA + KV_LORA], p['mla_g_kv'][i])
    krope = rope(a[..., None, Q_LORA + KV_LORA:], pos, QK_ROPE)[:, :, 0]
    q = (cq @ p['mla_w_uq'][i]).reshape(B, S, MLA_HEADS, QK_NOPE + QK_ROPE)
    q_nope = q[..., :QK_NOPE]
    q_rope = rope(q[..., QK_NOPE:], pos, QK_ROPE)
    if cache is None:
        ckv_all, krope_all, k_pos = ckv, krope, pos
    else:
        past = cache['ckv'].shape[2]
        ckv_all = jnp.concatenate([cache['ckv'][i], ckv], axis=1)
        krope_all = jnp.concatenate([cache['krope'][i], krope], axis=1)
        k_pos = jnp.concatenate([jnp.arange(past), pos])
    k_nope = jnp.einsum('bkc,chd->bkhd', ckv_all, p['mla_w_uk'][i])
    v = jnp.einsum('bkc,chd->bkhd', ckv_all, p['mla_w_uv'][i])
    if cache is None:
        o = mla_attend_blocked(q_nope, q_rope, pos, k_nope, krope_all, v, k_pos)
    else:
        o = mla_attend(q_nope, q_rope, pos, k_nope, krope_all, v, k_pos)
    out = o.reshape(B, S, MLA_HEADS * V_DIM) @ p['mla_w_o'][i]
    return out, ckv, krope


def shared_kv(h, pos, p):
    B, S, _ = h.shape
    kv = rmsnorm(h, p['kv_shared_norm']) @ p['swa_w_kv']
    k = kv[..., :SWA_KV_HEADS * SWA_HEAD_DIM].reshape(B, S, SWA_KV_HEADS, SWA_HEAD_DIM)
    v = kv[..., SWA_KV_HEADS * SWA_HEAD_DIM:].reshape(B, S, SWA_KV_HEADS, SWA_HEAD_DIM)
    return rope(k, pos, ROPE_DIM_B), v


def sink_softmax(s, sink):
    sb = sink.astype(jnp.float32)[..., None, None]
    m = jnp.maximum(jnp.max(s, axis=-1, keepdims=True), sb)
    e = jnp.exp(s - m)
    return e / (jnp.sum(e, axis=-1, keepdims=True) + jnp.exp(sb - m))


def swa_banded(q, k, v, sink):
    B, S, HKV, G, DH = q.shape
    nC = S // CHUNK
    qb = q.reshape(B, nC, CHUNK, HKV, G, DH)
    pad = ((0, 0), (WIN_CHUNKS * CHUNK, 0), (0, 0), (0, 0))
    kp = jnp.pad(k, pad).reshape(B, nC + WIN_CHUNKS, CHUNK, HKV, DH)
    vp = jnp.pad(v, pad).reshape(B, nC + WIN_CHUNKS, CHUNK, HKV, DH)
    kb = jnp.concatenate([kp[:, j:j + nC] for j in range(WIN_CHUNKS + 1)], axis=2)
    vb = jnp.concatenate([vp[:, j:j + nC] for j in range(WIN_CHUNKS + 1)], axis=2)
    key_chunk = (jnp.arange(nC)[:, None]
                 + jnp.repeat(jnp.arange(WIN_CHUNKS + 1), CHUNK)[None, :] - WIN_CHUNKS)
    valid = key_chunk >= 0
    s = jnp.einsum('bnqhgd,bnkhd->bnhgqk', qb, kb).astype(jnp.float32) * SWA_SCALE
    s = jnp.where(valid[None, :, None, None, None, :], s, NEG_INF)
    pr = sink_softmax(s, sink).astype(v.dtype)
    o = jnp.einsum('bnhgqk,bnkhd->bnqhgd', pr, vb)
    return o.reshape(B, S, HKV, G, DH)


def swa_recent(q, k, v, sink):
    s = jnp.einsum('bqhgd,bkhd->bhgqk', q, k).astype(jnp.float32) * SWA_SCALE
    pr = sink_softmax(s, sink).astype(v.dtype)
    return jnp.einsum('bhgqk,bkhd->bqhgd', pr, v)


def swa_layer(hn, pos, ks, vs, p, i, cache):
    B, S, _ = hn.shape
    q = rope((hn @ p['swa_w_q'][i]).reshape(B, S, SWA_HEADS, SWA_HEAD_DIM), pos, ROPE_DIM_B)
    q = q.reshape(B, S, SWA_KV_HEADS, SWA_GROUP, SWA_HEAD_DIM)
    sink = p['swa_sinks'][i].reshape(SWA_KV_HEADS, SWA_GROUP)
    if cache is None:
        o = swa_banded(q, ks, vs, sink)
    else:
        o = swa_recent(q, jnp.concatenate([cache['swa_k'], ks], axis=1),
                       jnp.concatenate([cache['swa_v'], vs], axis=1), sink)
    return o.reshape(B, S, SWA_HEADS * SWA_HEAD_DIM) @ p['swa_w_o'][i]


def conv_ffn(hn, conv_prev, p, l):
    g = hn @ p['ffn_w_gate'][l]
    u = hn @ p['ffn_w_up'][l]
    S = g.shape[1]
    gp = jnp.concatenate([conv_prev, g], axis=1)
    w = p['ffn_conv_w'][l]
    gc = p['ffn_conv_b'][l] + w[0] * gp[:, :S]
    for j in range(1, CONV_W):
        gc = gc + w[j] * gp[:, j:j + S]
    return (jax.nn.silu(gc) * u) @ p['ffn_w_down'][l], gp[:, S:]


def trunk(x, pos, p, cache):
    B, S, _ = x.shape
    h = x
    ckv_rows, krope_rows, conv_rows = [], [], []
    ks = vs = None
    for l in range(DEPTH):
        hn = rmsnorm(h, p['norm_attn'][l])
        if l < N_A:
            att, ckv, krope = mla_layer(hn, pos, p, l, cache)
            ckv_rows.append(ckv)
            krope_rows.append(krope)
        else:
            if l == N_A:
                ks, vs = shared_kv(h, pos, p)
            att = swa_layer(hn, pos, ks, vs, p, l - N_A, cache)
        h = h + att
        hn = rmsnorm(h, p['norm_ffn'][l])
        if cache is None:
            conv_prev = jnp.zeros((B, CONV_W - 1, D_FF), x.dtype)
        else:
            conv_prev = cache['conv'][l]
        ff, conv_new = conv_ffn(hn, conv_prev, p, l)
        conv_rows.append(conv_new)
        h = h + ff
    y = rmsnorm(h, p['norm_final'])
    if cache is None:
        win = min(WINDOW, S)
        k_rows, v_rows = ks[:, S - win:], vs[:, S - win:]
    else:
        k_rows, v_rows = ks, vs
    return y, jnp.stack(ckv_rows), jnp.stack(krope_rows), k_rows, v_rows, jnp.stack(conv_rows)


def setup_inputs(seed: int = 0) -> dict:
    key = jax.random.key(seed)
    k = jax.random.split(key, 32)
    f32 = jnp.float32

    def w(kk, shape, fan_in):
        return jax.random.normal(kk, shape, f32) * fan_in ** -0.5

    def gain(kk, shape):
        return 1.0 + 0.05 * jax.random.normal(kk, shape, f32)

    win = min(WINDOW, PAST_LEN)
    return {
        'x_prompt': jax.random.normal(k[0], (BATCH, SEQ, D_MODEL), f32),
        'x_sample': jax.random.normal(k[1], (DEC_BATCH, DEC_SEQ, D_MODEL), f32),
        'cache_mla_ckv': jax.random.normal(k[2], (N_A, DEC_BATCH, PAST_LEN, KV_LORA), f32),
        'cache_mla_krope': jax.random.normal(k[3], (N_A, DEC_BATCH, PAST_LEN, QK_ROPE), f32),
        'cache_swa_k': jax.random.normal(k[4], (DEC_BATCH, win, SWA_KV_HEADS, SWA_HEAD_DIM), f32),
        'cache_swa_v': jax.random.normal(k[5], (DEC_BATCH, win, SWA_KV_HEADS, SWA_HEAD_DIM), f32),
        'state_conv': jax.random.normal(k[6], (DEPTH, DEC_BATCH, CONV_W - 1, D_FF), f32),
        'norm_attn': gain(k[7], (DEPTH, D_MODEL)),
        'norm_ffn': gain(k[8], (DEPTH, D_MODEL)),
        'mla_w_a': w(k[9], (N_A, D_MODEL, Q_LORA + KV_LORA + QK_ROPE), D_MODEL),
        'mla_g_q': gain(k[10], (N_A, Q_LORA)),
        'mla_g_kv': gain(k[11], (N_A, KV_LORA)),
        'mla_w_uq': w(k[12], (N_A, Q_LORA, MLA_HEADS * (QK_NOPE + QK_ROPE)), Q_LORA),
        'mla_w_uk': w(k[13], (N_A, KV_LORA, MLA_HEADS, QK_NOPE), KV_LORA),
        'mla_w_uv': w(k[14], (N_A, KV_LORA, MLA_HEADS, V_DIM), KV_LORA),
        'mla_w_o': w(k[15], (N_A, MLA_HEADS * V_DIM, D_MODEL), MLA_HEADS * V_DIM),
        'kv_shared_norm': gain(k[16], (D_MODEL,)),
        'swa_w_kv': w(k[17], (D_MODEL, 2 * SWA_KV_HEADS * SWA_HEAD_DIM), D_MODEL),
        'swa_w_q': w(k[18], (N_B, D_MODEL, SWA_HEADS * SWA_HEAD_DIM), D_MODEL),
        'swa_sinks': jax.random.normal(k[19], (N_B, SWA_HEADS), f32),
        'swa_w_o': w(k[20], (N_B, SWA_HEADS * SWA_HEAD_DIM, D_MODEL), SWA_HEADS * SWA_HEAD_DIM),
        'ffn_w_gate': w(k[21], (DEPTH, D_MODEL, D_FF), D_MODEL),
        'ffn_w_up': w(k[22], (DEPTH, D_MODEL, D_FF), D_MODEL),
        'ffn_conv_w': w(k[23], (DEPTH, CONV_W, D_FF), CONV_W),
        'ffn_conv_b': 0.01 * jax.random.normal(k[24], (DEPTH, D_FF), f32),
        'ffn_w_down': w(k[25], (DEPTH, D_FF, D_MODEL), D_FF),
        'norm_final': gain(k[26], (D_MODEL,)),
    }


def reference(x_prompt, x_sample, cache_mla_ckv, cache_mla_krope, cache_swa_k, cache_swa_v, state_conv,
              norm_attn, norm_ffn, mla_w_a, mla_g_q, mla_g_kv, mla_w_uq, mla_w_uk, mla_w_uv, mla_w_o,
              kv_shared_norm, swa_w_kv, swa_w_q, swa_sinks, swa_w_o,
              ffn_w_gate, ffn_w_up, ffn_conv_w, ffn_conv_b, ffn_w_down, norm_final):
    p = {
        'norm_attn': norm_attn, 'norm_ffn': norm_ffn,
        'mla_w_a': mla_w_a, 'mla_g_q': mla_g_q, 'mla_g_kv': mla_g_kv, 'mla_w_uq': mla_w_uq,
        'mla_w_uk': mla_w_uk, 'mla_w_uv': mla_w_uv, 'mla_w_o': mla_w_o,
        'kv_shared_norm': kv_shared_norm, 'swa_w_kv': swa_w_kv, 'swa_w_q': swa_w_q,
        'swa_sinks': swa_sinks, 'swa_w_o': swa_w_o,
        'ffn_w_gate': ffn_w_gate, 'ffn_w_up': ffn_w_up, 'ffn_conv_w': ffn_conv_w,
        'ffn_conv_b': ffn_conv_b, 'ffn_w_down': ffn_w_down, 'norm_final': norm_final,
    }
    pos_p = jnp.arange(x_prompt.shape[1])
    y_prompt, ckv_p, krope_p, swa_k_p, swa_v_p, conv_p = trunk(x_prompt, pos_p, p, None)
    past = cache_mla_ckv.shape[2]
    pos_s = past + jnp.arange(x_sample.shape[1])
    cache = {'ckv': cache_mla_ckv, 'krope': cache_mla_krope, 'swa_k': cache_swa_k,
             'swa_v': cache_swa_v, 'conv': state_conv}
    y_sample, ckv_s, krope_s, swa_k_s, swa_v_s, conv_s = trunk(x_sample, pos_s, p, cache)
    return (y_prompt, y_sample, ckv_p, krope_p, swa_k_p, swa_v_p, conv_p,
            ckv_s, krope_s, swa_k_s, swa_v_s, conv_s)
```

```python
import functools

import jax
import jax.numpy as jnp
from jax import lax
from jax.experimental import pallas as pl
from jax.experimental.pallas import tpu as pltpu

D_MODEL = 4096
DEPTH = 4
CHUNK = 64
N_A = DEPTH // 2
N_B = DEPTH - N_A
ROPE_THETA = 500000.0
EPS = 1e-6
NEG_INF = -1e30
MLA_HEADS = D_MODEL // 128
Q_LORA = D_MODEL // 4
KV_LORA = 512
QK_NOPE = 128
QK_ROPE = 64
V_DIM = 128
MLA_SCALE = (QK_NOPE + QK_ROPE) ** -0.5
SWA_HEAD_DIM = 64
SWA_HEADS = D_MODEL // SWA_HEAD_DIM
SWA_KV_HEADS = 8
SWA_GROUP = SWA_HEADS // SWA_KV_HEADS
WINDOW = 128
ROPE_DIM_B = SWA_HEAD_DIM // 4
SWA_SCALE = SWA_HEAD_DIM ** -0.5
D_FF = 256 * ((8 * D_MODEL // 3 + 255) // 256)
CONV_W = 3

LANES = 128
SUBLANES = 8
V7X_VMEM_LIMIT = 56 * 1024 * 1024

MLA_QDIM = 2 * LANES
SWA_KV_DIM = SWA_KV_HEADS * SWA_HEAD_DIM
HEAD_SLAB = 2 * LANES
BF16 = jnp.bfloat16
F32 = jnp.float32


def _params(n_grid):
    return pltpu.CompilerParams(dimension_semantics=("arbitrary",) * n_grid,
                                vmem_limit_bytes=V7X_VMEM_LIMIT)


def _tile(m, cap):
    for step in (LANES, 16):
        t = (min(cap, m) // step) * step
        while t >= step:
            if m % t == 0:
                return t
            t -= step
    return m


def _rms(x, g):
    return x * lax.rsqrt(jnp.mean(x * x, axis=-1, keepdims=True) + EPS) * g


def _rmsnorm_kernel(x_ref, g_ref, o_ref):
    o_ref[...] = _rms(x_ref[...], g_ref[...]).astype(o_ref.dtype)


def rmsnorm(x, g, out_dtype):
    m, d = x.shape
    tm = _tile(m, 256)
    return pl.pallas_call(
        _rmsnorm_kernel,
        grid=(m // tm,),
        in_specs=[pl.BlockSpec((tm, d), lambda i: (i, 0)),
                  pl.BlockSpec((1, d), lambda i: (0, 0))],
        out_specs=pl.BlockSpec((tm, d), lambda i: (i, 0)),
        out_shape=jax.ShapeDtypeStruct((m, d), out_dtype),
        compiler_params=_params(1),
        name="rmsnorm",
    )(x, g.reshape(1, d))


def _mm_kernel(x_ref, w_ref, *rest, epilogue, n_extra):
    y = jnp.dot(x_ref[...], w_ref[...], preferred_element_type=F32)
    epilogue(y, rest[:n_extra], rest[n_extra:])


def matmul(x, w, *, tm, tn, epilogue, extras=(), extra_specs=(), out_shapes, out_specs, name):
    m, k = x.shape
    n = w.shape[1]
    return pl.pallas_call(
        functools.partial(_mm_kernel, epilogue=epilogue, n_extra=len(extras)),
        grid=(m // tm, n // tn),
        in_specs=[pl.BlockSpec((tm, k), lambda i, j: (i, 0)),
                  pl.BlockSpec((k, tn), lambda i, j: (0, j))] + list(extra_specs),
        out_specs=out_specs,
        out_shape=out_shapes,
        compiler_params=_params(2),
        name=name,
    )(x, w, *extras)


def _epi_cast(y, extras, outs):
    outs[0][...] = y.astype(outs[0].dtype)


def _epi_residual(y, extras, outs):
    outs[0][...] = extras[0][...] + y


def _rope_slab(a, cos2, sin2):
    return a * cos2 + pltpu.roll(a, QK_ROPE, axis=1) * sin2


def _epi_mla_in(y, extras, outs):
    gq_ref, gkv_ref, cos_ref, sin_ref = extras
    cq_ref, ckv_ref, ckvb_ref, kr_ref, krb_ref = outs
    cq_ref[...] = _rms(y[:, :Q_LORA], gq_ref[...]).astype(BF16)
    ckv = _rms(y[:, Q_LORA:Q_LORA + KV_LORA], gkv_ref[...])
    ckv_ref[...] = ckv
    ckvb_ref[...] = ckv.astype(BF16)
    kr = _rope_slab(y[:, Q_LORA + KV_LORA:], cos_ref[...], sin_ref[...])
    kr_ref[...] = kr[:, :QK_ROPE]
    krb_ref[...] = kr.astype(BF16)


def _epi_mla_q(y, extras, outs):
    cos_ref, sin_ref = extras
    cos2 = cos_ref[...] * MLA_SCALE
    sin2 = sin_ref[...] * MLA_SCALE
    for s in range(y.shape[1] // MLA_QDIM):
        lo = s * MLA_QDIM
        outs[0][:, lo:lo + LANES] = (y[:, lo:lo + LANES] * MLA_SCALE).astype(BF16)
        outs[0][:, lo + LANES:lo + MLA_QDIM] = _rope_slab(
            y[:, lo + LANES:lo + MLA_QDIM], cos2, sin2).astype(BF16)


def _rope_b(y, c, s1, s2, scale):
    half = ROPE_DIM_B // 2
    for s in range(y.shape[1] // LANES):
        a = y[:, s * LANES:(s + 1) * LANES]
        r = a * c + pltpu.roll(a, LANES - half, axis=1) * s1 + pltpu.roll(a, half, axis=1) * s2
        yield s, (r * scale if scale != 1.0 else r)


def _epi_swa_q(y, extras, outs):
    c_ref, s1_ref, s2_ref = extras
    for s, r in _rope_b(y, c_ref[...], s1_ref[...], s2_ref[...], SWA_SCALE):
        outs[0][:, s * LANES:(s + 1) * LANES] = r.astype(BF16)


def _epi_shared_kv(y, extras, outs):
    c_ref, s1_ref, s2_ref = extras
    k_ref, v_ref, kb_ref, vb_ref = outs
    for s, r in _rope_b(y[:, :SWA_KV_DIM], c_ref[...], s1_ref[...], s2_ref[...], 1.0):
        k_ref[:, s * LANES:(s + 1) * LANES] = r
        kb_ref[:, s * LANES:(s + 1) * LANES] = r.astype(BF16)
    v = y[:, SWA_KV_DIM:]
    v_ref[...] = v
    vb_ref[...] = v.astype(BF16)


def _row_spec(tm, width):
    return pl.BlockSpec((tm, width), lambda i, j: (i, 0))


def _tile_spec(tm, tn):
    return pl.BlockSpec((tm, tn), lambda i, j: (i, j))


def _const_spec(width):
    return pl.BlockSpec((1, width), lambda i, j: (0, 0))


def _ffn_a_kernel(x_ref, wg_ref, wu_ref, cw_ref, cb_ref, prev_ref, act_ref, gl_ref, gbuf, carry,
                  *, seg, nseg, tiles_per_seq):
    i = pl.program_id(0)
    j = pl.program_id(1)
    x = x_ref[...]
    g = jnp.dot(x, wg_ref[...], preferred_element_type=F32)
    u = jnp.dot(x, wu_ref[...], preferred_element_type=F32)
    w0 = cw_ref[0:1, :]
    w1 = cw_ref[1:2, :]
    w2 = cw_ref[2:3, :]
    b = cb_ref[...]
    for s in range(nseg):
        base = s * (SUBLANES + seg)
        gs = g[s * seg:(s + 1) * seg]
        gl_ref[s] = gs[seg - SUBLANES:]
        if tiles_per_seq > 1:
            @pl.when(i % tiles_per_seq == 0)
            def _():
                gbuf[base:base + SUBLANES] = prev_ref[s]

            @pl.when(i % tiles_per_seq != 0)
            def _():
                gbuf[base:base + SUBLANES] = carry[j]
        else:
            gbuf[base:base + SUBLANES] = prev_ref[s]
        gbuf[base + SUBLANES:base + SUBLANES + seg] = gs
        g1 = gbuf[base + SUBLANES - 1:base + SUBLANES - 1 + seg]
        g2 = gbuf[base + SUBLANES - 2:base + SUBLANES - 2 + seg]
        gc = b + w0 * g2 + w1 * g1 + w2 * gs
        a = gc * jax.nn.sigmoid(gc) * u[s * seg:(s + 1) * seg]
        act_ref[s * seg:(s + 1) * seg, :] = a.astype(BF16)
    if tiles_per_seq > 1:
        carry[j] = g[g.shape[0] - SUBLANES:]


def ffn_a(hn, wg, wu, cw, cb, prev, seq_len, *, tm_cap=1024, tf=256):
    m, d = hn.shape
    f = wg.shape[1]
    tm = _tile(m, tm_cap)
    seg = min(seq_len, tm)
    nseg = tm // seg
    tiles_per_seq = seq_len // seg
    nj = f // tf
    if nseg == 1:
        prev_map = lambda i, j: (i // tiles_per_seq, 0, j)
    else:
        prev_map = lambda i, j: (i, 0, j)
    return pl.pallas_call(
        functools.partial(_ffn_a_kernel, seg=seg, nseg=nseg, tiles_per_seq=tiles_per_seq),
        grid=(m // tm, nj),
        in_specs=[pl.BlockSpec((tm, d), lambda i, j: (i, 0)),
                  pl.BlockSpec((d, tf), lambda i, j: (0, j)),
                  pl.BlockSpec((d, tf), lambda i, j: (0, j)),
                  pl.BlockSpec((CONV_W, tf), lambda i, j: (0, j)),
                  pl.BlockSpec((1, tf), lambda i, j: (0, j)),
                  pl.BlockSpec((nseg, SUBLANES, tf), prev_map)],
        out_specs=[pl.BlockSpec((tm, tf), lambda i, j: (i, j)),
                   pl.BlockSpec((nseg, SUBLANES, tf), lambda i, j: (i, 0, j))],
        out_shape=[jax.ShapeDtypeStruct((m, f), BF16),
                   jax.ShapeDtypeStruct((m // seg, SUBLANES, f), F32)],
        scratch_shapes=[pltpu.VMEM((nseg * (SUBLANES + seg), tf), F32),
                        pltpu.VMEM((nj, SUBLANES, tf), F32)],
        compiler_params=_params(2),
        name="ffn_gate_up_conv",
    )(hn, wg, wu, cw, cb.reshape(1, f), prev)


def _mla_attn_kernel(q_ref, kn_ref, kr_ref, v_ref, o_ref, m_ref, l_ref, acc_ref,
                     *, hb, tq, tk, nk, q_off, k_valid):
    qi = pl.program_id(2)
    ki = pl.program_id(3)
    q_lo = q_off + qi * tq
    k_lo = ki * tk

    @pl.when(ki == 0)
    def _():
        m_ref[...] = jnp.full(m_ref.shape, NEG_INF, F32)
        l_ref[...] = jnp.zeros(l_ref.shape, F32)
        acc_ref[...] = jnp.zeros(acc_ref.shape, F32)

    needed = k_lo // CHUNK <= (q_lo + tq - 1) // CHUNK
    full = jnp.logical_and((k_lo + tk - 1) // CHUNK <= q_lo // CHUNK, k_lo + tk <= k_valid)

    def step(masked):
        kr = kr_ref[0]
        if masked:
            qc = (q_lo + lax.broadcasted_iota(jnp.int32, (tq, tk), 0)) // CHUNK
            kp = k_lo + lax.broadcasted_iota(jnp.int32, (tq, tk), 1)
            visible = jnp.logical_and(qc >= kp // CHUNK, kp < k_valid)
        for h in range(hb):
            q = q_ref[0, :, h * MLA_QDIM:(h + 1) * MLA_QDIM]
            k = jnp.concatenate([kn_ref[0, :, h * QK_NOPE:(h + 1) * QK_NOPE], kr], axis=1)
            s = lax.dot_general(q, k, (((1,), (1,)), ((), ())), preferred_element_type=F32)
            if masked:
                s = jnp.where(visible, s, NEG_INF)
            m_prev = m_ref[h]
            m_new = jnp.maximum(m_prev, jnp.max(s, axis=-1, keepdims=True))
            alpha = jnp.exp(m_prev - m_new)
            p = jnp.exp(s - m_new)
            l_ref[h] = alpha * l_ref[h] + jnp.sum(p, axis=-1, keepdims=True)
            pv = jnp.dot(p.astype(BF16), v_ref[0, :, h * V_DIM:(h + 1) * V_DIM],
                         preferred_element_type=F32)
            acc_ref[h] = alpha * acc_ref[h] + pv
            m_ref[h] = m_new

    @pl.when(jnp.logical_and(needed, full))
    def _():
        step(False)

    @pl.when(jnp.logical_and(needed, jnp.logical_not(full)))
    def _():
        step(True)

    @pl.when(ki == nk - 1)
    def _():
        for h in range(hb):
            o_ref[0, :, h * V_DIM:(h + 1) * V_DIM] = (acc_ref[h] / l_ref[h]).astype(BF16)


def mla_attention(q, kv, kr, *, q_off, k_valid, tq, tk, hb=4):
    bsz, sq, _ = q.shape
    sk = kv.shape[1]
    nq, nk = sq // tq, sk // tk
    n_hg = MLA_HEADS // hb

    def k_idx(qi, ki):
        last = ((q_off + (qi + 1) * tq - 1) // CHUNK * CHUNK + CHUNK - 1) // tk
        return jnp.minimum(ki, jnp.minimum(last, nk - 1))

    return pl.pallas_call(
        functools.partial(_mla_attn_kernel, hb=hb, tq=tq, tk=tk, nk=nk, q_off=q_off,
                          k_valid=k_valid),
        grid=(bsz, n_hg, nq, nk),
        in_specs=[pl.BlockSpec((1, tq, hb * MLA_QDIM), lambda b, g, qi, ki: (b, qi, g)),
                  pl.BlockSpec((1, tk, hb * QK_NOPE), lambda b, g, qi, ki: (b, k_idx(qi, ki), g)),
                  pl.BlockSpec((1, tk, LANES), lambda b, g, qi, ki: (b, k_idx(qi, ki), 0)),
                  pl.BlockSpec((1, tk, hb * V_DIM),
                               lambda b, g, qi, ki: (b, k_idx(qi, ki), n_hg + g))],
        out_specs=pl.BlockSpec((1, tq, hb * V_DIM), lambda b, g, qi, ki: (b, qi, g)),
        out_shape=jax.ShapeDtypeStruct((bsz, sq, MLA_HEADS * V_DIM), BF16),
        scratch_shapes=[pltpu.VMEM((hb, tq, 1), F32),
                        pltpu.VMEM((hb, tq, 1), F32),
                        pltpu.VMEM((hb, tq, V_DIM), F32)],
        compiler_params=_params(4),
        name="mla_flash_attention",
    )(q, kv, kr, kv)


def _swa_attn_kernel(sink_ref, q_ref, ka_ref, kb_ref, va_ref, vb_ref, o_ref, *, tq, k_valid):
    t = pl.program_id(1)
    k = jnp.concatenate([ka_ref[0], kb_ref[0]], axis=0)
    v = jnp.concatenate([va_ref[0], vb_ref[0]], axis=0)
    tk = k.shape[0]
    if k_valid is None:
        qc = (t * tq + lax.broadcasted_iota(jnp.int32, (tq, tk), 0)) // CHUNK
        kp = (t - 1) * tq + lax.broadcasted_iota(jnp.int32, (tq, tk), 1)
        kc = kp // CHUNK
        valid = jnp.logical_and(kp >= 0, jnp.logical_and(kc >= qc - WINDOW // CHUNK, kc <= qc))
    else:
        valid = lax.broadcasted_iota(jnp.int32, (tq, tk), 1) < k_valid
    lane_head = lax.broadcasted_iota(jnp.int32, (1, HEAD_SLAB), 1) // SWA_HEAD_DIM
    heads_per_slab = HEAD_SLAB // SWA_HEAD_DIM
    head_mask = [(lane_head == hh).astype(BF16) for hh in range(heads_per_slab)]
    for slab in range(SWA_KV_DIM // HEAD_SLAB):
        k_slab = k[:, slab * HEAD_SLAB:(slab + 1) * HEAD_SLAB]
        v_slab = v[:, slab * HEAD_SLAB:(slab + 1) * HEAD_SLAB]
        km = [k_slab * head_mask[hh] for hh in range(heads_per_slab)]
        vm = [v_slab * head_mask[hh] for hh in range(heads_per_slab)]
        for g in range(SWA_GROUP):
            lo = g * SWA_KV_DIM + slab * HEAD_SLAB
            qg = q_ref[0, :, lo:lo + HEAD_SLAB]
            acc = jnp.zeros((tq, HEAD_SLAB), F32)
            for hh in range(heads_per_slab):
                s = lax.dot_general(qg, km[hh], (((1,), (1,)), ((), ())), preferred_element_type=F32)
                s = jnp.where(valid, s, NEG_INF)
                sb = sink_ref[slab * heads_per_slab + hh, g]
                m = jnp.maximum(jnp.max(s, axis=-1, keepdims=True), sb)
                e = jnp.exp(s - m)
                p = e / (jnp.sum(e, axis=-1, keepdims=True) + jnp.exp(sb - m))
                acc = acc + jnp.dot(p.astype(BF16), vm[hh], preferred_element_type=F32)
            o_ref[0, :, lo:lo + HEAD_SLAB] = acc.astype(BF16)


def swa_attention(q, k, v, sink, *, k_valid=None):
    bsz, sq, _ = q.shape
    if k_valid is None:
        tq = WINDOW
        a_map = lambda b, t: (b, jnp.maximum(t - 1, 0), 0)
        b_map = lambda b, t: (b, t, 0)
    else:
        tq = sq
        a_map = lambda b, t: (b, 0, 0)
        b_map = lambda b, t: (b, 1, 0)
    kv_spec_a = pl.BlockSpec((1, WINDOW, SWA_KV_DIM), a_map)
    kv_spec_b = pl.BlockSpec((1, WINDOW, SWA_KV_DIM), b_map)
    return pl.pallas_call(
        functools.partial(_swa_attn_kernel, tq=tq, k_valid=k_valid),
        grid=(bsz, sq // tq),
        in_specs=[pl.BlockSpec(memory_space=pltpu.SMEM),
                  pl.BlockSpec((1, tq, D_MODEL), lambda b, t: (b, t, 0)),
                  kv_spec_a, kv_spec_b, kv_spec_a, kv_spec_b],
        out_specs=pl.BlockSpec((1, tq, D_MODEL), lambda b, t: (b, t, 0)),
        out_shape=jax.ShapeDtypeStruct((bsz, sq, D_MODEL), BF16),
        compiler_params=_params(2),
        name="swa_sink_attention",
    )(sink, q, k, k, v, v)


def _rope_cols(base):
    half = QK_ROPE // 2
    x1 = base + jnp.arange(half)
    x2 = base + half + jnp.arange(half)
    return jnp.concatenate([x1, x2, x2, x1])


def _mla_tables(pos):
    half = QK_ROPE // 2
    inv = jnp.power(jnp.float32(ROPE_THETA), -jnp.arange(half, dtype=F32) * (2.0 / QK_ROPE))
    ang = pos.astype(F32)[:, None] * inv[None, :]
    c, s = jnp.cos(ang), jnp.sin(ang)
    z = jnp.zeros_like(c)
    return jnp.concatenate([c, c, z, z], axis=1), jnp.concatenate([-s, s, z, z], axis=1)


def _swa_tables(pos):
    half = ROPE_DIM_B // 2
    inv = jnp.power(jnp.float32(ROPE_THETA), -jnp.arange(half, dtype=F32) * (2.0 / ROPE_DIM_B))
    ang = pos.astype(F32)[:, None] * inv[None, :]
    c, s = jnp.cos(ang), jnp.sin(ang)
    n = pos.shape[0]
    rest = SWA_HEAD_DIM - ROPE_DIM_B
    c64 = jnp.concatenate([c, c, jnp.ones((n, rest), F32)], axis=1)
    s1 = jnp.concatenate([-s, jnp.zeros((n, SWA_HEAD_DIM - half), F32)], axis=1)
    s2 = jnp.concatenate([jnp.zeros((n, half), F32), s, jnp.zeros((n, rest), F32)], axis=1)
    rep = LANES // SWA_HEAD_DIM
    return jnp.tile(c64, (1, rep)), jnp.tile(s1, (1, rep)), jnp.tile(s2, (1, rep))


def _prep_weights(p):
    w = {}
    a_cols = jnp.concatenate([jnp.arange(Q_LORA + KV_LORA), _rope_cols(Q_LORA + KV_LORA)])
    w['mla_w_a'] = p['mla_w_a'][:, :, a_cols].astype(BF16)
    qk = QK_NOPE + QK_ROPE
    q_cols = jnp.concatenate([jnp.concatenate([h * qk + jnp.arange(QK_NOPE), _rope_cols(h * qk + QK_NOPE)])
                              for h in range(MLA_HEADS)])
    w['mla_w_uq'] = p['mla_w_uq'][:, :, q_cols].astype(BF16)
    w['mla_w_ukv'] = jnp.concatenate(
        [p['mla_w_uk'].reshape(N_A, KV_LORA, MLA_HEADS * QK_NOPE),
         p['mla_w_uv'].reshape(N_A, KV_LORA, MLA_HEADS * V_DIM)], axis=2).astype(BF16)
    w['mla_w_o'] = p['mla_w_o'].astype(BF16)
    w['swa_w_kv'] = p['swa_w_kv'].astype(BF16)
    wq = p['swa_w_q'].reshape(N_B, D_MODEL, SWA_KV_HEADS, SWA_GROUP, SWA_HEAD_DIM)
    w['swa_w_q'] = wq.transpose(0, 1, 3, 2, 4).reshape(N_B, D_MODEL, D_MODEL).astype(BF16)
    wo = p['swa_w_o'].reshape(N_B, SWA_KV_HEADS, SWA_GROUP, SWA_HEAD_DIM, D_MODEL)
    w['swa_w_o'] = wo.transpose(0, 2, 1, 3, 4).reshape(N_B, D_MODEL, D_MODEL).astype(BF16)
    w['swa_sinks'] = p['swa_sinks'].reshape(N_B, SWA_KV_HEADS, SWA_GROUP)
    w['ffn_w_gate'] = p['ffn_w_gate'].astype(BF16)
    w['ffn_w_up'] = p['ffn_w_up'].astype(BF16)
    w['ffn_w_down'] = p['ffn_w_down'].astype(BF16)
    return w


def _trunk(x, pos, p, w, cache):
    bsz, seq, _ = x.shape
    m = bsz * seq
    h = x.reshape(m, D_MODEL)
    tm = _tile(m, 512)
    past = 0 if cache is None else cache['ckv'].shape[2]

    mla_cos, mla_sin = (jnp.tile(t, (bsz, 1)) for t in _mla_tables(pos))
    swa_c, swa_s1, swa_s2 = (jnp.tile(t, (bsz, 1)) for t in _swa_tables(pos))
    tab = _row_spec(tm, LANES)

    def out_proj(o, w_o, h_res):
        return matmul(o, w_o, tm=tm, tn=512, epilogue=_epi_residual,
                      extras=(h_res,), extra_specs=(_tile_spec(tm, 512),),
                      out_shapes=jax.ShapeDtypeStruct((m, D_MODEL), F32),
                      out_specs=_tile_spec(tm, 512), name="out_proj_residual")

    ckv_rows, krope_rows, conv_rows = [], [], []
    ks = vs = ks_b = vs_b = None
    for l in range(DEPTH):
        hn = rmsnorm(h, p['norm_attn'][l], BF16)
        if l < N_A:
            n_a = w['mla_w_a'].shape[2]
            cq, ckv, ckv_b, krope, kr_b = matmul(
                hn, w['mla_w_a'][l], tm=tm, tn=n_a, epilogue=_epi_mla_in,
                extras=(p['mla_g_q'][l].reshape(1, Q_LORA), p['mla_g_kv'][l].reshape(1, KV_LORA),
                        mla_cos, mla_sin),
                extra_specs=(_const_spec(Q_LORA), _const_spec(KV_LORA), tab, tab),
                out_shapes=[jax.ShapeDtypeStruct((m, Q_LORA), BF16),
                            jax.ShapeDtypeStruct((m, KV_LORA), F32),
                            jax.ShapeDtypeStruct((m, KV_LORA), BF16),
                            jax.ShapeDtypeStruct((m, QK_ROPE), F32),
                            jax.ShapeDtypeStruct((m, LANES), BF16)],
                out_specs=[_row_spec(tm, Q_LORA), _row_spec(tm, KV_LORA), _row_spec(tm, KV_LORA),
                           _row_spec(tm, QK_ROPE), _row_spec(tm, LANES)],
                name="mla_in_proj")
            ckv_rows.append(ckv.reshape(bsz, seq, KV_LORA))
            krope_rows.append(krope.reshape(bsz, seq, QK_ROPE))
            tn_q = 4 * MLA_QDIM
            q = matmul(cq, w['mla_w_uq'][l], tm=tm, tn=tn_q, epilogue=_epi_mla_q,
                       extras=(mla_cos, mla_sin), extra_specs=(tab, tab),
                       out_shapes=jax.ShapeDtypeStruct((m, MLA_HEADS * MLA_QDIM), BF16),
                       out_specs=_tile_spec(tm, tn_q), name="mla_q_proj")
            if cache is None:
                ckv_all = ckv_b.reshape(bsz, seq, KV_LORA)
                kr_all = kr_b.reshape(bsz, seq, LANES)
                k_valid = seq
                tq = tk = min(seq, 1024)
            else:
                k_valid = past + seq
                sk = -(-k_valid // LANES) * LANES
                ckv_all = jnp.concatenate(
                    [cache['ckv'][l].astype(BF16), ckv_b.reshape(bsz, seq, KV_LORA),
                     jnp.zeros((bsz, sk - k_valid, KV_LORA), BF16)], axis=1)
                kr_cache = jnp.pad(cache['krope'][l], ((0, 0), (0, 0), (0, LANES - QK_ROPE)))
                kr_all = jnp.concatenate(
                    [kr_cache.astype(BF16), kr_b.reshape(bsz, seq, LANES),
                     jnp.zeros((bsz, sk - k_valid, LANES), BF16)], axis=1)
                tq, tk = seq, sk
            sk = ckv_all.shape[1]
            mk = bsz * sk
            tmk = _tile(mk, 1024)
            kv = matmul(ckv_all.reshape(mk, KV_LORA), w['mla_w_ukv'][l], tm=tmk, tn=1024,
                        epilogue=_epi_cast,
                        out_shapes=jax.ShapeDtypeStruct((mk, 2 * MLA_HEADS * QK_NOPE), BF16),
                        out_specs=_tile_spec(tmk, 1024), name="mla_kv_up_proj")
            o = mla_attention(q.reshape(bsz, seq, MLA_HEADS * MLA_QDIM),
                              kv.reshape(bsz, sk, 2 * MLA_HEADS * QK_NOPE), kr_all,
                              q_off=past, k_valid=k_valid, tq=tq, tk=tk)
            h = out_proj(o.reshape(m, D_MODEL), w['mla_w_o'][l], h)
        else:
            i = l - N_A
            if l == N_A:
                hk = rmsnorm(h, p['kv_shared_norm'], BF16)
                ks, vs, ks_b, vs_b = matmul(
                    hk, w['swa_w_kv'], tm=tm, tn=2 * SWA_KV_DIM, epilogue=_epi_shared_kv,
                    extras=(swa_c, swa_s1, swa_s2), extra_specs=(tab, tab, tab),
                    out_shapes=[jax.ShapeDtypeStruct((m, SWA_KV_DIM), F32)] * 2
                    + [jax.ShapeDtypeStruct((m, SWA_KV_DIM), BF16)] * 2,
                    out_specs=[_row_spec(tm, SWA_KV_DIM)] * 4, name="swa_shared_kv_proj")
            q = matmul(hn, w['swa_w_q'][i], tm=tm, tn=1024, epilogue=_epi_swa_q,
                       extras=(swa_c, swa_s1, swa_s2), extra_specs=(tab, tab, tab),
                       out_shapes=jax.ShapeDtypeStruct((m, D_MODEL), BF16),
                       out_specs=_tile_spec(tm, 1024), name="swa_q_proj")
            q3 = q.reshape(bsz, seq, D_MODEL)
            k3 = ks_b.reshape(bsz, seq, SWA_KV_DIM)
            v3 = vs_b.reshape(bsz, seq, SWA_KV_DIM)
            sink = w['swa_sinks'][i]
            if cache is None:
                o = swa_attention(q3, k3, v3, sink)
            else:
                win = cache['swa_k'].shape[1]
                pad = jnp.zeros((bsz, 2 * WINDOW - win - seq, SWA_KV_DIM), BF16)
                k_all = jnp.concatenate(
                    [cache['swa_k'].reshape(bsz, win, SWA_KV_DIM).astype(BF16), k3, pad], axis=1)
                v_all = jnp.concatenate(
                    [cache['swa_v'].reshape(bsz, win, SWA_KV_DIM).astype(BF16), v3, pad], axis=1)
                o = swa_attention(q3, k_all, v_all, sink, k_valid=win + seq)
            h = out_proj(o.reshape(m, D_MODEL), w['swa_w_o'][i], h)

        hn = rmsnorm(h, p['norm_ffn'][l], BF16)
        if cache is None:
            prev = jnp.zeros((bsz, SUBLANES, D_FF), F32)
        else:
            prev = jnp.pad(cache['conv'][l], ((0, 0), (SUBLANES - (CONV_W - 1), 0), (0, 0)))
        act, gl = ffn_a(hn, w['ffn_w_gate'][l], w['ffn_w_up'][l], p['ffn_conv_w'][l],
                        p['ffn_conv_b'][l], prev, seq)
        gl = gl.reshape(bsz, -1, SUBLANES, D_FF)
        conv_rows.append(gl[:, -1, SUBLANES - (CONV_W - 1):, :])
        h = matmul(act, w['ffn_w_down'][l], tm=tm, tn=512, epilogue=_epi_residual,
                   extras=(h,), extra_specs=(_tile_spec(tm, 512),),
                   out_shapes=jax.ShapeDtypeStruct((m, D_MODEL), F32),
                   out_specs=_tile_spec(tm, 512), name="ffn_down_residual")

    y = rmsnorm(h, p['norm_final'], F32).reshape(bsz, seq, D_MODEL)
    ks4 = ks.reshape(bsz, seq, SWA_KV_HEADS, SWA_HEAD_DIM)
    vs4 = vs.reshape(bsz, seq, SWA_KV_HEADS, SWA_HEAD_DIM)
    if cache is None:
        win = min(WINDOW, seq)
        ks4, vs4 = ks4[:, seq - win:], vs4[:, seq - win:]
    return y, jnp.stack(ckv_rows), jnp.stack(krope_rows), ks4, vs4, jnp.stack(conv_rows)


def kernel(x_prompt, x_sample, cache_mla_ckv, cache_mla_krope, cache_swa_k, cache_swa_v, state_conv,
           norm_attn, norm_ffn, mla_w_a, mla_g_q, mla_g_kv, mla_w_uq, mla_w_uk, mla_w_uv, mla_w_o,
           kv_shared_norm, swa_w_kv, swa_w_q, swa_sinks, swa_w_o,
           ffn_w_gate, ffn_w_up, ffn_conv_w, ffn_conv_b, ffn_w_down, norm_final):
    p = {
        'norm_attn': norm_attn, 'norm_ffn': norm_ffn,
        'mla_w_a': mla_w_a, 'mla_g_q': mla_g_q, 'mla_g_kv': mla_g_kv, 'mla_w_uq': mla_w_uq,
        'mla_w_uk': mla_w_uk, 'mla_w_uv': mla_w_uv, 'mla_w_o': mla_w_o,
        'kv_shared_norm': kv_shared_norm, 'swa_w_kv': swa_w_kv, 'swa_w_q': swa_w_q,
        'swa_sinks': swa_sinks, 'swa_w_o': swa_w_o,
        'ffn_w_gate': ffn_w_gate, 'ffn_w_up': ffn_w_up, 'ffn_conv_w': ffn_conv_w,
        'ffn_conv_b': ffn_conv_b, 'ffn_w_down': ffn_w_down, 'norm_final': norm_final,
    }
    w = _prep_weights(p)
    pos_p = jnp.arange(x_prompt.shape[1])
    out_p = _trunk(x_prompt, pos_p, p, w, None)
    past = cache_mla_ckv.shape[2]
    pos_s = past + jnp.arange(x_sample.shape[1])
    cache = {'ckv': cache_mla_ckv, 'krope': cache_mla_krope, 'swa_k': cache_swa_k,
             'swa_v': cache_swa_v, 'conv': state_conv}
    out_s = _trunk(x_sample, pos_s, p, w, cache)
    return (out_p[0], out_s[0]) + out_p[1:] + out_s[1:]
```

```python
import functools

import jax
import jax.numpy as jnp
from jax import lax
from jax.experimental import pallas as pl
from jax.experimental.pallas import tpu as pltpu

D_MODEL = 4096
DEPTH = 4
CHUNK = 64
N_A = DEPTH // 2
N_B = DEPTH - N_A
ROPE_THETA = 500000.0
EPS = 1e-6
NEG_INF = -1e30
MLA_HEADS = D_MODEL // 128
Q_LORA = D_MODEL // 4
KV_LORA = 512
QK_NOPE = 128
QK_ROPE = 64
V_DIM = 128
MLA_SCALE = (QK_NOPE + QK_ROPE) ** -0.5
LOG2E = 1.4426950408889634
MLA_Q_SCALE = MLA_SCALE * LOG2E
SWA_HEAD_DIM = 64
SWA_HEADS = D_MODEL // SWA_HEAD_DIM
SWA_KV_HEADS = 8
SWA_GROUP = SWA_HEADS // SWA_KV_HEADS
WINDOW = 128
ROPE_DIM_B = SWA_HEAD_DIM // 4
SWA_SCALE = SWA_HEAD_DIM ** -0.5
D_FF = 256 * ((8 * D_MODEL // 3 + 255) // 256)
CONV_W = 3
FFN_TF = 512
F_PAD = -(-D_FF // FFN_TF) * FFN_TF

LANES = 128
SUBLANES = 8
V7X_VMEM_LIMIT = 56 * 1024 * 1024

MLA_QDIM = 2 * LANES
DENOM_ROWS = 16
SWA_KV_DIM = SWA_KV_HEADS * SWA_HEAD_DIM
HEAD_SLAB = 2 * LANES
BF16 = jnp.bfloat16
F32 = jnp.float32


def _params(n_grid):
    return pltpu.CompilerParams(dimension_semantics=("arbitrary",) * n_grid,
                                vmem_limit_bytes=V7X_VMEM_LIMIT)


def _tile(m, cap):
    for step in (LANES, 16):
        t = (min(cap, m) // step) * step
        while t >= step:
            if m % t == 0:
                return t
            t -= step
    return m


def _rms(x, g):
    return x * lax.rsqrt(jnp.mean(x * x, axis=-1, keepdims=True) + EPS) * g


def _rmsnorm_kernel(x_ref, g_ref, o_ref):
    o_ref[...] = _rms(x_ref[...], g_ref[...]).astype(o_ref.dtype)


def rmsnorm(x, g, out_dtype):
    m, d = x.shape
    tm = _tile(m, 256)
    return pl.pallas_call(
        _rmsnorm_kernel,
        grid=(m // tm,),
        in_specs=[pl.BlockSpec((tm, d), lambda i: (i, 0)),
                  pl.BlockSpec((1, d), lambda i: (0, 0))],
        out_specs=pl.BlockSpec((tm, d), lambda i: (i, 0)),
        out_shape=jax.ShapeDtypeStruct((m, d), out_dtype),
        compiler_params=_params(1),
        name="rmsnorm",
    )(x, g.reshape(1, d))


def _mm_kernel(x_ref, w_ref, *rest, epilogue, n_extra, norm_x):
    extras = rest[:n_extra]
    if norm_x:
        x = _rms(x_ref[...], extras[-1][...]).astype(BF16)
    else:
        x = x_ref[...]
    y = jnp.dot(x, w_ref[...], preferred_element_type=F32)
    epilogue(y, extras, rest[n_extra:])


def matmul(x, w, *, tm, tn, epilogue, extras=(), extra_specs=(), out_shapes, out_specs, name,
           norm_x=False):
    m, k = x.shape
    n = w.shape[1]
    assert not norm_x or tn == n
    w_mode = dict(pipeline_mode=pl.Buffered(1)) if tn == n else {}
    return pl.pallas_call(
        functools.partial(_mm_kernel, epilogue=epilogue, n_extra=len(extras), norm_x=norm_x),
        grid=(m // tm, n // tn),
        in_specs=[pl.BlockSpec((tm, k), lambda i, j: (i, 0)),
                  pl.BlockSpec((k, tn), lambda i, j: (0, j), **w_mode)] + list(extra_specs),
        out_specs=out_specs,
        out_shape=out_shapes,
        compiler_params=_params(2),
        name=name,
    )(x, w, *extras)


def _epi_cast(y, extras, outs):
    outs[0][...] = y.astype(outs[0].dtype)


def _epi_residual(y, extras, outs):
    outs[0][...] = extras[0][...] + y


def _rope_slab(a, cos2, sin2):
    return a * cos2 + pltpu.roll(a, QK_ROPE, axis=1) * sin2


def _epi_mla_in(y, extras, outs):
    gq_ref, gkv_ref, cos_ref, sin_ref = extras[:4]
    cq_ref, ckv_ref, ckvb_ref, kr_ref, krb_ref = outs
    cq_ref[...] = _rms(y[:, :Q_LORA], gq_ref[...]).astype(BF16)
    ckv = _rms(y[:, Q_LORA:Q_LORA + KV_LORA], gkv_ref[...])
    ckv_ref[...] = ckv
    ckvb_ref[...] = ckv.astype(BF16)
    kr = _rope_slab(y[:, Q_LORA + KV_LORA:], cos_ref[...], sin_ref[...])
    kr_ref[...] = kr[:, :QK_ROPE]
    krb_ref[...] = kr.astype(BF16)


def _epi_mla_q(y, extras, outs):
    cos_ref, sin_ref = extras
    cos2 = cos_ref[...] * MLA_Q_SCALE
    sin2 = sin_ref[...] * MLA_Q_SCALE
    for s in range(y.shape[1] // MLA_QDIM):
        lo = s * MLA_QDIM
        outs[0][:, lo:lo + LANES] = (y[:, lo:lo + LANES] * MLA_Q_SCALE).astype(BF16)
        outs[0][:, lo + LANES:lo + MLA_QDIM] = _rope_slab(
            y[:, lo + LANES:lo + MLA_QDIM], cos2, sin2).astype(BF16)


def _rope_b(y, c, s1, s2, scale):
    half = ROPE_DIM_B // 2
    for s in range(y.shape[1] // LANES):
        a = y[:, s * LANES:(s + 1) * LANES]
        r = a * c + pltpu.roll(a, LANES - half, axis=1) * s1 + pltpu.roll(a, half, axis=1) * s2
        yield s, (r * scale if scale != 1.0 else r)


def _epi_swa_q(y, extras, outs):
    c_ref, s1_ref, s2_ref = extras
    for s, r in _rope_b(y, c_ref[...], s1_ref[...], s2_ref[...], SWA_SCALE):
        outs[0][:, s * LANES:(s + 1) * LANES] = r.astype(BF16)


def _epi_shared_kv(y, extras, outs):
    c_ref, s1_ref, s2_ref = extras[:3]
    k_ref, v_ref, kb_ref, vb_ref = outs
    for s, r in _rope_b(y[:, :SWA_KV_DIM], c_ref[...], s1_ref[...], s2_ref[...], 1.0):
        k_ref[:, s * LANES:(s + 1) * LANES] = r
        kb_ref[:, s * LANES:(s + 1) * LANES] = r.astype(BF16)
    v = y[:, SWA_KV_DIM:]
    v_ref[...] = v
    vb_ref[...] = v.astype(BF16)


def _row_spec(tm, width):
    return pl.BlockSpec((tm, width), lambda i, j: (i, 0))


def _tile_spec(tm, tn):
    return pl.BlockSpec((tm, tn), lambda i, j: (i, j))


def _const_spec(width):
    return pl.BlockSpec((1, width), lambda i, j: (0, 0))


def _ffn_a_kernel(x_ref, wg_ref, wu_ref, cw_ref, cb_ref, prev_ref, act_ref, gl_ref, gbuf, carry,
                  *, seg, nseg, tiles_per_seq, sub):
    i = pl.program_id(0)
    j = pl.program_id(1)
    if tiles_per_seq > 1:
        @pl.when(jnp.logical_and(i == 0, j == 0))
        def _():
            carry[...] = jnp.zeros(carry.shape, F32)

        gbuf[0:SUBLANES] = jnp.where(i % tiles_per_seq == 0, prev_ref[0], carry[j])
    else:
        for s in range(nseg):
            gbuf[s * (SUBLANES + seg):s * (SUBLANES + seg) + SUBLANES] = prev_ref[s]
    x = x_ref[...]
    for c in range(act_ref.shape[1] // sub):
        cols = slice(c * sub, (c + 1) * sub)
        g = jnp.dot(x, wg_ref[:, cols], preferred_element_type=F32)
        u = jnp.dot(x, wu_ref[:, cols], preferred_element_type=F32)
        w0 = cw_ref[0:1, cols]
        w1 = cw_ref[1:2, cols]
        w2 = cw_ref[2:3, cols]
        b = cb_ref[:, cols]
        for s in range(nseg):
            base = s * (SUBLANES + seg)
            gs = g[s * seg:(s + 1) * seg]
            gl_ref[s, :, cols] = gs[seg - SUBLANES:]
            gbuf[base + SUBLANES:base + SUBLANES + seg, cols] = gs
            g1 = gbuf[base + SUBLANES - 1:base + SUBLANES - 1 + seg, cols]
            g2 = gbuf[base + SUBLANES - 2:base + SUBLANES - 2 + seg, cols]
            gc = b + w0 * g2 + w1 * g1 + w2 * gs
            a = gc * jax.nn.sigmoid(gc) * u[s * seg:(s + 1) * seg]
            act_ref[s * seg:(s + 1) * seg, cols] = a.astype(BF16)
        if tiles_per_seq > 1:
            carry[j, :, cols] = g[g.shape[0] - SUBLANES:]


def ffn_a(hn, wg, wu, cw, cb, prev, seq_len, *, tm_cap=1024, tf=FFN_TF, sub=2 * LANES):
    m, d = hn.shape
    f = wg.shape[1]
    tm = _tile(m, tm_cap)
    seg = min(seq_len, tm)
    nseg = tm // seg
    tiles_per_seq = seq_len // seg
    nj = f // tf
    if nseg == 1:
        prev_map = lambda i, j: (i // tiles_per_seq, 0, j)
    else:
        prev_map = lambda i, j: (i, 0, j)
    return pl.pallas_call(
        functools.partial(_ffn_a_kernel, seg=seg, nseg=nseg, tiles_per_seq=tiles_per_seq, sub=sub),
        grid=(m // tm, nj),
        in_specs=[pl.BlockSpec((tm, d), lambda i, j: (i, 0)),
                  pl.BlockSpec((d, tf), lambda i, j: (0, j)),
                  pl.BlockSpec((d, tf), lambda i, j: (0, j)),
                  pl.BlockSpec((CONV_W, tf), lambda i, j: (0, j)),
                  pl.BlockSpec((1, tf), lambda i, j: (0, j)),
                  pl.BlockSpec((nseg, SUBLANES, tf), prev_map)],
        out_specs=[pl.BlockSpec((tm, tf), lambda i, j: (i, j)),
                   pl.BlockSpec((nseg, SUBLANES, tf), lambda i, j: (i, 0, j))],
        out_shape=[jax.ShapeDtypeStruct((m, f), BF16),
                   jax.ShapeDtypeStruct((m // seg, SUBLANES, f), F32)],
        scratch_shapes=[pltpu.VMEM((nseg * (SUBLANES + seg), tf), F32),
                        pltpu.VMEM((nj, SUBLANES, tf), F32)],
        compiler_params=_params(2),
        name="ffn_gate_up_conv",
    )(hn, wg, wu, cw, cb.reshape(1, f), prev)


def _mla_attn_kernel(q_ref, kn_ref, kr_ref, vt_ref, o_ref, m_ref, acc_ref, s_ref, p_ref,
                     *, hb, tq, tk, nk, q_off, k_valid, cg, rb):
    qi = pl.program_id(2)
    ki = pl.program_id(3)
    q_lo = q_off + qi * tq
    k_lo = ki * tk

    @pl.when(ki == 0)
    def _():
        m_ref[...] = jnp.full(m_ref.shape, NEG_INF, F32)
        acc_ref[...] = jnp.zeros(acc_ref.shape, F32)

    needed = k_lo // CHUNK <= (q_lo + tq - 1) // CHUNK
    full = jnp.logical_and((k_lo + tk - 1) // CHUNK <= q_lo // CHUNK, k_lo + tk <= k_valid)

    def step(masked):
        kr = kr_ref[0]
        ones = jnp.ones((DENOM_ROWS, tk), BF16)
        for h in range(hb):
            buf = h % s_ref.shape[0]
            q = q_ref[0, :, h * MLA_QDIM:(h + 1) * MLA_QDIM]
            k = jnp.concatenate([kn_ref[0, :, h * QK_NOPE:(h + 1) * QK_NOPE], kr], axis=1)
            s_ref[buf] = lax.dot_general(k, q, (((1,), (1,)), ((), ())), preferred_element_type=F32)
            alphas = []
            for c in range(tq // cg):
                cols = slice(c * cg, (c + 1) * cg)
                if masked:
                    qc = (q_lo + c * cg + lax.broadcasted_iota(jnp.int32, (1, cg), 1)) // CHUNK

                def scores(r):
                    s = s_ref[buf, r * rb:(r + 1) * rb, cols]
                    if masked:
                        kp = k_lo + r * rb + lax.broadcasted_iota(jnp.int32, (rb, 1), 0)
                        s = jnp.where(jnp.logical_and(qc >= kp // CHUNK, kp < k_valid), s, NEG_INF)
                    return s

                part = None
                for r in range(tk // rb):
                    blk = jnp.max(scores(r).reshape(rb // SUBLANES, SUBLANES, cg), axis=0)
                    part = blk if part is None else jnp.maximum(part, blk)
                m_prev = m_ref[h, :, cols]
                m_new = jnp.maximum(m_prev, jnp.max(part, axis=0, keepdims=True))
                alphas.append(jnp.exp2(m_prev - m_new))
                m_ref[h, :, cols] = m_new
                for r in range(tk // rb):
                    p_ref[buf, r * rb:(r + 1) * rb, cols] = jnp.exp2(scores(r) - m_new).astype(BF16)
            vt = jnp.concatenate([vt_ref[0, h * V_DIM:(h + 1) * V_DIM, :], ones], axis=0)
            pv = jnp.dot(vt, p_ref[buf], preferred_element_type=F32)
            acc_ref[h] = jnp.concatenate(alphas, axis=1) * acc_ref[h] + pv

    @pl.when(jnp.logical_and(needed, full))
    def _():
        step(False)

    @pl.when(jnp.logical_and(needed, jnp.logical_not(full)))
    def _():
        step(True)

    @pl.when(ki == nk - 1)
    def _():
        for h in range(hb):
            o_t = acc_ref[h, :V_DIM, :] / acc_ref[h, V_DIM:V_DIM + 1, :]
            o_ref[0, :, h * V_DIM:(h + 1) * V_DIM] = o_t.T.astype(BF16)


def mla_attention(q, kn, kr, vt, *, q_off, k_valid, tq, tk, hb=4):
    bsz, sq, _ = q.shape
    sk = kn.shape[1]
    nq, nk = sq // tq, sk // tk
    n_hg = MLA_HEADS // hb

    def k_idx(qi, ki):
        last = ((q_off + (qi + 1) * tq - 1) // CHUNK * CHUNK + CHUNK - 1) // tk
        return jnp.minimum(ki, jnp.minimum(last, nk - 1))

    return pl.pallas_call(
        functools.partial(_mla_attn_kernel, hb=hb, tq=tq, tk=tk, nk=nk, q_off=q_off,
                          k_valid=k_valid, cg=min(tq, 2 * LANES), rb=LANES),
        grid=(bsz, n_hg, nq, nk),
        in_specs=[pl.BlockSpec((1, tq, hb * MLA_QDIM), lambda b, g, qi, ki: (b, qi, g)),
                  pl.BlockSpec((1, tk, hb * QK_NOPE), lambda b, g, qi, ki: (b, k_idx(qi, ki), g)),
                  pl.BlockSpec((1, tk, LANES), lambda b, g, qi, ki: (b, k_idx(qi, ki), 0)),
                  pl.BlockSpec((1, hb * V_DIM, tk), lambda b, g, qi, ki: (b, g, k_idx(qi, ki)))],
        out_specs=pl.BlockSpec((1, tq, hb * V_DIM), lambda b, g, qi, ki: (b, qi, g)),
        out_shape=jax.ShapeDtypeStruct((bsz, sq, MLA_HEADS * V_DIM), BF16),
        scratch_shapes=[pltpu.VMEM((hb, 1, tq), F32),
                        pltpu.VMEM((hb, V_DIM + DENOM_ROWS, tq), F32),
                        pltpu.VMEM((2, tk, tq), F32),
                        pltpu.VMEM((2, tk, tq), BF16)],
        compiler_params=_params(4),
        name="mla_flash_attention",
    )(q, kn, kr, vt)


def _nt_kernel(w_ref, x_ref, o_ref):
    o_ref[0] = lax.dot_general(w_ref[...], x_ref[0], (((1,), (1,)), ((), ())),
                               preferred_element_type=F32).astype(o_ref.dtype)


def matmul_nt(wt, x, *, tn, ts):
    n, k = wt.shape
    bsz, s, _ = x.shape
    return pl.pallas_call(
        _nt_kernel,
        grid=(bsz, s // ts, n // tn),
        in_specs=[pl.BlockSpec((tn, k), lambda b, i, j: (j, 0)),
                  pl.BlockSpec((1, ts, k), lambda b, i, j: (b, i, 0))],
        out_specs=pl.BlockSpec((1, tn, ts), lambda b, i, j: (b, j, i)),
        out_shape=jax.ShapeDtypeStruct((bsz, n, s), BF16),
        compiler_params=_params(3),
        name="mla_v_up_proj_t",
    )(wt, x)


def _swa_attn_kernel(sink_ref, q_ref, ka_ref, kb_ref, va_ref, vb_ref, o_ref, *, tq, k_valid):
    t = pl.program_id(1)
    k = jnp.concatenate([ka_ref[0], kb_ref[0]], axis=0)
    v = jnp.concatenate([va_ref[0], vb_ref[0]], axis=0)
    tk = k.shape[0]
    if k_valid is None:
        qc = (t * tq + lax.broadcasted_iota(jnp.int32, (tq, tk), 0)) // CHUNK
        kp = (t - 1) * tq + lax.broadcasted_iota(jnp.int32, (tq, tk), 1)
        kc = kp // CHUNK
        valid = jnp.logical_and(kp >= 0, jnp.logical_and(kc >= qc - WINDOW // CHUNK, kc <= qc))
    else:
        valid = lax.broadcasted_iota(jnp.int32, (tq, tk), 1) < k_valid
    lane_head = lax.broadcasted_iota(jnp.int32, (1, HEAD_SLAB), 1) // SWA_HEAD_DIM
    heads_per_slab = HEAD_SLAB // SWA_HEAD_DIM
    head_mask = [(lane_head == hh).astype(BF16) for hh in range(heads_per_slab)]
    for slab in range(SWA_KV_DIM // HEAD_SLAB):
        k_slab = k[:, slab * HEAD_SLAB:(slab + 1) * HEAD_SLAB]
        v_slab = v[:, slab * HEAD_SLAB:(slab + 1) * HEAD_SLAB]
        km = [k_slab * head_mask[hh] for hh in range(heads_per_slab)]
        vm = [v_slab * head_mask[hh] for hh in range(heads_per_slab)]
        for g in range(SWA_GROUP):
            lo = g * SWA_KV_DIM + slab * HEAD_SLAB
            qg = q_ref[0, :, lo:lo + HEAD_SLAB]
            acc = jnp.zeros((tq, HEAD_SLAB), F32)
            for hh in range(heads_per_slab):
                s = lax.dot_general(qg, km[hh], (((1,), (1,)), ((), ())), preferred_element_type=F32)
                s = jnp.where(valid, s, NEG_INF)
                sb = sink_ref[slab * heads_per_slab + hh, g]
                m = jnp.maximum(jnp.max(s, axis=-1, keepdims=True), sb)
                e = jnp.exp(s - m)
                p = e / (jnp.sum(e, axis=-1, keepdims=True) + jnp.exp(sb - m))
                acc = acc + jnp.dot(p.astype(BF16), vm[hh], preferred_element_type=F32)
            o_ref[0, :, lo:lo + HEAD_SLAB] = acc.astype(BF16)


def swa_attention(q, k, v, sink, *, k_valid=None):
    bsz, sq, _ = q.shape
    if k_valid is None:
        tq = WINDOW
        a_map = lambda b, t: (b, jnp.maximum(t - 1, 0), 0)
        b_map = lambda b, t: (b, t, 0)
    else:
        tq = sq
        a_map = lambda b, t: (b, 0, 0)
        b_map = lambda b, t: (b, 1, 0)
    kv_spec_a = pl.BlockSpec((1, WINDOW, SWA_KV_DIM), a_map)
    kv_spec_b = pl.BlockSpec((1, WINDOW, SWA_KV_DIM), b_map)
    return pl.pallas_call(
        functools.partial(_swa_attn_kernel, tq=tq, k_valid=k_valid),
        grid=(bsz, sq // tq),
        in_specs=[pl.BlockSpec(memory_space=pltpu.SMEM),
                  pl.BlockSpec((1, tq, D_MODEL), lambda b, t: (b, t, 0)),
                  kv_spec_a, kv_spec_b, kv_spec_a, kv_spec_b],
        out_specs=pl.BlockSpec((1, tq, D_MODEL), lambda b, t: (b, t, 0)),
        out_shape=jax.ShapeDtypeStruct((bsz, sq, D_MODEL), BF16),
        compiler_params=_params(2),
        name="swa_sink_attention",
    )(sink, q, k, k, v, v)


def _rope_cols(base):
    half = QK_ROPE // 2
    x1 = base + jnp.arange(half)
    x2 = base + half + jnp.arange(half)
    return jnp.concatenate([x1, x2, x2, x1])


def _mla_tables(pos):
    half = QK_ROPE // 2
    inv = jnp.power(jnp.float32(ROPE_THETA), -jnp.arange(half, dtype=F32) * (2.0 / QK_ROPE))
    ang = pos.astype(F32)[:, None] * inv[None, :]
    c, s = jnp.cos(ang), jnp.sin(ang)
    z = jnp.zeros_like(c)
    return jnp.concatenate([c, c, z, z], axis=1), jnp.concatenate([-s, s, z, z], axis=1)


def _swa_tables(pos):
    half = ROPE_DIM_B // 2
    inv = jnp.power(jnp.float32(ROPE_THETA), -jnp.arange(half, dtype=F32) * (2.0 / ROPE_DIM_B))
    ang = pos.astype(F32)[:, None] * inv[None, :]
    c, s = jnp.cos(ang), jnp.sin(ang)
    n = pos.shape[0]
    rest = SWA_HEAD_DIM - ROPE_DIM_B
    c64 = jnp.concatenate([c, c, jnp.ones((n, rest), F32)], axis=1)
    s1 = jnp.concatenate([-s, jnp.zeros((n, SWA_HEAD_DIM - half), F32)], axis=1)
    s2 = jnp.concatenate([jnp.zeros((n, half), F32), s, jnp.zeros((n, rest), F32)], axis=1)
    rep = LANES // SWA_HEAD_DIM
    return jnp.tile(c64, (1, rep)), jnp.tile(s1, (1, rep)), jnp.tile(s2, (1, rep))


def _prep_weights(p):
    w = {}
    a_cols = jnp.concatenate([jnp.arange(Q_LORA + KV_LORA), _rope_cols(Q_LORA + KV_LORA)])
    w['mla_w_a'] = p['mla_w_a'][:, :, a_cols].astype(BF16)
    qk = QK_NOPE + QK_ROPE
    q_cols = jnp.concatenate([jnp.concatenate([h * qk + jnp.arange(QK_NOPE), _rope_cols(h * qk + QK_NOPE)])
                              for h in range(MLA_HEADS)])
    w['mla_w_uq'] = p['mla_w_uq'][:, :, q_cols].astype(BF16)
    w['mla_w_uk'] = p['mla_w_uk'].reshape(N_A, KV_LORA, MLA_HEADS * QK_NOPE).astype(BF16)
    w['mla_w_uvt'] = p['mla_w_uv'].reshape(N_A, KV_LORA, MLA_HEADS * V_DIM).transpose(0, 2, 1).astype(BF16)
    w['mla_w_o'] = p['mla_w_o'].astype(BF16)
    w['swa_w_kv'] = p['swa_w_kv'].astype(BF16)
    wq = p['swa_w_q'].reshape(N_B, D_MODEL, SWA_KV_HEADS, SWA_GROUP, SWA_HEAD_DIM)
    w['swa_w_q'] = wq.transpose(0, 1, 3, 2, 4).reshape(N_B, D_MODEL, D_MODEL).astype(BF16)
    wo = p['swa_w_o'].reshape(N_B, SWA_KV_HEADS, SWA_GROUP, SWA_HEAD_DIM, D_MODEL)
    w['swa_w_o'] = wo.transpose(0, 2, 1, 3, 4).reshape(N_B, D_MODEL, D_MODEL).astype(BF16)
    w['swa_sinks'] = p['swa_sinks'].reshape(N_B, SWA_KV_HEADS, SWA_GROUP)
    fpad = F_PAD - D_FF
    w['ffn_w_gate'] = jnp.pad(p['ffn_w_gate'].astype(BF16), ((0, 0), (0, 0), (0, fpad)))
    w['ffn_w_up'] = jnp.pad(p['ffn_w_up'].astype(BF16), ((0, 0), (0, 0), (0, fpad)))
    w['ffn_w_down'] = jnp.pad(p['ffn_w_down'].astype(BF16), ((0, 0), (0, fpad), (0, 0)))
    w['ffn_conv_w'] = jnp.pad(p['ffn_conv_w'], ((0, 0), (0, 0), (0, fpad)))
    w['ffn_conv_b'] = jnp.pad(p['ffn_conv_b'], ((0, 0), (0, fpad)))
    return w


def _trunk(x, pos, p, w, cache):
    bsz, seq, _ = x.shape
    m = bsz * seq
    h = x.reshape(m, D_MODEL)
    tm = _tile(m, 512)
    tm_l = _tile(m, 1024)
    past = 0 if cache is None else cache['ckv'].shape[2]

    mla_cos, mla_sin = (jnp.tile(t, (bsz, 1)) for t in _mla_tables(pos))
    swa_c, swa_s1, swa_s2 = (jnp.tile(t, (bsz, 1)) for t in _swa_tables(pos))
    tab = _row_spec(tm, LANES)
    tab_l = _row_spec(tm_l, LANES)

    def out_proj(o, w_o, h_res):
        return matmul(o, w_o, tm=tm_l, tn=1024, epilogue=_epi_residual,
                      extras=(h_res,), extra_specs=(_tile_spec(tm_l, 1024),),
                      out_shapes=jax.ShapeDtypeStruct((m, D_MODEL), F32),
                      out_specs=_tile_spec(tm_l, 1024), name="out_proj_residual")

    ckv_rows, krope_rows, conv_rows = [], [], []
    ks = vs = ks_b = vs_b = None
    for l in range(DEPTH):
        if l < N_A:
            n_a = w['mla_w_a'].shape[2]
            cq, ckv, ckv_b, krope, kr_b = matmul(
                h, w['mla_w_a'][l], tm=tm, tn=n_a, epilogue=_epi_mla_in, norm_x=True,
                extras=(p['mla_g_q'][l].reshape(1, Q_LORA), p['mla_g_kv'][l].reshape(1, KV_LORA),
                        mla_cos, mla_sin, p['norm_attn'][l].reshape(1, D_MODEL)),
                extra_specs=(_const_spec(Q_LORA), _const_spec(KV_LORA), tab, tab,
                             _const_spec(D_MODEL)),
                out_shapes=[jax.ShapeDtypeStruct((m, Q_LORA), BF16),
                            jax.ShapeDtypeStruct((m, KV_LORA), F32),
                            jax.ShapeDtypeStruct((m, KV_LORA), BF16),
                            jax.ShapeDtypeStruct((m, QK_ROPE), F32),
                            jax.ShapeDtypeStruct((m, LANES), BF16)],
                out_specs=[_row_spec(tm, Q_LORA), _row_spec(tm, KV_LORA), _row_spec(tm, KV_LORA),
                           _row_spec(tm, QK_ROPE), _row_spec(tm, LANES)],
                name="mla_in_proj")
            ckv_rows.append(ckv.reshape(bsz, seq, KV_LORA))
            krope_rows.append(krope.reshape(bsz, seq, QK_ROPE))
            tn_q = 8 * MLA_QDIM
            q = matmul(cq, w['mla_w_uq'][l], tm=tm_l, tn=tn_q, epilogue=_epi_mla_q,
                       extras=(mla_cos, mla_sin), extra_specs=(tab_l, tab_l),
                       out_shapes=jax.ShapeDtypeStruct((m, MLA_HEADS * MLA_QDIM), BF16),
                       out_specs=_tile_spec(tm_l, tn_q), name="mla_q_proj")
            if cache is None:
                ckv_all = ckv_b.reshape(bsz, seq, KV_LORA)
                kr_all = kr_b.reshape(bsz, seq, LANES)
                k_valid = seq
                tq = tk = min(seq, 1024)
            else:
                k_valid = past + seq
                sk = -(-k_valid // LANES) * LANES
                ckv_all = jnp.concatenate(
                    [cache['ckv'][l].astype(BF16), ckv_b.reshape(bsz, seq, KV_LORA),
                     jnp.zeros((bsz, sk - k_valid, KV_LORA), BF16)], axis=1)
                kr_cache = jnp.pad(cache['krope'][l], ((0, 0), (0, 0), (0, LANES - QK_ROPE)))
                kr_all = jnp.concatenate(
                    [kr_cache.astype(BF16), kr_b.reshape(bsz, seq, LANES),
                     jnp.zeros((bsz, sk - k_valid, LANES), BF16)], axis=1)
                tq, tk = LANES, sk
            sk = ckv_all.shape[1]
            mk = bsz * sk
            tmk = _tile(mk, 1024)
            kn = matmul(ckv_all.reshape(mk, KV_LORA), w['mla_w_uk'][l], tm=tmk, tn=2048,
                        epilogue=_epi_cast,
                        out_shapes=jax.ShapeDtypeStruct((mk, MLA_HEADS * QK_NOPE), BF16),
                        out_specs=_tile_spec(tmk, 2048), name="mla_k_up_proj")
            if sk % 512 == 0:
                vt = matmul_nt(w['mla_w_uvt'][l], ckv_all, tn=MLA_HEADS * V_DIM, ts=512)
            else:
                vt = matmul_nt(w['mla_w_uvt'][l], ckv_all, tn=1024, ts=sk)
            q3 = q.reshape(bsz, seq, MLA_HEADS * MLA_QDIM)
            if seq < tq:
                q3 = jnp.pad(q3, ((0, 0), (0, tq - seq), (0, 0)))
            o = mla_attention(q3, kn.reshape(bsz, sk, MLA_HEADS * QK_NOPE), kr_all, vt,
                              q_off=past, k_valid=k_valid, tq=tq, tk=tk)
            h = out_proj(o[:, :seq].reshape(m, D_MODEL), w['mla_w_o'][l], h)
        else:
            i = l - N_A
            if l == N_A:
                ks, vs, ks_b, vs_b = matmul(
                    h, w['swa_w_kv'], tm=tm, tn=2 * SWA_KV_DIM, epilogue=_epi_shared_kv,
                    norm_x=True,
                    extras=(swa_c, swa_s1, swa_s2, p['kv_shared_norm'].reshape(1, D_MODEL)),
                    extra_specs=(tab, tab, tab, _const_spec(D_MODEL)),
                    out_shapes=[jax.ShapeDtypeStruct((m, SWA_KV_DIM), F32)] * 2
                    + [jax.ShapeDtypeStruct((m, SWA_KV_DIM), BF16)] * 2,
                    out_specs=[_row_spec(tm, SWA_KV_DIM)] * 4, name="swa_shared_kv_proj")
            hn = rmsnorm(h, p['norm_attn'][l], BF16)
            q = matmul(hn, w['swa_w_q'][i], tm=tm_l, tn=1024, epilogue=_epi_swa_q,
                       extras=(swa_c, swa_s1, swa_s2), extra_specs=(tab_l, tab_l, tab_l),
                       out_shapes=jax.ShapeDtypeStruct((m, D_MODEL), BF16),
                       out_specs=_tile_spec(tm_l, 1024), name="swa_q_proj")
            q3 = q.reshape(bsz, seq, D_MODEL)
            k3 = ks_b.reshape(bsz, seq, SWA_KV_DIM)
            v3 = vs_b.reshape(bsz, seq, SWA_KV_DIM)
            sink = w['swa_sinks'][i]
            if cache is None:
                o = swa_attention(q3, k3, v3, sink)
            else:
                win = cache['swa_k'].shape[1]
                pad = jnp.zeros((bsz, 2 * WINDOW - win - seq, SWA_KV_DIM), BF16)
                k_all = jnp.concatenate(
                    [cache['swa_k'].reshape(bsz, win, SWA_KV_DIM).astype(BF16), k3, pad], axis=1)
                v_all = jnp.concatenate(
                    [cache['swa_v'].reshape(bsz, win, SWA_KV_DIM).astype(BF16), v3, pad], axis=1)
                o = swa_attention(q3, k_all, v_all, sink, k_valid=win + seq)
            h = out_proj(o.reshape(m, D_MODEL), w['swa_w_o'][i], h)

        hn = rmsnorm(h, p['norm_ffn'][l], BF16)
        if cache is None:
            prev = jnp.zeros((bsz, SUBLANES, F_PAD), F32)
        else:
            prev = jnp.pad(cache['conv'][l],
                           ((0, 0), (SUBLANES - (CONV_W - 1), 0), (0, F_PAD - D_FF)))
        act, gl = ffn_a(hn, w['ffn_w_gate'][l], w['ffn_w_up'][l], w['ffn_conv_w'][l],
                        w['ffn_conv_b'][l], prev, seq)
        gl = gl.reshape(bsz, -1, SUBLANES, F_PAD)
        conv_rows.append(gl[:, -1, SUBLANES - (CONV_W - 1):, :D_FF])
        h = matmul(act, w['ffn_w_down'][l], tm=tm, tn=512, epilogue=_epi_residual,
                   extras=(h,), extra_specs=(_tile_spec(tm, 512),),
                   out_shapes=jax.ShapeDtypeStruct((m, D_MODEL), F32),
                   out_specs=_tile_spec(tm, 512), name="ffn_down_residual")

    y = rmsnorm(h, p['norm_final'], F32).reshape(bsz, seq, D_MODEL)
    ks4 = ks.reshape(bsz, seq, SWA_KV_HEADS, SWA_HEAD_DIM)
    vs4 = vs.reshape(bsz, seq, SWA_KV_HEADS, SWA_HEAD_DIM)
    if cache is None:
        win = min(WINDOW, seq)
        ks4, vs4 = ks4[:, seq - win:], vs4[:, seq - win:]
    return y, jnp.stack(ckv_rows), jnp.stack(krope_rows), ks4, vs4, jnp.stack(conv_rows)


def kernel(x_prompt, x_sample, cache_mla_ckv, cache_mla_krope, cache_swa_k, cache_swa_v, state_conv,
           norm_attn, norm_ffn, mla_w_a, mla_g_q, mla_g_kv, mla_w_uq, mla_w_uk, mla_w_uv, mla_w_o,
           kv_shared_norm, swa_w_kv, swa_w_q, swa_sinks, swa_w_o,
           ffn_w_gate, ffn_w_up, ffn_conv_w, ffn_conv_b, ffn_w_down, norm_final):
    p = {
        'norm_attn': norm_attn, 'norm_ffn': norm_ffn,
        'mla_w_a': mla_w_a, 'mla_g_q': mla_g_q, 'mla_g_kv': mla_g_kv, 'mla_w_uq': mla_w_uq,
        'mla_w_uk': mla_w_uk, 'mla_w_uv': mla_w_uv, 'mla_w_o': mla_w_o,
        'kv_shared_norm': kv_shared_norm, 'swa_w_kv': swa_w_kv, 'swa_w_q': swa_w_q,
        'swa_sinks': swa_sinks, 'swa_w_o': swa_w_o,
        'ffn_w_gate': ffn_w_gate, 'ffn_w_up': ffn_w_up, 'ffn_conv_w': ffn_conv_w,
        'ffn_conv_b': ffn_conv_b, 'ffn_w_down': ffn_w_down, 'norm_final': norm_final,
    }
    w = _prep_weights(p)
    pos_p = jnp.arange(x_prompt.shape[1])
    out_p = _trunk(x_prompt, pos_p, p, w, None)
    past = cache_mla_ckv.shape[2]
    pos_s = past + jnp.arange(x_sample.shape[1])
    cache = {'ckv': cache_mla_ckv, 'krope': cache_mla_krope, 'swa_k': cache_swa_k,
             'swa_v': cache_swa_v, 'conv': state_conv}
    out_s = _trunk(x_sample, pos_s, p, w, cache)
    return (out_p[0], out_s[0]) + out_p[1:] + out_s[1:]
```

```python
import functools

import jax
import jax.numpy as jnp
from jax import lax
from jax.experimental import pallas as pl
from jax.experimental.pallas import tpu as pltpu

D_MODEL = 4096
DEPTH = 4
CHUNK = 64
N_A = DEPTH // 2
N_B = DEPTH - N_A
ROPE_THETA = 500000.0
EPS = 1e-6
NEG_INF = -1e30
MLA_HEADS = D_MODEL // 128
Q_LORA = D_MODEL // 4
KV_LORA = 512
QK_NOPE = 128
QK_ROPE = 64
V_DIM = 128
MLA_SCALE = (QK_NOPE + QK_ROPE) ** -0.5
LOG2E = 1.4426950408889634
MLA_Q_SCALE = MLA_SCALE * LOG2E
SWA_HEAD_DIM = 64
SWA_HEADS = D_MODEL // SWA_HEAD_DIM
SWA_KV_HEADS = 8
SWA_GROUP = SWA_HEADS // SWA_KV_HEADS
WINDOW = 128
ROPE_DIM_B = SWA_HEAD_DIM // 4
SWA_SCALE = SWA_HEAD_DIM ** -0.5
D_FF = 256 * ((8 * D_MODEL // 3 + 255) // 256)
CONV_W = 3
FFN_TF = 512
F_PAD = -(-D_FF // FFN_TF) * FFN_TF

LANES = 128
SUBLANES = 8
V7X_VMEM_LIMIT = 56 * 1024 * 1024

MLA_QDIM = 2 * LANES
DENOM_ROWS = 16
SWA_KV_DIM = SWA_KV_HEADS * SWA_HEAD_DIM
HEAD_SLAB = 2 * LANES
BF16 = jnp.bfloat16
F32 = jnp.float32


def _params(n_grid):
    return pltpu.CompilerParams(dimension_semantics=("arbitrary",) * n_grid,
                                vmem_limit_bytes=V7X_VMEM_LIMIT)


def _tile(m, cap):
    for step in (LANES, 16):
        t = (min(cap, m) // step) * step
        while t >= step:
            if m % t == 0:
                return t
            t -= step
    return m


def _rms(x, g):
    return x * lax.rsqrt(jnp.mean(x * x, axis=-1, keepdims=True) + EPS) * g


def _rmsnorm_kernel(x_ref, g_ref, o_ref):
    o_ref[...] = _rms(x_ref[...], g_ref[...]).astype(o_ref.dtype)


def rmsnorm(x, g, out_dtype):
    m, d = x.shape
    tm = _tile(m, 256)
    return pl.pallas_call(
        _rmsnorm_kernel,
        grid=(m // tm,),
        in_specs=[pl.BlockSpec((tm, d), lambda i: (i, 0)),
                  pl.BlockSpec((1, d), lambda i: (0, 0))],
        out_specs=pl.BlockSpec((tm, d), lambda i: (i, 0)),
        out_shape=jax.ShapeDtypeStruct((m, d), out_dtype),
        compiler_params=_params(1),
        name="rmsnorm",
    )(x, g.reshape(1, d))


def _mm_kernel(x_ref, w_ref, *rest, epilogue, n_extra, norm_x):
    extras = rest[:n_extra]
    if norm_x:
        x = _rms(x_ref[...], extras[-1][...]).astype(BF16)
    else:
        x = x_ref[...]
    y = jnp.dot(x, w_ref[...], preferred_element_type=F32)
    epilogue(y, extras, rest[n_extra:])


def matmul(x, w, *, tm, tn, epilogue, extras=(), extra_specs=(), out_shapes, out_specs, name,
           norm_x=False):
    m, k = x.shape
    n = w.shape[1]
    assert not norm_x or tn == n
    w_mode = dict(pipeline_mode=pl.Buffered(1)) if tn == n else {}
    return pl.pallas_call(
        functools.partial(_mm_kernel, epilogue=epilogue, n_extra=len(extras), norm_x=norm_x),
        grid=(m // tm, n // tn),
        in_specs=[pl.BlockSpec((tm, k), lambda i, j: (i, 0)),
                  pl.BlockSpec((k, tn), lambda i, j: (0, j), **w_mode)] + list(extra_specs),
        out_specs=out_specs,
        out_shape=out_shapes,
        compiler_params=_params(2),
        name=name,
    )(x, w, *extras)


def _epi_cast(y, extras, outs):
    outs[0][...] = y.astype(outs[0].dtype)


def _epi_residual(y, extras, outs):
    outs[0][...] = extras[0][...] + y


def _rope_slab(a, cos2, sin2):
    return a * cos2 + pltpu.roll(a, QK_ROPE, axis=1) * sin2


def _epi_mla_in(y, extras, outs):
    gq_ref, gkv_ref, cos_ref, sin_ref = extras[:4]
    cq_ref, ckv_ref, ckvb_ref, kr_ref, krb_ref = outs
    cq_ref[...] = _rms(y[:, :Q_LORA], gq_ref[...]).astype(BF16)
    ckv = _rms(y[:, Q_LORA:Q_LORA + KV_LORA], gkv_ref[...])
    ckv_ref[...] = ckv
    ckvb_ref[...] = ckv.astype(BF16)
    kr = _rope_slab(y[:, Q_LORA + KV_LORA:], cos_ref[...], sin_ref[...])
    kr_ref[...] = kr[:, :QK_ROPE]
    krb_ref[...] = kr.astype(BF16)


def _epi_mla_q(y, extras, outs):
    cos_ref, sin_ref = extras
    cos2 = cos_ref[...] * MLA_Q_SCALE
    sin2 = sin_ref[...] * MLA_Q_SCALE
    for s in range(y.shape[1] // MLA_QDIM):
        lo = s * MLA_QDIM
        outs[0][:, lo:lo + LANES] = (y[:, lo:lo + LANES] * MLA_Q_SCALE).astype(BF16)
        outs[0][:, lo + LANES:lo + MLA_QDIM] = _rope_slab(
            y[:, lo + LANES:lo + MLA_QDIM], cos2, sin2).astype(BF16)


def _rope_b(y, c, s1, s2, scale):
    half = ROPE_DIM_B // 2
    for s in range(y.shape[1] // LANES):
        a = y[:, s * LANES:(s + 1) * LANES]
        r = a * c + pltpu.roll(a, LANES - half, axis=1) * s1 + pltpu.roll(a, half, axis=1) * s2
        yield s, (r * scale if scale != 1.0 else r)


def _epi_swa_q(y, extras, outs):
    c_ref, s1_ref, s2_ref = extras
    for s, r in _rope_b(y, c_ref[...], s1_ref[...], s2_ref[...], SWA_SCALE):
        outs[0][:, s * LANES:(s + 1) * LANES] = r.astype(BF16)


def _epi_shared_kv(y, extras, outs):
    c_ref, s1_ref, s2_ref = extras[:3]
    k_ref, v_ref, kb_ref, vtb_ref = outs
    for s, r in _rope_b(y[:, :SWA_KV_DIM], c_ref[...], s1_ref[...], s2_ref[...], 1.0):
        k_ref[:, s * LANES:(s + 1) * LANES] = r
        kb_ref[:, s * LANES:(s + 1) * LANES] = r.astype(BF16)
    v = y[:, SWA_KV_DIM:]
    v_ref[...] = v
    vtb_ref[...] = v.T.astype(BF16)


def _row_spec(tm, width):
    return pl.BlockSpec((tm, width), lambda i, j: (i, 0))


def _tile_spec(tm, tn):
    return pl.BlockSpec((tm, tn), lambda i, j: (i, j))


def _const_spec(width):
    return pl.BlockSpec((1, width), lambda i, j: (0, 0))


def _ffn_a_kernel(x_ref, wg_ref, wu_ref, cw_ref, cb_ref, prev_ref, act_ref, gl_ref, gbuf, carry,
                  *, seg, nseg, tiles_per_seq, sub):
    i = pl.program_id(0)
    j = pl.program_id(1)
    if tiles_per_seq > 1:
        @pl.when(jnp.logical_and(i == 0, j == 0))
        def _():
            carry[...] = jnp.zeros(carry.shape, F32)

        gbuf[0:SUBLANES] = jnp.where(i % tiles_per_seq == 0, prev_ref[0], carry[j])
    else:
        for s in range(nseg):
            gbuf[s * (SUBLANES + seg):s * (SUBLANES + seg) + SUBLANES] = prev_ref[s]
    x = x_ref[...]
    for c in range(act_ref.shape[1] // sub):
        cols = slice(c * sub, (c + 1) * sub)
        g = jnp.dot(x, wg_ref[:, cols], preferred_element_type=F32)
        u = jnp.dot(x, wu_ref[:, cols], preferred_element_type=F32)
        w0 = cw_ref[0:1, cols]
        w1 = cw_ref[1:2, cols]
        w2 = cw_ref[2:3, cols]
        b = cb_ref[:, cols]
        for s in range(nseg):
            base = s * (SUBLANES + seg)
            gs = g[s * seg:(s + 1) * seg]
            gl_ref[s, :, cols] = gs[seg - SUBLANES:]
            gbuf[base + SUBLANES:base + SUBLANES + seg, cols] = gs
            g1 = gbuf[base + SUBLANES - 1:base + SUBLANES - 1 + seg, cols]
            g2 = gbuf[base + SUBLANES - 2:base + SUBLANES - 2 + seg, cols]
            gc = b + w0 * g2 + w1 * g1 + w2 * gs
            a = gc * jax.nn.sigmoid(gc) * u[s * seg:(s + 1) * seg]
            act_ref[s * seg:(s + 1) * seg, cols] = a.astype(BF16)
        if tiles_per_seq > 1:
            carry[j, :, cols] = g[g.shape[0] - SUBLANES:]


def ffn_a(hn, wg, wu, cw, cb, prev, seq_len, *, tm_cap=1024, tf=FFN_TF, sub=2 * LANES):
    m, d = hn.shape
    f = wg.shape[1]
    tm = _tile(m, tm_cap)
    seg = min(seq_len, tm)
    nseg = tm // seg
    tiles_per_seq = seq_len // seg
    nj = f // tf
    if nseg == 1:
        prev_map = lambda i, j: (i // tiles_per_seq, 0, j)
    else:
        prev_map = lambda i, j: (i, 0, j)
    return pl.pallas_call(
        functools.partial(_ffn_a_kernel, seg=seg, nseg=nseg, tiles_per_seq=tiles_per_seq, sub=sub),
        grid=(m // tm, nj),
        in_specs=[pl.BlockSpec((tm, d), lambda i, j: (i, 0)),
                  pl.BlockSpec((d, tf), lambda i, j: (0, j)),
                  pl.BlockSpec((d, tf), lambda i, j: (0, j)),
                  pl.BlockSpec((CONV_W, tf), lambda i, j: (0, j)),
                  pl.BlockSpec((1, tf), lambda i, j: (0, j)),
                  pl.BlockSpec((nseg, SUBLANES, tf), prev_map)],
        out_specs=[pl.BlockSpec((tm, tf), lambda i, j: (i, j)),
                   pl.BlockSpec((nseg, SUBLANES, tf), lambda i, j: (i, 0, j))],
        out_shape=[jax.ShapeDtypeStruct((m, f), BF16),
                   jax.ShapeDtypeStruct((m // seg, SUBLANES, f), F32)],
        scratch_shapes=[pltpu.VMEM((nseg * (SUBLANES + seg), tf), F32),
                        pltpu.VMEM((nj, SUBLANES, tf), F32)],
        compiler_params=_params(2),
        name="ffn_gate_up_conv",
    )(hn, wg, wu, cw, cb.reshape(1, f), prev)


def _mla_attn_kernel(q_ref, kn_ref, kr_ref, vt_ref, o_ref, m_ref, acc_ref, s_ref, p_ref,
                     *, hb, tq, tk, nk, q_off, k_valid, cg, rb):
    qi = pl.program_id(2)
    ki = pl.program_id(3)
    q_lo = q_off + qi * tq
    k_lo = ki * tk

    @pl.when(ki == 0)
    def _():
        m_ref[...] = jnp.full(m_ref.shape, NEG_INF, F32)
        acc_ref[...] = jnp.zeros(acc_ref.shape, F32)

    needed = k_lo // CHUNK <= (q_lo + tq - 1) // CHUNK
    full = jnp.logical_and((k_lo + tk - 1) // CHUNK <= q_lo // CHUNK, k_lo + tk <= k_valid)

    def step(masked):
        kr = kr_ref[0]
        ones = jnp.ones((DENOM_ROWS, tk), BF16)
        for h in range(hb):
            buf = h % s_ref.shape[0]
            q = q_ref[0, :, h * MLA_QDIM:(h + 1) * MLA_QDIM]
            k = jnp.concatenate([kn_ref[0, :, h * QK_NOPE:(h + 1) * QK_NOPE], kr], axis=1)
            s_ref[buf] = lax.dot_general(k, q, (((1,), (1,)), ((), ())), preferred_element_type=F32)
            alphas = []
            for c in range(tq // cg):
                cols = slice(c * cg, (c + 1) * cg)
                if masked:
                    qc = (q_lo + c * cg + lax.broadcasted_iota(jnp.int32, (1, cg), 1)) // CHUNK

                def scores(r):
                    s = s_ref[buf, r * rb:(r + 1) * rb, cols]
                    if masked:
                        kp = k_lo + r * rb + lax.broadcasted_iota(jnp.int32, (rb, 1), 0)
                        s = jnp.where(jnp.logical_and(qc >= kp // CHUNK, kp < k_valid), s, NEG_INF)
                    return s

                part = None
                for r in range(tk // rb):
                    blk = jnp.max(scores(r).reshape(rb // SUBLANES, SUBLANES, cg), axis=0)
                    part = blk if part is None else jnp.maximum(part, blk)
                m_prev = m_ref[h, :, cols]
                m_new = jnp.maximum(m_prev, jnp.max(part, axis=0, keepdims=True))
                alphas.append(jnp.exp2(m_prev - m_new))
                m_ref[h, :, cols] = m_new
                for r in range(tk // rb):
                    p_ref[buf, r * rb:(r + 1) * rb, cols] = jnp.exp2(scores(r) - m_new).astype(BF16)
            vt = jnp.concatenate([vt_ref[0, h * V_DIM:(h + 1) * V_DIM, :], ones], axis=0)
            pv = jnp.dot(vt, p_ref[buf], preferred_element_type=F32)
            acc_ref[h] = jnp.concatenate(alphas, axis=1) * acc_ref[h] + pv

    @pl.when(jnp.logical_and(needed, full))
    def _():
        step(False)

    @pl.when(jnp.logical_and(needed, jnp.logical_not(full)))
    def _():
        step(True)

    @pl.when(ki == nk - 1)
    def _():
        for h in range(hb):
            o_t = acc_ref[h, :V_DIM, :] / acc_ref[h, V_DIM:V_DIM + 1, :]
            o_ref[0, :, h * V_DIM:(h + 1) * V_DIM] = o_t.T.astype(BF16)


def mla_attention(q, kn, kr, vt, *, q_off, k_valid, tq, tk, hb=4):
    bsz, sq, _ = q.shape
    sk = kn.shape[1]
    nq, nk = sq // tq, sk // tk
    n_hg = MLA_HEADS // hb

    def k_idx(qi, ki):
        last = ((q_off + (qi + 1) * tq - 1) // CHUNK * CHUNK + CHUNK - 1) // tk
        return jnp.minimum(ki, jnp.minimum(last, nk - 1))

    return pl.pallas_call(
        functools.partial(_mla_attn_kernel, hb=hb, tq=tq, tk=tk, nk=nk, q_off=q_off,
                          k_valid=k_valid, cg=min(tq, 2 * LANES), rb=LANES),
        grid=(bsz, n_hg, nq, nk),
        in_specs=[pl.BlockSpec((1, tq, hb * MLA_QDIM), lambda b, g, qi, ki: (b, qi, g)),
                  pl.BlockSpec((1, tk, hb * QK_NOPE), lambda b, g, qi, ki: (b, k_idx(qi, ki), g)),
                  pl.BlockSpec((1, tk, LANES), lambda b, g, qi, ki: (b, k_idx(qi, ki), 0)),
                  pl.BlockSpec((1, hb * V_DIM, tk), lambda b, g, qi, ki: (b, g, k_idx(qi, ki)))],
        out_specs=pl.BlockSpec((1, tq, hb * V_DIM), lambda b, g, qi, ki: (b, qi, g)),
        out_shape=jax.ShapeDtypeStruct((bsz, sq, MLA_HEADS * V_DIM), BF16),
        scratch_shapes=[pltpu.VMEM((hb, 1, tq), F32),
                        pltpu.VMEM((hb, V_DIM + DENOM_ROWS, tq), F32),
                        pltpu.VMEM((2, tk, tq), F32),
                        pltpu.VMEM((2, tk, tq), BF16)],
        compiler_params=_params(4),
        name="mla_flash_attention",
    )(q, kn, kr, vt)


def _nt_kernel(w_ref, x_ref, o_ref):
    o_ref[0] = lax.dot_general(w_ref[...], x_ref[0], (((1,), (1,)), ((), ())),
                               preferred_element_type=F32).astype(o_ref.dtype)


def matmul_nt(wt, x, *, tn, ts):
    n, k = wt.shape
    bsz, s, _ = x.shape
    return pl.pallas_call(
        _nt_kernel,
        grid=(bsz, s // ts, n // tn),
        in_specs=[pl.BlockSpec((tn, k), lambda b, i, j: (j, 0)),
                  pl.BlockSpec((1, ts, k), lambda b, i, j: (b, i, 0))],
        out_specs=pl.BlockSpec((1, tn, ts), lambda b, i, j: (b, j, i)),
        out_shape=jax.ShapeDtypeStruct((bsz, n, s), BF16),
        compiler_params=_params(3),
        name="mla_v_up_proj_t",
    )(wt, x)


def _swa_attn_kernel(sink_ref, q_ref, ka_ref, kb_ref, vta_ref, vtb_ref, o_ref, ot_ref, *, tq, k_valid):
    t = pl.program_id(1)
    k = jnp.concatenate([ka_ref[0], kb_ref[0]], axis=0)
    vt = jnp.concatenate([vta_ref[0], vtb_ref[0]], axis=1)
    tk = k.shape[0]
    kp = lax.broadcasted_iota(jnp.int32, (tk, 1), 0)
    if k_valid is None:
        kp = kp + (t - 1) * tq
        kc = kp // CHUNK
        qc = (t * tq + lax.broadcasted_iota(jnp.int32, (1, tq), 1)) // CHUNK
        valid = jnp.logical_and(kp >= 0, jnp.logical_and(kc >= qc - WINDOW // CHUNK, kc <= qc))
    else:
        valid = jnp.broadcast_to(kp < k_valid, (tk, tq))
    valid = jnp.concatenate([valid] * SWA_GROUP, axis=1)
    lane_head = lax.broadcasted_iota(jnp.int32, (1, HEAD_SLAB), 1) // SWA_HEAD_DIM
    heads_per_slab = HEAD_SLAB // SWA_HEAD_DIM
    for slab in range(SWA_KV_DIM // HEAD_SLAB):
        k_slab = k[:, slab * HEAD_SLAB:(slab + 1) * HEAD_SLAB]
        qs = jnp.concatenate(
            [q_ref[0, :, g * SWA_KV_DIM + slab * HEAD_SLAB:g * SWA_KV_DIM + (slab + 1) * HEAD_SLAB]
             for g in range(SWA_GROUP)], axis=0)
        for hh in range(heads_per_slab):
            kvh = slab * heads_per_slab + hh
            km = k_slab * (lane_head == hh).astype(BF16)
            vth = vt[kvh * SWA_HEAD_DIM:(kvh + 1) * SWA_HEAD_DIM, :]
            sb = jnp.concatenate([jnp.full((1, tq), sink_ref[kvh, g], F32) for g in range(SWA_GROUP)],
                                 axis=1)
            s = lax.dot_general(km, qs, (((1,), (1,)), ((), ())), preferred_element_type=F32)
            s = jnp.where(valid, s, NEG_INF)
            m = jnp.maximum(jnp.max(s, axis=0, keepdims=True), sb)
            e = jnp.exp(s - m)
            inv = 1.0 / (jnp.sum(e, axis=0, keepdims=True) + jnp.exp(sb - m))
            o_t = jnp.dot(vth, (e * inv).astype(BF16), preferred_element_type=F32)
            for g in range(SWA_GROUP):
                row = g * SWA_KV_DIM + kvh * SWA_HEAD_DIM
                ot_ref[row:row + SWA_HEAD_DIM, :] = o_t[:, g * tq:(g + 1) * tq]
    for c in range(D_MODEL // LANES):
        o_ref[0, :, c * LANES:(c + 1) * LANES] = ot_ref[c * LANES:(c + 1) * LANES, :].T.astype(BF16)


def swa_attention(q, k, vt, sink, *, k_valid=None):
    bsz, sq, _ = q.shape
    tq = WINDOW
    if k_valid is None:
        a_idx = lambda t: jnp.maximum(t - 1, 0)
        b_idx = lambda t: t
    else:
        assert sq == tq and k.shape[1] == 2 * WINDOW
        a_idx = lambda t: 0
        b_idx = lambda t: 1
    return pl.pallas_call(
        functools.partial(_swa_attn_kernel, tq=tq, k_valid=k_valid),
        grid=(bsz, sq // tq),
        in_specs=[pl.BlockSpec(memory_space=pltpu.SMEM),
                  pl.BlockSpec((1, tq, D_MODEL), lambda b, t: (b, t, 0)),
                  pl.BlockSpec((1, WINDOW, SWA_KV_DIM), lambda b, t: (b, a_idx(t), 0)),
                  pl.BlockSpec((1, WINDOW, SWA_KV_DIM), lambda b, t: (b, b_idx(t), 0)),
                  pl.BlockSpec((1, SWA_KV_DIM, WINDOW), lambda b, t: (b, 0, a_idx(t))),
                  pl.BlockSpec((1, SWA_KV_DIM, WINDOW), lambda b, t: (b, 0, b_idx(t)))],
        out_specs=pl.BlockSpec((1, tq, D_MODEL), lambda b, t: (b, t, 0)),
        out_shape=jax.ShapeDtypeStruct((bsz, sq, D_MODEL), BF16),
        scratch_shapes=[pltpu.VMEM((D_MODEL, tq), F32)],
        compiler_params=_params(2),
        name="swa_sink_attention",
    )(sink, q, k, k, vt, vt)


def _cast_pad_cols_kernel(x_ref, o_ref):
    n = x_ref.shape[-1]
    o_ref[0, :, :n] = x_ref[0].astype(BF16)
    o_ref[0, :, n:] = jnp.zeros((o_ref.shape[1], o_ref.shape[2] - n), BF16)


def cast_pad_cols(x, n_pad, tr=256):
    nl, r, c = x.shape
    return pl.pallas_call(
        _cast_pad_cols_kernel,
        grid=(nl, r // tr),
        in_specs=[pl.BlockSpec((1, tr, c), lambda l, i: (l, i, 0))],
        out_specs=pl.BlockSpec((1, tr, n_pad), lambda l, i: (l, i, 0)),
        out_shape=jax.ShapeDtypeStruct((nl, r, n_pad), BF16),
        compiler_params=_params(2),
        name="cast_pad_cols",
    )(x)


def _cast_pad_rows_kernel(x_ref, o_ref, *, n_blocks):
    i = pl.program_id(1)

    @pl.when(i < n_blocks)
    def _():
        o_ref[0] = x_ref[0].astype(BF16)

    @pl.when(i >= n_blocks)
    def _():
        o_ref[0] = jnp.zeros(o_ref.shape[1:], BF16)


def cast_pad_rows(x, r_pad, tr=256):
    nl, r, c = x.shape
    n_blocks = r // tr
    return pl.pallas_call(
        functools.partial(_cast_pad_rows_kernel, n_blocks=n_blocks),
        grid=(nl, r_pad // tr),
        in_specs=[pl.BlockSpec((1, tr, c), lambda l, i: (l, jnp.minimum(i, n_blocks - 1), 0))],
        out_specs=pl.BlockSpec((1, tr, c), lambda l, i: (l, i, 0)),
        out_shape=jax.ShapeDtypeStruct((nl, r_pad, c), BF16),
        compiler_params=_params(2),
        name="cast_pad_rows",
    )(x)


def _rope_cols(base):
    half = QK_ROPE // 2
    x1 = base + jnp.arange(half)
    x2 = base + half + jnp.arange(half)
    return jnp.concatenate([x1, x2, x2, x1])


def _mla_tables(pos):
    half = QK_ROPE // 2
    inv = jnp.power(jnp.float32(ROPE_THETA), -jnp.arange(half, dtype=F32) * (2.0 / QK_ROPE))
    ang = pos.astype(F32)[:, None] * inv[None, :]
    c, s = jnp.cos(ang), jnp.sin(ang)
    z = jnp.zeros_like(c)
    return jnp.concatenate([c, c, z, z], axis=1), jnp.concatenate([-s, s, z, z], axis=1)


def _swa_tables(pos):
    half = ROPE_DIM_B // 2
    inv = jnp.power(jnp.float32(ROPE_THETA), -jnp.arange(half, dtype=F32) * (2.0 / ROPE_DIM_B))
    ang = pos.astype(F32)[:, None] * inv[None, :]
    c, s = jnp.cos(ang), jnp.sin(ang)
    n = pos.shape[0]
    rest = SWA_HEAD_DIM - ROPE_DIM_B
    c64 = jnp.concatenate([c, c, jnp.ones((n, rest), F32)], axis=1)
    s1 = jnp.concatenate([-s, jnp.zeros((n, SWA_HEAD_DIM - half), F32)], axis=1)
    s2 = jnp.concatenate([jnp.zeros((n, half), F32), s, jnp.zeros((n, rest), F32)], axis=1)
    rep = LANES // SWA_HEAD_DIM
    return jnp.tile(c64, (1, rep)), jnp.tile(s1, (1, rep)), jnp.tile(s2, (1, rep))


def _prep_weights(p):
    w = {}
    a_cols = jnp.concatenate([jnp.arange(Q_LORA + KV_LORA), _rope_cols(Q_LORA + KV_LORA)])
    w['mla_w_a'] = p['mla_w_a'][:, :, a_cols].astype(BF16)
    qk = QK_NOPE + QK_ROPE
    q_cols = jnp.concatenate([jnp.concatenate([h * qk + jnp.arange(QK_NOPE), _rope_cols(h * qk + QK_NOPE)])
                              for h in range(MLA_HEADS)])
    w['mla_w_uq'] = p['mla_w_uq'][:, :, q_cols].astype(BF16)
    w['mla_w_uk'] = p['mla_w_uk'].reshape(N_A, KV_LORA, MLA_HEADS * QK_NOPE).astype(BF16)
    w['mla_w_uvt'] = p['mla_w_uv'].reshape(N_A, KV_LORA, MLA_HEADS * V_DIM).transpose(0, 2, 1).astype(BF16)
    w['mla_w_o'] = p['mla_w_o'].astype(BF16)
    w['swa_w_kv'] = p['swa_w_kv'].astype(BF16)
    wq = p['swa_w_q'].reshape(N_B, D_MODEL, SWA_KV_HEADS, SWA_GROUP, SWA_HEAD_DIM)
    w['swa_w_q'] = wq.transpose(0, 1, 3, 2, 4).reshape(N_B, D_MODEL, D_MODEL).astype(BF16)
    wo = p['swa_w_o'].reshape(N_B, SWA_KV_HEADS, SWA_GROUP, SWA_HEAD_DIM, D_MODEL)
    w['swa_w_o'] = wo.transpose(0, 2, 1, 3, 4).reshape(N_B, D_MODEL, D_MODEL).astype(BF16)
    w['swa_sinks'] = p['swa_sinks'].reshape(N_B, SWA_KV_HEADS, SWA_GROUP)
    fpad = F_PAD - D_FF
    w['ffn_w_gate'] = cast_pad_cols(p['ffn_w_gate'], F_PAD)
    w['ffn_w_up'] = cast_pad_cols(p['ffn_w_up'], F_PAD)
    w['ffn_w_down'] = cast_pad_rows(p['ffn_w_down'], F_PAD)
    w['ffn_conv_w'] = jnp.pad(p['ffn_conv_w'], ((0, 0), (0, 0), (0, fpad)))
    w['ffn_conv_b'] = jnp.pad(p['ffn_conv_b'], ((0, 0), (0, fpad)))
    return w


def _trunk(x, pos, p, w, cache):
    bsz, seq, _ = x.shape
    m = bsz * seq
    h = x.reshape(m, D_MODEL)
    tm = _tile(m, 512)
    tm_l = _tile(m, 1024)
    past = 0 if cache is None else cache['ckv'].shape[2]

    mla_cos, mla_sin = (jnp.tile(t, (bsz, 1)) for t in _mla_tables(pos))
    swa_c, swa_s1, swa_s2 = (jnp.tile(t, (bsz, 1)) for t in _swa_tables(pos))
    tab = _row_spec(tm, LANES)
    tab_l = _row_spec(tm_l, LANES)

    def out_proj(o, w_o, h_res):
        return matmul(o, w_o, tm=tm_l, tn=1024, epilogue=_epi_residual,
                      extras=(h_res,), extra_specs=(_tile_spec(tm_l, 1024),),
                      out_shapes=jax.ShapeDtypeStruct((m, D_MODEL), F32),
                      out_specs=_tile_spec(tm_l, 1024), name="out_proj_residual")

    ckv_rows, krope_rows, conv_rows = [], [], []
    ks = vs = ks_b = vs_t = None
    for l in range(DEPTH):
        if l < N_A:
            n_a = w['mla_w_a'].shape[2]
            cq, ckv, ckv_b, krope, kr_b = matmul(
                h, w['mla_w_a'][l], tm=tm, tn=n_a, epilogue=_epi_mla_in, norm_x=True,
                extras=(p['mla_g_q'][l].reshape(1, Q_LORA), p['mla_g_kv'][l].reshape(1, KV_LORA),
                        mla_cos, mla_sin, p['norm_attn'][l].reshape(1, D_MODEL)),
                extra_specs=(_const_spec(Q_LORA), _const_spec(KV_LORA), tab, tab,
                             _const_spec(D_MODEL)),
                out_shapes=[jax.ShapeDtypeStruct((m, Q_LORA), BF16),
                            jax.ShapeDtypeStruct((m, KV_LORA), F32),
                            jax.ShapeDtypeStruct((m, KV_LORA), BF16),
                            jax.ShapeDtypeStruct((m, QK_ROPE), F32),
                            jax.ShapeDtypeStruct((m, LANES), BF16)],
                out_specs=[_row_spec(tm, Q_LORA), _row_spec(tm, KV_LORA), _row_spec(tm, KV_LORA),
                           _row_spec(tm, QK_ROPE), _row_spec(tm, LANES)],
                name="mla_in_proj")
            ckv_rows.append(ckv.reshape(bsz, seq, KV_LORA))
            krope_rows.append(krope.reshape(bsz, seq, QK_ROPE))
            tn_q = 8 * MLA_QDIM
            q = matmul(cq, w['mla_w_uq'][l], tm=tm_l, tn=tn_q, epilogue=_epi_mla_q,
                       extras=(mla_cos, mla_sin), extra_specs=(tab_l, tab_l),
                       out_shapes=jax.ShapeDtypeStruct((m, MLA_HEADS * MLA_QDIM), BF16),
                       out_specs=_tile_spec(tm_l, tn_q), name="mla_q_proj")
            if cache is None:
                ckv_all = ckv_b.reshape(bsz, seq, KV_LORA)
                kr_all = kr_b.reshape(bsz, seq, LANES)
                k_valid = seq
                tq = tk = min(seq, 1024)
            else:
                k_valid = past + seq
                sk = -(-k_valid // LANES) * LANES
                ckv_all = jnp.concatenate(
                    [cache['ckv'][l].astype(BF16), ckv_b.reshape(bsz, seq, KV_LORA),
                     jnp.zeros((bsz, sk - k_valid, KV_LORA), BF16)], axis=1)
                kr_cache = jnp.pad(cache['krope'][l], ((0, 0), (0, 0), (0, LANES - QK_ROPE)))
                kr_all = jnp.concatenate(
                    [kr_cache.astype(BF16), kr_b.reshape(bsz, seq, LANES),
                     jnp.zeros((bsz, sk - k_valid, LANES), BF16)], axis=1)
                tq, tk = LANES, sk
            sk = ckv_all.shape[1]
            mk = bsz * sk
            tmk = _tile(mk, 1024)
            kn = matmul(ckv_all.reshape(mk, KV_LORA), w['mla_w_uk'][l], tm=tmk, tn=2048,
                        epilogue=_epi_cast,
                        out_shapes=jax.ShapeDtypeStruct((mk, MLA_HEADS * QK_NOPE), BF16),
                        out_specs=_tile_spec(tmk, 2048), name="mla_k_up_proj")
            if sk % 512 == 0:
                vt = matmul_nt(w['mla_w_uvt'][l], ckv_all, tn=MLA_HEADS * V_DIM, ts=512)
            else:
                vt = matmul_nt(w['mla_w_uvt'][l], ckv_all, tn=1024, ts=sk)
            q3 = q.reshape(bsz, seq, MLA_HEADS * MLA_QDIM)
            if seq < tq:
                q3 = jnp.pad(q3, ((0, 0), (0, tq - seq), (0, 0)))
            o = mla_attention(q3, kn.reshape(bsz, sk, MLA_HEADS * QK_NOPE), kr_all, vt,
                              q_off=past, k_valid=k_valid, tq=tq, tk=tk)
            h = out_proj(o[:, :seq].reshape(m, D_MODEL), w['mla_w_o'][l], h)
        else:
            i = l - N_A
            if l == N_A:
                ks, vs, ks_b, vs_t = matmul(
                    h, w['swa_w_kv'], tm=tm, tn=2 * SWA_KV_DIM, epilogue=_epi_shared_kv,
                    norm_x=True,
                    extras=(swa_c, swa_s1, swa_s2, p['kv_shared_norm'].reshape(1, D_MODEL)),
                    extra_specs=(tab, tab, tab, _const_spec(D_MODEL)),
                    out_shapes=[jax.ShapeDtypeStruct((m, SWA_KV_DIM), F32)] * 2
                    + [jax.ShapeDtypeStruct((m, SWA_KV_DIM), BF16),
                       jax.ShapeDtypeStruct((SWA_KV_DIM, m), BF16)],
                    out_specs=[_row_spec(tm, SWA_KV_DIM)] * 3
                    + [pl.BlockSpec((SWA_KV_DIM, tm), lambda i, j: (0, i))],
                    name="swa_shared_kv_proj")
            hn = rmsnorm(h, p['norm_attn'][l], BF16)
            q = matmul(hn, w['swa_w_q'][i], tm=tm_l, tn=1024, epilogue=_epi_swa_q,
                       extras=(swa_c, swa_s1, swa_s2), extra_specs=(tab_l, tab_l, tab_l),
                       out_shapes=jax.ShapeDtypeStruct((m, D_MODEL), BF16),
                       out_specs=_tile_spec(tm_l, 1024), name="swa_q_proj")
            q3 = q.reshape(bsz, seq, D_MODEL)
            k3 = ks_b.reshape(bsz, seq, SWA_KV_DIM)
            vt3 = vs_t.reshape(SWA_KV_DIM, bsz, seq).transpose(1, 0, 2)
            sink = w['swa_sinks'][i]
            if cache is None:
                o = swa_attention(q3, k3, vt3, sink)
            else:
                win = cache['swa_k'].shape[1]
                npad = 2 * WINDOW - win - seq
                k_all = jnp.concatenate(
                    [cache['swa_k'].reshape(bsz, win, SWA_KV_DIM).astype(BF16), k3,
                     jnp.zeros((bsz, npad, SWA_KV_DIM), BF16)], axis=1)
                vt_cache = cache['swa_v'].reshape(bsz, win, SWA_KV_DIM).transpose(0, 2, 1)
                vt_all = jnp.concatenate(
                    [vt_cache.astype(BF16), vt3, jnp.zeros((bsz, SWA_KV_DIM, npad), BF16)], axis=2)
                q3 = jnp.pad(q3, ((0, 0), (0, WINDOW - seq), (0, 0)))
                o = swa_attention(q3, k_all, vt_all, sink, k_valid=win + seq)[:, :seq]
            h = out_proj(o.reshape(m, D_MODEL), w['swa_w_o'][i], h)

        hn = rmsnorm(h, p['norm_ffn'][l], BF16)
        if cache is None:
            prev = jnp.zeros((bsz, SUBLANES, F_PAD), F32)
        else:
            prev = jnp.pad(cache['conv'][l],
                           ((0, 0), (SUBLANES - (CONV_W - 1), 0), (0, F_PAD - D_FF)))
        act, gl = ffn_a(hn, w['ffn_w_gate'][l], w['ffn_w_up'][l], w['ffn_conv_w'][l],
                        w['ffn_conv_b'][l], prev, seq)
        gl = gl.reshape(bsz, -1, SUBLANES, F_PAD)
        conv_rows.append(gl[:, -1, SUBLANES - (CONV_W - 1):, :D_FF])
        h = matmul(act, w['ffn_w_down'][l], tm=tm, tn=512, epilogue=_epi_residual,
                   extras=(h,), extra_specs=(_tile_spec(tm, 512),),
                   out_shapes=jax.ShapeDtypeStruct((m, D_MODEL), F32),
                   out_specs=_tile_spec(tm, 512), name="ffn_down_residual")

    y = rmsnorm(h, p['norm_final'], F32).reshape(bsz, seq, D_MODEL)
    ks4 = ks.reshape(bsz, seq, SWA_KV_HEADS, SWA_HEAD_DIM)
    vs4 = vs.reshape(bsz, seq, SWA_KV_HEADS, SWA_HEAD_DIM)
    if cache is None:
        win = min(WINDOW, seq)
        ks4, vs4 = ks4[:, seq - win:], vs4[:, seq - win:]
    return y, jnp.stack(ckv_rows), jnp.stack(krope_rows), ks4, vs4, jnp.stack(conv_rows)


def kernel(x_prompt, x_sample, cache_mla_ckv, cache_mla_krope, cache_swa_k, cache_swa_v, state_conv,
           norm_attn, norm_ffn, mla_w_a, mla_g_q, mla_g_kv, mla_w_uq, mla_w_uk, mla_w_uv, mla_w_o,
           kv_shared_norm, swa_w_kv, swa_w_q, swa_sinks, swa_w_o,
           ffn_w_gate, ffn_w_up, ffn_conv_w, ffn_conv_b, ffn_w_down, norm_final):
    p = {
        'norm_attn': norm_attn, 'norm_ffn': norm_ffn,
        'mla_w_a': mla_w_a, 'mla_g_q': mla_g_q, 'mla_g_kv': mla_g_kv, 'mla_w_uq': mla_w_uq,
        'mla_w_uk': mla_w_uk, 'mla_w_uv': mla_w_uv, 'mla_w_o': mla_w_o,
        'kv_shared_norm': kv_shared_norm, 'swa_w_kv': swa_w_kv, 'swa_w_q': swa_w_q,
        'swa_sinks': swa_sinks, 'swa_w_o': swa_w_o,
        'ffn_w_gate': ffn_w_gate, 'ffn_w_up': ffn_w_up, 'ffn_conv_w': ffn_conv_w,
        'ffn_conv_b': ffn_conv_b, 'ffn_w_down': ffn_w_down, 'norm_final': norm_final,
    }
    w = _prep_weights(p)
    pos_p = jnp.arange(x_prompt.shape[1])
    out_p = _trunk(x_prompt, pos_p, p, w, None)
    past = cache_mla_ckv.shape[2]
    pos_s = past + jnp.arange(x_sample.shape[1])
    cache = {'ckv': cache_mla_ckv, 'krope': cache_mla_krope, 'swa_k': cache_swa_k,
             'swa_v': cache_swa_v, 'conv': state_conv}
    out_s = _trunk(x_sample, pos_s, p, w, cache)
    return (out_p[0], out_s[0]) + out_p[1:] + out_s[1:]
```

```python
import functools

import jax
import jax.numpy as jnp
from jax import lax
from jax.experimental import pallas as pl
from jax.experimental.pallas import tpu as pltpu

D_MODEL = 4096
DEPTH = 4
CHUNK = 64
N_A = DEPTH // 2
N_B = DEPTH - N_A
ROPE_THETA = 500000.0
EPS = 1e-6
NEG_INF = -1e30
MLA_HEADS = D_MODEL // 128
Q_LORA = D_MODEL // 4
KV_LORA = 512
QK_NOPE = 128
QK_ROPE = 64
V_DIM = 128
MLA_SCALE = (QK_NOPE + QK_ROPE) ** -0.5
LOG2E = 1.4426950408889634
MLA_Q_SCALE = MLA_SCALE * LOG2E
SWA_HEAD_DIM = 64
SWA_HEADS = D_MODEL // SWA_HEAD_DIM
SWA_KV_HEADS = 8
SWA_GROUP = SWA_HEADS // SWA_KV_HEADS
WINDOW = 128
ROPE_DIM_B = SWA_HEAD_DIM // 4
SWA_SCALE = SWA_HEAD_DIM ** -0.5
D_FF = 256 * ((8 * D_MODEL // 3 + 255) // 256)
CONV_W = 3
FFN_TF = 512
F_PAD = -(-D_FF // FFN_TF) * FFN_TF

LANES = 128
SUBLANES = 8
V7X_VMEM_LIMIT = 56 * 1024 * 1024

MLA_QDIM = 2 * LANES
DENOM_ROWS = 16
SWA_KV_DIM = SWA_KV_HEADS * SWA_HEAD_DIM
HEAD_SLAB = 2 * LANES
BF16 = jnp.bfloat16
F32 = jnp.float32


def _params(n_grid):
    return pltpu.CompilerParams(dimension_semantics=("arbitrary",) * n_grid,
                                vmem_limit_bytes=V7X_VMEM_LIMIT)


def _tile(m, cap):
    for step in (LANES, 16):
        t = (min(cap, m) // step) * step
        while t >= step:
            if m % t == 0:
                return t
            t -= step
    return m


def _rms(x, g):
    return x * lax.rsqrt(jnp.mean(x * x, axis=-1, keepdims=True) + EPS) * g


def _rmsnorm_kernel(x_ref, g_ref, o_ref):
    o_ref[...] = _rms(x_ref[...], g_ref[...]).astype(o_ref.dtype)


def rmsnorm(x, g, out_dtype):
    m, d = x.shape
    tm = _tile(m, 256)
    return pl.pallas_call(
        _rmsnorm_kernel,
        grid=(m // tm,),
        in_specs=[pl.BlockSpec((tm, d), lambda i: (i, 0)),
                  pl.BlockSpec((1, d), lambda i: (0, 0))],
        out_specs=pl.BlockSpec((tm, d), lambda i: (i, 0)),
        out_shape=jax.ShapeDtypeStruct((m, d), out_dtype),
        compiler_params=_params(1),
        name="rmsnorm",
    )(x, g.reshape(1, d))


def _mm_kernel(x_ref, w_ref, *rest, epilogue, n_extra, norm_x):
    extras = rest[:n_extra]
    if norm_x:
        x = _rms(x_ref[...], extras[-1][...]).astype(BF16)
    else:
        x = x_ref[...]
    y = jnp.dot(x, w_ref[...], preferred_element_type=F32)
    epilogue(y, extras, rest[n_extra:])


def matmul(x, w, *, tm, tn, epilogue, extras=(), extra_specs=(), out_shapes, out_specs, name,
           norm_x=False, layer=None):
    m, k = x.shape
    n = w.shape[-1]
    assert not norm_x or tn == n
    w_mode = dict(pipeline_mode=pl.Buffered(1)) if tn == n else {}
    if layer is None:
        w_spec = pl.BlockSpec((k, tn), lambda i, j: (0, j), **w_mode)
    else:
        w_spec = pl.BlockSpec((None, k, tn), lambda i, j: (layer, 0, j), **w_mode)
    return pl.pallas_call(
        functools.partial(_mm_kernel, epilogue=epilogue, n_extra=len(extras), norm_x=norm_x),
        grid=(m // tm, n // tn),
        in_specs=[pl.BlockSpec((tm, k), lambda i, j: (i, 0)), w_spec] + list(extra_specs),
        out_specs=out_specs,
        out_shape=out_shapes,
        compiler_params=_params(2),
        name=name,
    )(x, w, *extras)


def _epi_cast(y, extras, outs):
    outs[0][...] = y.astype(outs[0].dtype)


def _epi_residual(y, extras, outs):
    outs[0][...] = extras[0][...] + y


def _rope_slab(a, cos2, sin2):
    return a * cos2 + pltpu.roll(a, QK_ROPE, axis=1) * sin2


def _epi_mla_in(y, extras, outs):
    gq_ref, gkv_ref, cos_ref, sin_ref = extras[:4]
    cq_ref, ckv_ref, ckvb_ref, kr_ref, krb_ref = outs
    cq_ref[...] = _rms(y[:, :Q_LORA], gq_ref[...]).astype(BF16)
    ckv = _rms(y[:, Q_LORA:Q_LORA + KV_LORA], gkv_ref[...])
    ckv_ref[...] = ckv
    ckvb_ref[...] = ckv.astype(BF16)
    kr = _rope_slab(y[:, Q_LORA + KV_LORA:], cos_ref[...], sin_ref[...])
    kr_ref[...] = kr[:, :QK_ROPE]
    krb_ref[...] = kr.astype(BF16)


def _epi_mla_q(y, extras, outs):
    cos_ref, sin_ref = extras
    cos2 = cos_ref[...] * MLA_Q_SCALE
    sin2 = sin_ref[...] * MLA_Q_SCALE
    for s in range(y.shape[1] // MLA_QDIM):
        lo = s * MLA_QDIM
        outs[0][:, lo:lo + LANES] = (y[:, lo:lo + LANES] * MLA_Q_SCALE).astype(BF16)
        outs[0][:, lo + LANES:lo + MLA_QDIM] = _rope_slab(
            y[:, lo + LANES:lo + MLA_QDIM], cos2, sin2).astype(BF16)


def _rope_b(y, c, s1, s2, scale):
    half = ROPE_DIM_B // 2
    for s in range(y.shape[1] // LANES):
        a = y[:, s * LANES:(s + 1) * LANES]
        r = a * c + pltpu.roll(a, LANES - half, axis=1) * s1 + pltpu.roll(a, half, axis=1) * s2
        yield s, (r * scale if scale != 1.0 else r)


def _epi_swa_q(y, extras, outs):
    c_ref, s1_ref, s2_ref = extras
    for s, r in _rope_b(y, c_ref[...], s1_ref[...], s2_ref[...], SWA_SCALE):
        outs[0][:, s * LANES:(s + 1) * LANES] = r.astype(BF16)


def _epi_shared_kv(y, extras, outs):
    c_ref, s1_ref, s2_ref = extras[:3]
    k_ref, v_ref, kb_ref, vtb_ref = outs
    for s, r in _rope_b(y[:, :SWA_KV_DIM], c_ref[...], s1_ref[...], s2_ref[...], 1.0):
        k_ref[:, s * LANES:(s + 1) * LANES] = r
        kb_ref[:, s * LANES:(s + 1) * LANES] = r.astype(BF16)
    v = y[:, SWA_KV_DIM:]
    v_ref[...] = v
    vtb_ref[...] = v.T.astype(BF16)


def _row_spec(tm, width):
    return pl.BlockSpec((tm, width), lambda i, j: (i, 0))


def _tile_spec(tm, tn):
    return pl.BlockSpec((tm, tn), lambda i, j: (i, j))


def _const_spec(width):
    return pl.BlockSpec((1, width), lambda i, j: (0, 0))


def _ffn_a_kernel(x_ref, wg_ref, wu_ref, cw_ref, cb_ref, prev_ref, act_ref, gl_ref, gbuf, carry,
                  *, seg, nseg, tiles_per_seq, sub):
    i = pl.program_id(0)
    j = pl.program_id(1)
    if tiles_per_seq > 1:
        @pl.when(jnp.logical_and(i == 0, j == 0))
        def _():
            carry[...] = jnp.zeros(carry.shape, F32)

        gbuf[0:SUBLANES] = jnp.where(i % tiles_per_seq == 0, prev_ref[0], carry[j])
    else:
        for s in range(nseg):
            gbuf[s * (SUBLANES + seg):s * (SUBLANES + seg) + SUBLANES] = prev_ref[s]
    x = x_ref[...]
    for c in range(act_ref.shape[1] // sub):
        cols = slice(c * sub, (c + 1) * sub)
        g = jnp.dot(x, wg_ref[:, cols], preferred_element_type=F32)
        u = jnp.dot(x, wu_ref[:, cols], preferred_element_type=F32)
        w0 = cw_ref[0:1, cols]
        w1 = cw_ref[1:2, cols]
        w2 = cw_ref[2:3, cols]
        b = cb_ref[:, cols]
        for s in range(nseg):
            base = s * (SUBLANES + seg)
            gs = g[s * seg:(s + 1) * seg]
            gl_ref[s, :, cols] = gs[seg - SUBLANES:]
            gbuf[base + SUBLANES:base + SUBLANES + seg, cols] = gs
            g1 = gbuf[base + SUBLANES - 1:base + SUBLANES - 1 + seg, cols]
            g2 = gbuf[base + SUBLANES - 2:base + SUBLANES - 2 + seg, cols]
            gc = b + w0 * g2 + w1 * g1 + w2 * gs
            a = gc * jax.nn.sigmoid(gc) * u[s * seg:(s + 1) * seg]
            act_ref[s * seg:(s + 1) * seg, cols] = a.astype(BF16)
        if tiles_per_seq > 1:
            carry[j, :, cols] = g[g.shape[0] - SUBLANES:]


def ffn_a(hn, wg, wu, layer, cw, cb, prev, seq_len, *, tm_cap=1024, tf=FFN_TF, sub=2 * LANES):
    m, d = hn.shape
    f = wg.shape[2]
    tm = _tile(m, tm_cap)
    seg = min(seq_len, tm)
    nseg = tm // seg
    tiles_per_seq = seq_len // seg
    nj = f // tf
    if nseg == 1:
        prev_map = lambda i, j: (i // tiles_per_seq, 0, j)
    else:
        prev_map = lambda i, j: (i, 0, j)
    return pl.pallas_call(
        functools.partial(_ffn_a_kernel, seg=seg, nseg=nseg, tiles_per_seq=tiles_per_seq, sub=sub),
        grid=(m // tm, nj),
        in_specs=[pl.BlockSpec((tm, d), lambda i, j: (i, 0)),
                  pl.BlockSpec((None, d, tf), lambda i, j: (layer, 0, j)),
                  pl.BlockSpec((None, d, tf), lambda i, j: (layer, 0, j)),
                  pl.BlockSpec((CONV_W, tf), lambda i, j: (0, j)),
                  pl.BlockSpec((1, tf), lambda i, j: (0, j)),
                  pl.BlockSpec((nseg, SUBLANES, tf), prev_map)],
        out_specs=[pl.BlockSpec((tm, tf), lambda i, j: (i, j)),
                   pl.BlockSpec((nseg, SUBLANES, tf), lambda i, j: (i, 0, j))],
        out_shape=[jax.ShapeDtypeStruct((m, f), BF16),
                   jax.ShapeDtypeStruct((m // seg, SUBLANES, f), F32)],
        scratch_shapes=[pltpu.VMEM((nseg * (SUBLANES + seg), tf), F32),
                        pltpu.VMEM((nj, SUBLANES, tf), F32)],
        compiler_params=_params(2),
        name="ffn_gate_up_conv",
    )(hn, wg, wu, cw, cb.reshape(1, f), prev)


def _mla_attn_kernel(q_ref, kn_ref, kr_ref, vt_ref, o_ref, m_ref, acc_ref, s_ref, p_ref,
                     *, hb, tq, tk, nk, q_off, k_valid, cg, rb):
    qi = pl.program_id(2)
    ki = pl.program_id(3)
    q_lo = q_off + qi * tq
    k_lo = ki * tk

    @pl.when(ki == 0)
    def _():
        m_ref[...] = jnp.full(m_ref.shape, NEG_INF, F32)
        acc_ref[...] = jnp.zeros(acc_ref.shape, F32)

    needed = k_lo // CHUNK <= (q_lo + tq - 1) // CHUNK
    full = jnp.logical_and((k_lo + tk - 1) // CHUNK <= q_lo // CHUNK, k_lo + tk <= k_valid)

    def step(masked):
        kr = kr_ref[0]
        ones = jnp.ones((DENOM_ROWS, tk), BF16)
        for h in range(hb):
            buf = h % s_ref.shape[0]
            q = q_ref[0, :, h * MLA_QDIM:(h + 1) * MLA_QDIM]
            k = jnp.concatenate([kn_ref[0, :, h * QK_NOPE:(h + 1) * QK_NOPE], kr], axis=1)
            s_ref[buf] = lax.dot_general(k, q, (((1,), (1,)), ((), ())), preferred_element_type=F32)
            alphas = []
            for c in range(tq // cg):
                cols = slice(c * cg, (c + 1) * cg)
                if masked:
                    qc = (q_lo + c * cg + lax.broadcasted_iota(jnp.int32, (1, cg), 1)) // CHUNK

                def scores(r):
                    s = s_ref[buf, r * rb:(r + 1) * rb, cols]
                    if masked:
                        kp = k_lo + r * rb + lax.broadcasted_iota(jnp.int32, (rb, 1), 0)
                        s = jnp.where(jnp.logical_and(qc >= kp // CHUNK, kp < k_valid), s, NEG_INF)
                    return s

                part = None
                for r in range(tk // rb):
                    blk = jnp.max(scores(r).reshape(rb // SUBLANES, SUBLANES, cg), axis=0)
                    part = blk if part is None else jnp.maximum(part, blk)
                m_prev = m_ref[h, :, cols]
                m_new = jnp.maximum(m_prev, jnp.max(part, axis=0, keepdims=True))
                alphas.append(jnp.exp2(m_prev - m_new))
                m_ref[h, :, cols] = m_new
                for r in range(tk // rb):
                    p_ref[buf, r * rb:(r + 1) * rb, cols] = jnp.exp2(scores(r) - m_new).astype(BF16)
            vt = jnp.concatenate([vt_ref[0, h * V_DIM:(h + 1) * V_DIM, :], ones], axis=0)
            pv = jnp.dot(vt, p_ref[buf], preferred_element_type=F32)
            acc_ref[h] = jnp.concatenate(alphas, axis=1) * acc_ref[h] + pv

    @pl.when(jnp.logical_and(needed, full))
    def _():
        step(False)

    @pl.when(jnp.logical_and(needed, jnp.logical_not(full)))
    def _():
        step(True)

    @pl.when(ki == nk - 1)
    def _():
        for h in range(hb):
            o_t = acc_ref[h, :V_DIM, :] / acc_ref[h, V_DIM:V_DIM + 1, :]
            o_ref[0, :, h * V_DIM:(h + 1) * V_DIM] = o_t.T.astype(BF16)


def mla_attention(q, kn, kr, vt, *, q_off, k_valid, tq, tk, hb=8):
    bsz, sq, _ = q.shape
    sk = kn.shape[1]
    nq, nk = sq // tq, sk // tk
    n_hg = MLA_HEADS // hb

    def k_idx(qi, ki):
        last = ((q_off + (qi + 1) * tq - 1) // CHUNK * CHUNK + CHUNK - 1) // tk
        return jnp.minimum(ki, jnp.minimum(last, nk - 1))

    return pl.pallas_call(
        functools.partial(_mla_attn_kernel, hb=hb, tq=tq, tk=tk, nk=nk, q_off=q_off,
                          k_valid=k_valid, cg=min(tq, 2 * LANES), rb=LANES),
        grid=(bsz, n_hg, nq, nk),
        in_specs=[pl.BlockSpec((1, tq, hb * MLA_QDIM), lambda b, g, qi, ki: (b, qi, g)),
                  pl.BlockSpec((1, tk, hb * QK_NOPE), lambda b, g, qi, ki: (b, k_idx(qi, ki), g)),
                  pl.BlockSpec((1, tk, LANES), lambda b, g, qi, ki: (b, k_idx(qi, ki), 0)),
                  pl.BlockSpec((1, hb * V_DIM, tk), lambda b, g, qi, ki: (b, g, k_idx(qi, ki)))],
        out_specs=pl.BlockSpec((1, tq, hb * V_DIM), lambda b, g, qi, ki: (b, qi, g)),
        out_shape=jax.ShapeDtypeStruct((bsz, sq, MLA_HEADS * V_DIM), BF16),
        scratch_shapes=[pltpu.VMEM((hb, 1, tq), F32),
                        pltpu.VMEM((hb, V_DIM + DENOM_ROWS, tq), F32),
                        pltpu.VMEM((2, tk, tq), F32),
                        pltpu.VMEM((2, tk, tq), BF16)],
        compiler_params=_params(4),
        name="mla_flash_attention",
    )(q, kn, kr, vt)


def _nt_kernel(w_ref, x_ref, o_ref):
    o_ref[0] = lax.dot_general(w_ref[...], x_ref[0], (((1,), (1,)), ((), ())),
                               preferred_element_type=F32).astype(o_ref.dtype)


def matmul_nt(wt, layer, x, *, tn, ts):
    _, n, k = wt.shape
    bsz, s, _ = x.shape
    return pl.pallas_call(
        _nt_kernel,
        grid=(bsz, s // ts, n // tn),
        in_specs=[pl.BlockSpec((None, tn, k), lambda b, i, j: (layer, j, 0)),
                  pl.BlockSpec((1, ts, k), lambda b, i, j: (b, i, 0))],
        out_specs=pl.BlockSpec((1, tn, ts), lambda b, i, j: (b, j, i)),
        out_shape=jax.ShapeDtypeStruct((bsz, n, s), BF16),
        compiler_params=_params(3),
        name="mla_v_up_proj_t",
    )(wt, x)


def _swa_attn_kernel(sink_ref, q_ref, ka_ref, kb_ref, vta_ref, vtb_ref, o_ref, ot_ref, *, tq, k_valid):
    t = pl.program_id(1)
    k = jnp.concatenate([ka_ref[0], kb_ref[0]], axis=0)
    vt = jnp.concatenate([vta_ref[0], vtb_ref[0]], axis=1)
    tk = k.shape[0]
    kp = lax.broadcasted_iota(jnp.int32, (tk, 1), 0)
    if k_valid is None:
        kp = kp + (t - 1) * tq
        kc = kp // CHUNK
        qc = (t * tq + lax.broadcasted_iota(jnp.int32, (1, tq), 1)) // CHUNK
        valid = jnp.logical_and(kp >= 0, jnp.logical_and(kc >= qc - WINDOW // CHUNK, kc <= qc))
    else:
        valid = jnp.broadcast_to(kp < k_valid, (tk, tq))
    valid = jnp.concatenate([valid] * SWA_GROUP, axis=1)
    lane_head = lax.broadcasted_iota(jnp.int32, (1, HEAD_SLAB), 1) // SWA_HEAD_DIM
    heads_per_slab = HEAD_SLAB // SWA_HEAD_DIM
    for slab in range(SWA_KV_DIM // HEAD_SLAB):
        k_slab = k[:, slab * HEAD_SLAB:(slab + 1) * HEAD_SLAB]
        qs = jnp.concatenate(
            [q_ref[0, :, g * SWA_KV_DIM + slab * HEAD_SLAB:g * SWA_KV_DIM + (slab + 1) * HEAD_SLAB]
             for g in range(SWA_GROUP)], axis=0)
        for hh in range(heads_per_slab):
            kvh = slab * heads_per_slab + hh
            km = k_slab * (lane_head == hh).astype(BF16)
            vth = vt[kvh * SWA_HEAD_DIM:(kvh + 1) * SWA_HEAD_DIM, :]
            sb = jnp.concatenate([jnp.full((1, tq), sink_ref[kvh, g], F32) for g in range(SWA_GROUP)],
                                 axis=1)
            s = lax.dot_general(km, qs, (((1,), (1,)), ((), ())), preferred_element_type=F32)
            s = jnp.where(valid, s, NEG_INF)
            m = jnp.maximum(jnp.max(s, axis=0, keepdims=True), sb)
            e = jnp.exp(s - m)
            inv = 1.0 / (jnp.sum(e, axis=0, keepdims=True) + jnp.exp(sb - m))
            o_t = jnp.dot(vth, (e * inv).astype(BF16), preferred_element_type=F32)
            for g in range(SWA_GROUP):
                row = g * SWA_KV_DIM + kvh * SWA_HEAD_DIM
                ot_ref[row:row + SWA_HEAD_DIM, :] = o_t[:, g * tq:(g + 1) * tq]
    for c in range(D_MODEL // LANES):
        o_ref[0, :, c * LANES:(c + 1) * LANES] = ot_ref[c * LANES:(c + 1) * LANES, :].T.astype(BF16)


def swa_attention(q, k, vt, sink, *, k_valid=None):
    bsz, sq, _ = q.shape
    tq = WINDOW
    if k_valid is None:
        a_idx = lambda t: jnp.maximum(t - 1, 0)
        b_idx = lambda t: t
    else:
        assert sq == tq and k.shape[1] == 2 * WINDOW
        a_idx = lambda t: 0
        b_idx = lambda t: 1
    return pl.pallas_call(
        functools.partial(_swa_attn_kernel, tq=tq, k_valid=k_valid),
        grid=(bsz, sq // tq),
        in_specs=[pl.BlockSpec(memory_space=pltpu.SMEM),
                  pl.BlockSpec((1, tq, D_MODEL), lambda b, t: (b, t, 0)),
                  pl.BlockSpec((1, WINDOW, SWA_KV_DIM), lambda b, t: (b, a_idx(t), 0)),
                  pl.BlockSpec((1, WINDOW, SWA_KV_DIM), lambda b, t: (b, b_idx(t), 0)),
                  pl.BlockSpec((1, SWA_KV_DIM, WINDOW), lambda b, t: (b, 0, a_idx(t))),
                  pl.BlockSpec((1, SWA_KV_DIM, WINDOW), lambda b, t: (b, 0, b_idx(t)))],
        out_specs=pl.BlockSpec((1, tq, D_MODEL), lambda b, t: (b, t, 0)),
        out_shape=jax.ShapeDtypeStruct((bsz, sq, D_MODEL), BF16),
        scratch_shapes=[pltpu.VMEM((D_MODEL, tq), F32)],
        compiler_params=_params(2),
        name="swa_sink_attention",
    )(sink, q, k, k, vt, vt)


def _cast_pad_cols_kernel(x_ref, o_ref):
    n = x_ref.shape[-1]
    o_ref[0, :, :n] = x_ref[0].astype(BF16)
    o_ref[0, :, n:] = jnp.zeros((o_ref.shape[1], o_ref.shape[2] - n), BF16)


def cast_pad_cols(x, n_pad, tr=256):
    nl, r, c = x.shape
    return pl.pallas_call(
        _cast_pad_cols_kernel,
        grid=(nl, r // tr),
        in_specs=[pl.BlockSpec((1, tr, c), lambda l, i: (l, i, 0))],
        out_specs=pl.BlockSpec((1, tr, n_pad), lambda l, i: (l, i, 0)),
        out_shape=jax.ShapeDtypeStruct((nl, r, n_pad), BF16),
        compiler_params=_params(2),
        name="cast_pad_cols",
    )(x)


def _cast_pad_rows_kernel(x_ref, o_ref, *, n_blocks):
    i = pl.program_id(1)

    @pl.when(i < n_blocks)
    def _():
        o_ref[0] = x_ref[0].astype(BF16)

    @pl.when(i >= n_blocks)
    def _():
        o_ref[0] = jnp.zeros(o_ref.shape[1:], BF16)


def cast_pad_rows(x, r_pad, tr=256):
    nl, r, c = x.shape
    n_blocks = r // tr
    return pl.pallas_call(
        functools.partial(_cast_pad_rows_kernel, n_blocks=n_blocks),
        grid=(nl, r_pad // tr),
        in_specs=[pl.BlockSpec((1, tr, c), lambda l, i: (l, jnp.minimum(i, n_blocks - 1), 0))],
        out_specs=pl.BlockSpec((1, tr, c), lambda l, i: (l, i, 0)),
        out_shape=jax.ShapeDtypeStruct((nl, r_pad, c), BF16),
        compiler_params=_params(2),
        name="cast_pad_rows",
    )(x)


def _rope_cols(base):
    half = QK_ROPE // 2
    x1 = base + jnp.arange(half)
    x2 = base + half + jnp.arange(half)
    return jnp.concatenate([x1, x2, x2, x1])


def _mla_tables(pos):
    half = QK_ROPE // 2
    inv = jnp.power(jnp.float32(ROPE_THETA), -jnp.arange(half, dtype=F32) * (2.0 / QK_ROPE))
    ang = pos.astype(F32)[:, None] * inv[None, :]
    c, s = jnp.cos(ang), jnp.sin(ang)
    z = jnp.zeros_like(c)
    return jnp.concatenate([c, c, z, z], axis=1), jnp.concatenate([-s, s, z, z], axis=1)


def _swa_tables(pos):
    half = ROPE_DIM_B // 2
    inv = jnp.power(jnp.float32(ROPE_THETA), -jnp.arange(half, dtype=F32) * (2.0 / ROPE_DIM_B))
    ang = pos.astype(F32)[:, None] * inv[None, :]
    c, s = jnp.cos(ang), jnp.sin(ang)
    n = pos.shape[0]
    rest = SWA_HEAD_DIM - ROPE_DIM_B
    c64 = jnp.concatenate([c, c, jnp.ones((n, rest), F32)], axis=1)
    s1 = jnp.concatenate([-s, jnp.zeros((n, SWA_HEAD_DIM - half), F32)], axis=1)
    s2 = jnp.concatenate([jnp.zeros((n, half), F32), s, jnp.zeros((n, rest), F32)], axis=1)
    rep = LANES // SWA_HEAD_DIM
    return jnp.tile(c64, (1, rep)), jnp.tile(s1, (1, rep)), jnp.tile(s2, (1, rep))


def _prep_weights(p):
    w = {}
    a_cols = jnp.concatenate([jnp.arange(Q_LORA + KV_LORA), _rope_cols(Q_LORA + KV_LORA)])
    w['mla_w_a'] = p['mla_w_a'][:, :, a_cols].astype(BF16)
    qk = QK_NOPE + QK_ROPE
    q_cols = jnp.concatenate([jnp.concatenate([h * qk + jnp.arange(QK_NOPE), _rope_cols(h * qk + QK_NOPE)])
                              for h in range(MLA_HEADS)])
    w['mla_w_uq'] = p['mla_w_uq'][:, :, q_cols].astype(BF16)
    w['mla_w_uk'] = p['mla_w_uk'].reshape(N_A, KV_LORA, MLA_HEADS * QK_NOPE).astype(BF16)
    w['mla_w_uvt'] = p['mla_w_uv'].reshape(N_A, KV_LORA, MLA_HEADS * V_DIM).transpose(0, 2, 1).astype(BF16)
    w['mla_w_o'] = p['mla_w_o'].astype(BF16)
    w['swa_w_kv'] = p['swa_w_kv'].astype(BF16)
    wq = p['swa_w_q'].reshape(N_B, D_MODEL, SWA_KV_HEADS, SWA_GROUP, SWA_HEAD_DIM)
    w['swa_w_q'] = wq.transpose(0, 1, 3, 2, 4).reshape(N_B, D_MODEL, D_MODEL).astype(BF16)
    wo = p['swa_w_o'].reshape(N_B, SWA_KV_HEADS, SWA_GROUP, SWA_HEAD_DIM, D_MODEL)
    w['swa_w_o'] = wo.transpose(0, 2, 1, 3, 4).reshape(N_B, D_MODEL, D_MODEL).astype(BF16)
    w['swa_sinks'] = p['swa_sinks'].reshape(N_B, SWA_KV_HEADS, SWA_GROUP)
    fpad = F_PAD - D_FF
    w['ffn_w_gate'] = cast_pad_cols(p['ffn_w_gate'], F_PAD)
    w['ffn_w_up'] = cast_pad_cols(p['ffn_w_up'], F_PAD)
    w['ffn_w_down'] = cast_pad_rows(p['ffn_w_down'], F_PAD)
    w['ffn_conv_w'] = jnp.pad(p['ffn_conv_w'], ((0, 0), (0, 0), (0, fpad)))
    w['ffn_conv_b'] = jnp.pad(p['ffn_conv_b'], ((0, 0), (0, fpad)))
    return w


def _trunk(x, pos, p, w, cache):
    bsz, seq, _ = x.shape
    m = bsz * seq
    h = x.reshape(m, D_MODEL)
    tm = _tile(m, 512)
    tm_l = _tile(m, 1024)
    past = 0 if cache is None else cache['ckv'].shape[2]

    mla_cos, mla_sin = (jnp.tile(t, (bsz, 1)) for t in _mla_tables(pos))
    swa_c, swa_s1, swa_s2 = (jnp.tile(t, (bsz, 1)) for t in _swa_tables(pos))
    tab = _row_spec(tm, LANES)
    tab_l = _row_spec(tm_l, LANES)

    def out_proj(o, w_o, layer, h_res):
        return matmul(o, w_o, layer=layer, tm=tm_l, tn=1024, epilogue=_epi_residual,
                      extras=(h_res,), extra_specs=(_tile_spec(tm_l, 1024),),
                      out_shapes=jax.ShapeDtypeStruct((m, D_MODEL), F32),
                      out_specs=_tile_spec(tm_l, 1024), name="out_proj_residual")

    ckv_rows, krope_rows, conv_rows = [], [], []
    ks = vs = ks_b = vs_t = None
    for l in range(DEPTH):
        if l < N_A:
            n_a = w['mla_w_a'].shape[2]
            cq, ckv, ckv_b, krope, kr_b = matmul(
                h, w['mla_w_a'], layer=l, tm=tm, tn=n_a, epilogue=_epi_mla_in, norm_x=True,
                extras=(p['mla_g_q'][l].reshape(1, Q_LORA), p['mla_g_kv'][l].reshape(1, KV_LORA),
                        mla_cos, mla_sin, p['norm_attn'][l].reshape(1, D_MODEL)),
                extra_specs=(_const_spec(Q_LORA), _const_spec(KV_LORA), tab, tab,
                             _const_spec(D_MODEL)),
                out_shapes=[jax.ShapeDtypeStruct((m, Q_LORA), BF16),
                            jax.ShapeDtypeStruct((m, KV_LORA), F32),
                            jax.ShapeDtypeStruct((m, KV_LORA), BF16),
                            jax.ShapeDtypeStruct((m, QK_ROPE), F32),
                            jax.ShapeDtypeStruct((m, LANES), BF16)],
                out_specs=[_row_spec(tm, Q_LORA), _row_spec(tm, KV_LORA), _row_spec(tm, KV_LORA),
                           _row_spec(tm, QK_ROPE), _row_spec(tm, LANES)],
                name="mla_in_proj")
            ckv_rows.append(ckv.reshape(bsz, seq, KV_LORA))
            krope_rows.append(krope.reshape(bsz, seq, QK_ROPE))
            tn_q = 8 * MLA_QDIM
            q = matmul(cq, w['mla_w_uq'], layer=l, tm=tm_l, tn=tn_q, epilogue=_epi_mla_q,
                       extras=(mla_cos, mla_sin), extra_specs=(tab_l, tab_l),
                       out_shapes=jax.ShapeDtypeStruct((m, MLA_HEADS * MLA_QDIM), BF16),
                       out_specs=_tile_spec(tm_l, tn_q), name="mla_q_proj")
            if cache is None:
                ckv_all = ckv_b.reshape(bsz, seq, KV_LORA)
                kr_all = kr_b.reshape(bsz, seq, LANES)
                k_valid = seq
                tq = tk = min(seq, 1024)
            else:
                k_valid = past + seq
                sk = -(-k_valid // LANES) * LANES
                ckv_all = jnp.concatenate(
                    [cache['ckv'][l].astype(BF16), ckv_b.reshape(bsz, seq, KV_LORA),
                     jnp.zeros((bsz, sk - k_valid, KV_LORA), BF16)], axis=1)
                kr_cache = jnp.pad(cache['krope'][l], ((0, 0), (0, 0), (0, LANES - QK_ROPE)))
                kr_all = jnp.concatenate(
                    [kr_cache.astype(BF16), kr_b.reshape(bsz, seq, LANES),
                     jnp.zeros((bsz, sk - k_valid, LANES), BF16)], axis=1)
                tq, tk = LANES, sk
            sk = ckv_all.shape[1]
            mk = bsz * sk
            tmk = _tile(mk, 1024)
            kn = matmul(ckv_all.reshape(mk, KV_LORA), w['mla_w_uk'], layer=l, tm=tmk, tn=2048,
                        epilogue=_epi_cast,
                        out_shapes=jax.ShapeDtypeStruct((mk, MLA_HEADS * QK_NOPE), BF16),
                        out_specs=_tile_spec(tmk, 2048), name="mla_k_up_proj")
            if sk % 512 == 0:
                vt = matmul_nt(w['mla_w_uvt'], l, ckv_all, tn=MLA_HEADS * V_DIM, ts=512)
            else:
                vt = matmul_nt(w['mla_w_uvt'], l, ckv_all, tn=1024, ts=sk)
            q3 = q.reshape(bsz, seq, MLA_HEADS * MLA_QDIM)
            if seq < tq:
                q3 = jnp.pad(q3, ((0, 0), (0, tq - seq), (0, 0)))
            o = mla_attention(q3, kn.reshape(bsz, sk, MLA_HEADS * QK_NOPE), kr_all, vt,
                              q_off=past, k_valid=k_valid, tq=tq, tk=tk)
            h = out_proj(o[:, :seq].reshape(m, D_MODEL), w['mla_w_o'], l, h)
        else:
            i = l - N_A
            if l == N_A:
                ks, vs, ks_b, vs_t = matmul(
                    h, w['swa_w_kv'], tm=tm, tn=2 * SWA_KV_DIM, epilogue=_epi_shared_kv,
                    norm_x=True,
                    extras=(swa_c, swa_s1, swa_s2, p['kv_shared_norm'].reshape(1, D_MODEL)),
                    extra_specs=(tab, tab, tab, _const_spec(D_MODEL)),
                    out_shapes=[jax.ShapeDtypeStruct((m, SWA_KV_DIM), F32)] * 2
                    + [jax.ShapeDtypeStruct((m, SWA_KV_DIM), BF16),
                       jax.ShapeDtypeStruct((SWA_KV_DIM, m), BF16)],
                    out_specs=[_row_spec(tm, SWA_KV_DIM)] * 3
                    + [pl.BlockSpec((SWA_KV_DIM, tm), lambda i, j: (0, i))],
                    name="swa_shared_kv_proj")
            hn = rmsnorm(h, p['norm_attn'][l], BF16)
            q = matmul(hn, w['swa_w_q'], layer=i, tm=tm_l, tn=1024, epilogue=_epi_swa_q,
                       extras=(swa_c, swa_s1, swa_s2), extra_specs=(tab_l, tab_l, tab_l),
                       out_shapes=jax.ShapeDtypeStruct((m, D_MODEL), BF16),
                       out_specs=_tile_spec(tm_l, 1024), name="swa_q_proj")
            q3 = q.reshape(bsz, seq, D_MODEL)
            k3 = ks_b.reshape(bsz, seq, SWA_KV_DIM)
            vt3 = vs_t.reshape(SWA_KV_DIM, bsz, seq).transpose(1, 0, 2)
            sink = w['swa_sinks'][i]
            if cache is None:
                o = swa_attention(q3, k3, vt3, sink)
            else:
                win = cache['swa_k'].shape[1]
                npad = 2 * WINDOW - win - seq
                k_all = jnp.concatenate(
                    [cache['swa_k'].reshape(bsz, win, SWA_KV_DIM).astype(BF16), k3,
                     jnp.zeros((bsz, npad, SWA_KV_DIM), BF16)], axis=1)
                vt_cache = cache['swa_v'].reshape(bsz, win, SWA_KV_DIM).transpose(0, 2, 1)
                vt_all = jnp.concatenate(
                    [vt_cache.astype(BF16), vt3, jnp.zeros((bsz, SWA_KV_DIM, npad), BF16)], axis=2)
                q3 = jnp.pad(q3, ((0, 0), (0, WINDOW - seq), (0, 0)))
                o = swa_attention(q3, k_all, vt_all, sink, k_valid=win + seq)[:, :seq]
            h = out_proj(o.reshape(m, D_MODEL), w['swa_w_o'], i, h)

        hn = rmsnorm(h, p['norm_ffn'][l], BF16)
        if cache is None:
            prev = jnp.zeros((bsz, SUBLANES, F_PAD), F32)
        else:
            prev = jnp.pad(cache['conv'][l],
                           ((0, 0), (SUBLANES - (CONV_W - 1), 0), (0, F_PAD - D_FF)))
        act, gl = ffn_a(hn, w['ffn_w_gate'], w['ffn_w_up'], l, w['ffn_conv_w'][l],
                        w['ffn_conv_b'][l], prev, seq)
        gl = gl.reshape(bsz, -1, SUBLANES, F_PAD)
        conv_rows.append(gl[:, -1, SUBLANES - (CONV_W - 1):, :D_FF])
        h = matmul(act, w['ffn_w_down'], layer=l, tm=tm, tn=512, epilogue=_epi_residual,
                   extras=(h,), extra_specs=(_tile_spec(tm, 512),),
                   out_shapes=jax.ShapeDtypeStruct((m, D_MODEL), F32),
                   out_specs=_tile_spec(tm, 512), name="ffn_down_residual")

    y = rmsnorm(h, p['norm_final'], F32).reshape(bsz, seq, D_MODEL)
    win = min(WINDOW, seq) if cache is None else seq
    ks4 = ks.reshape(bsz, seq, SWA_KV_DIM)[:, seq - win:].reshape(bsz, win, SWA_KV_HEADS, SWA_HEAD_DIM)
    vs4 = vs.reshape(bsz, seq, SWA_KV_DIM)[:, seq - win:].reshape(bsz, win, SWA_KV_HEADS, SWA_HEAD_DIM)
    return y, jnp.stack(ckv_rows), jnp.stack(krope_rows), ks4, vs4, jnp.stack(conv_rows)


def kernel(x_prompt, x_sample, cache_mla_ckv, cache_mla_krope, cache_swa_k, cache_swa_v, state_conv,
           norm_attn, norm_ffn, mla_w_a, mla_g_q, mla_g_kv, mla_w_uq, mla_w_uk, mla_w_uv, mla_w_o,
           kv_shared_norm, swa_w_kv, swa_w_q, swa_sinks, swa_w_o,
           ffn_w_gate, ffn_w_up, ffn_conv_w, ffn_conv_b, ffn_w_down, norm_final):
    p = {
        'norm_attn': norm_attn, 'norm_ffn': norm_ffn,
        'mla_w_a': mla_w_a, 'mla_g_q': mla_g_q, 'mla_g_kv': mla_g_kv, 'mla_w_uq': mla_w_uq,
        'mla_w_uk': mla_w_uk, 'mla_w_uv': mla_w_uv, 'mla_w_o': mla_w_o,
        'kv_shared_norm': kv_shared_norm, 'swa_w_kv': swa_w_kv, 'swa_w_q': swa_w_q,
        'swa_sinks': swa_sinks, 'swa_w_o': swa_w_o,
        'ffn_w_gate': ffn_w_gate, 'ffn_w_up': ffn_w_up, 'ffn_conv_w': ffn_conv_w,
        'ffn_conv_b': ffn_conv_b, 'ffn_w_down': ffn_w_down, 'norm_final': norm_final,
    }
    w = _prep_weights(p)
    pos_p = jnp.arange(x_prompt.shape[1])
    out_p = _trunk(x_prompt, pos_p, p, w, None)
    past = cache_mla_ckv.shape[2]
    pos_s = past + jnp.arange(x_sample.shape[1])
    cache = {'ckv': cache_mla_ckv, 'krope': cache_mla_krope, 'swa_k': cache_swa_k,
             'swa_v': cache_swa_v, 'conv': state_conv}
    out_s = _trunk(x_sample, pos_s, p, w, cache)
    return (out_p[0], out_s[0]) + out_p[1:] + out_s[1:]
```

```python
import functools

import jax
import jax.numpy as jnp
from jax import lax
from jax.experimental import pallas as pl
from jax.experimental.pallas import tpu as pltpu

D_MODEL = 4096
DEPTH = 4
CHUNK = 64
N_A = DEPTH // 2
N_B = DEPTH - N_A
ROPE_THETA = 500000.0
EPS = 1e-6
NEG_INF = -1e30
MLA_HEADS = D_MODEL // 128
Q_LORA = D_MODEL // 4
KV_LORA = 512
QK_NOPE = 128
QK_ROPE = 64
V_DIM = 128
MLA_SCALE = (QK_NOPE + QK_ROPE) ** -0.5
LOG2E = 1.4426950408889634
MLA_Q_SCALE = MLA_SCALE * LOG2E
SWA_HEAD_DIM = 64
SWA_HEADS = D_MODEL // SWA_HEAD_DIM
SWA_KV_HEADS = 8
SWA_GROUP = SWA_HEADS // SWA_KV_HEADS
WINDOW = 128
ROPE_DIM_B = SWA_HEAD_DIM // 4
SWA_SCALE = SWA_HEAD_DIM ** -0.5
D_FF = 256 * ((8 * D_MODEL // 3 + 255) // 256)
CONV_W = 3
FFN_TF = 512
F_PAD = -(-D_FF // FFN_TF) * FFN_TF

LANES = 128
SUBLANES = 8
V7X_VMEM_LIMIT = 56 * 1024 * 1024

MLA_QDIM = 2 * LANES
DENOM_ROWS = 16
SWA_KV_DIM = SWA_KV_HEADS * SWA_HEAD_DIM
HEAD_SLAB = 2 * LANES
BF16 = jnp.bfloat16
F32 = jnp.float32


def _params(n_grid):
    return pltpu.CompilerParams(dimension_semantics=("arbitrary",) * n_grid,
                                vmem_limit_bytes=V7X_VMEM_LIMIT)


def _tile(m, cap):
    for step in (LANES, 16):
        t = (min(cap, m) // step) * step
        while t >= step:
            if m % t == 0:
                return t
            t -= step
    return m


def _rms(x, g):
    return x * lax.rsqrt(jnp.mean(x * x, axis=-1, keepdims=True) + EPS) * g


def _rmsnorm_kernel(x_ref, g_ref, o_ref):
    o_ref[...] = _rms(x_ref[...], g_ref[...]).astype(o_ref.dtype)


def rmsnorm(x, g, out_dtype):
    m, d = x.shape
    tm = _tile(m, 256)
    return pl.pallas_call(
        _rmsnorm_kernel,
        grid=(m // tm,),
        in_specs=[pl.BlockSpec((tm, d), lambda i: (i, 0)),
                  pl.BlockSpec((1, d), lambda i: (0, 0))],
        out_specs=pl.BlockSpec((tm, d), lambda i: (i, 0)),
        out_shape=jax.ShapeDtypeStruct((m, d), out_dtype),
        compiler_params=_params(1),
        name="rmsnorm",
    )(x, g.reshape(1, d))


def _mm_kernel(x_ref, w_ref, *rest, epilogue, n_extra, norm_x):
    extras = rest[:n_extra]
    if norm_x:
        x = _rms(x_ref[...], extras[-1][...]).astype(BF16)
    else:
        x = x_ref[...]
    y = jnp.dot(x, w_ref[...], preferred_element_type=F32)
    epilogue(y, extras, rest[n_extra:])


def matmul(x, w, *, tm, tn, epilogue, extras=(), extra_specs=(), out_shapes, out_specs, name,
           norm_x=False, layer=None):
    m, k = x.shape
    n = w.shape[-1]
    assert not norm_x or tn == n
    w_mode = dict(pipeline_mode=pl.Buffered(1)) if tn == n else {}
    if layer is None:
        w_spec = pl.BlockSpec((k, tn), lambda i, j: (0, j), **w_mode)
    else:
        w_spec = pl.BlockSpec((None, k, tn), lambda i, j: (layer, 0, j), **w_mode)
    return pl.pallas_call(
        functools.partial(_mm_kernel, epilogue=epilogue, n_extra=len(extras), norm_x=norm_x),
        grid=(m // tm, n // tn),
        in_specs=[pl.BlockSpec((tm, k), lambda i, j: (i, 0)), w_spec] + list(extra_specs),
        out_specs=out_specs,
        out_shape=out_shapes,
        compiler_params=_params(2),
        name=name,
    )(x, w, *extras)


def _epi_cast(y, extras, outs):
    outs[0][...] = y.astype(outs[0].dtype)


def _epi_residual(y, extras, outs):
    outs[0][...] = extras[0][...] + y


def _rope_slab(a, cos2, sin2):
    return a * cos2 + pltpu.roll(a, QK_ROPE, axis=1) * sin2


def _epi_mla_in(y, extras, outs):
    gq_ref, gkv_ref, cos_ref, sin_ref = extras[:4]
    cq_ref, ckv_ref, ckvb_ref, kr_ref, krb_ref = outs
    cq_ref[...] = _rms(y[:, :Q_LORA], gq_ref[...]).astype(BF16)
    ckv = _rms(y[:, Q_LORA:Q_LORA + KV_LORA], gkv_ref[...])
    ckv_ref[...] = ckv
    ckvb_ref[...] = ckv.astype(BF16)
    kr = _rope_slab(y[:, Q_LORA + KV_LORA:], cos_ref[...], sin_ref[...])
    kr_ref[...] = kr[:, :QK_ROPE]
    krb_ref[...] = kr.astype(BF16)


def _epi_mla_q(y, extras, outs):
    cos_ref, sin_ref = extras
    cos2 = cos_ref[...] * MLA_Q_SCALE
    sin2 = sin_ref[...] * MLA_Q_SCALE
    for s in range(y.shape[1] // MLA_QDIM):
        lo = s * MLA_QDIM
        outs[0][:, lo:lo + LANES] = (y[:, lo:lo + LANES] * MLA_Q_SCALE).astype(BF16)
        outs[0][:, lo + LANES:lo + MLA_QDIM] = _rope_slab(
            y[:, lo + LANES:lo + MLA_QDIM], cos2, sin2).astype(BF16)


def _rope_b(y, c, s1, s2, scale):
    half = ROPE_DIM_B // 2
    for s in range(y.shape[1] // LANES):
        a = y[:, s * LANES:(s + 1) * LANES]
        r = a * c + pltpu.roll(a, LANES - half, axis=1) * s1 + pltpu.roll(a, half, axis=1) * s2
        yield s, (r * scale if scale != 1.0 else r)


def _epi_swa_q(y, extras, outs):
    c_ref, s1_ref, s2_ref = extras
    for s, r in _rope_b(y, c_ref[...], s1_ref[...], s2_ref[...], SWA_SCALE):
        outs[0][:, s * LANES:(s + 1) * LANES] = r.astype(BF16)


def _epi_shared_kv(y, extras, outs):
    c_ref, s1_ref, s2_ref = extras[:3]
    k_ref, v_ref, kb_ref, vtb_ref = outs
    for s, r in _rope_b(y[:, :SWA_KV_DIM], c_ref[...], s1_ref[...], s2_ref[...], 1.0):
        k_ref[:, s * LANES:(s + 1) * LANES] = r
        kb_ref[:, s * LANES:(s + 1) * LANES] = r.astype(BF16)
    v = y[:, SWA_KV_DIM:]
    v_ref[...] = v
    vtb_ref[...] = v.T.astype(BF16)


def _row_spec(tm, width):
    return pl.BlockSpec((tm, width), lambda i, j: (i, 0))


def _tile_spec(tm, tn):
    return pl.BlockSpec((tm, tn), lambda i, j: (i, j))


def _const_spec(width):
    return pl.BlockSpec((1, width), lambda i, j: (0, 0))


def _ffn_a_kernel(x_ref, wg_ref, wu_ref, cw_ref, cb_ref, prev_ref, act_ref, gl_ref, gbuf, carry,
                  *, seg, nseg, tiles_per_seq, sub):
    i = pl.program_id(0)
    j = pl.program_id(1)
    if tiles_per_seq > 1:
        @pl.when(jnp.logical_and(i == 0, j == 0))
        def _():
            carry[...] = jnp.zeros(carry.shape, F32)

        gbuf[0:SUBLANES] = jnp.where(i % tiles_per_seq == 0, prev_ref[0], carry[j])
    else:
        for s in range(nseg):
            gbuf[s * SUBLANES:(s + 1) * SUBLANES] = prev_ref[s]
    x = x_ref[...]
    for c in range(act_ref.shape[1] // sub):
        cols = slice(c * sub, (c + 1) * sub)
        g = jnp.dot(x, wg_ref[:, cols], preferred_element_type=F32)
        u = jnp.dot(x, wu_ref[:, cols], preferred_element_type=F32)
        w0 = cw_ref[0:1, cols]
        w1 = cw_ref[1:2, cols]
        w2 = cw_ref[2:3, cols]
        b = cb_ref[:, cols]
        for s in range(nseg):
            base = s * SUBLANES
            gs = g[s * seg:(s + 1) * seg]
            gl_ref[s, :, cols] = gs[seg - SUBLANES:]
            pv = gbuf[base:base + SUBLANES, cols]
            r1 = pltpu.roll(gs, 1, axis=0)
            r2 = pltpu.roll(gs, 2, axis=0)
            row = lax.broadcasted_iota(jnp.int32, (SUBLANES, sub), 0)
            h1 = jnp.where(row == 0, pv[SUBLANES - 1:SUBLANES], r1[:SUBLANES])
            h2 = jnp.where(row == 0, pv[SUBLANES - 2:SUBLANES - 1],
                           jnp.where(row == 1, pv[SUBLANES - 1:SUBLANES], r2[:SUBLANES]))
            g1 = jnp.concatenate([h1, r1[SUBLANES:]], axis=0)
            g2 = jnp.concatenate([h2, r2[SUBLANES:]], axis=0)
            gc = b + w0 * g2 + w1 * g1 + w2 * gs
            a = gc * jax.nn.sigmoid(gc) * u[s * seg:(s + 1) * seg]
            act_ref[s * seg:(s + 1) * seg, cols] = a.astype(BF16)
        if tiles_per_seq > 1:
            carry[j, :, cols] = g[g.shape[0] - SUBLANES:]


def ffn_a(hn, wg, wu, layer, cw, cb, prev, seq_len, *, tm_cap=1024, tf=FFN_TF, sub=2 * LANES):
    m, d = hn.shape
    f = wg.shape[2]
    tm = _tile(m, tm_cap)
    seg = min(seq_len, tm)
    nseg = tm // seg
    tiles_per_seq = seq_len // seg
    nj = f // tf
    if nseg == 1:
        prev_map = lambda i, j: (i // tiles_per_seq, 0, j)
    else:
        prev_map = lambda i, j: (i, 0, j)
    return pl.pallas_call(
        functools.partial(_ffn_a_kernel, seg=seg, nseg=nseg, tiles_per_seq=tiles_per_seq, sub=sub),
        grid=(m // tm, nj),
        in_specs=[pl.BlockSpec((tm, d), lambda i, j: (i, 0)),
                  pl.BlockSpec((None, d, tf), lambda i, j: (layer, 0, j)),
                  pl.BlockSpec((None, d, tf), lambda i, j: (layer, 0, j)),
                  pl.BlockSpec((CONV_W, tf), lambda i, j: (0, j)),
                  pl.BlockSpec((1, tf), lambda i, j: (0, j)),
                  pl.BlockSpec((nseg, SUBLANES, tf), prev_map)],
        out_specs=[pl.BlockSpec((tm, tf), lambda i, j: (i, j)),
                   pl.BlockSpec((nseg, SUBLANES, tf), lambda i, j: (i, 0, j))],
        out_shape=[jax.ShapeDtypeStruct((m, f), BF16),
                   jax.ShapeDtypeStruct((m // seg, SUBLANES, f), F32)],
        scratch_shapes=[pltpu.VMEM((nseg * SUBLANES, tf), F32),
                        pltpu.VMEM((nj, SUBLANES, tf), F32)],
        compiler_params=_params(2),
        name="ffn_gate_up_conv",
    )(hn, wg, wu, cw, cb.reshape(1, f), prev)


def _mla_attn_kernel(q_ref, kn_ref, kr_ref, vt_ref, o_ref, m_ref, acc_ref, s_ref, p_ref,
                     *, hb, tq, tk, nk, q_off, k_valid, cg, rb):
    qi = pl.program_id(2)
    ki = pl.program_id(3)
    q_lo = q_off + qi * tq
    k_lo = ki * tk

    @pl.when(ki == 0)
    def _():
        m_ref[...] = jnp.full(m_ref.shape, NEG_INF, F32)
        acc_ref[...] = jnp.zeros(acc_ref.shape, F32)

    needed = k_lo // CHUNK <= (q_lo + tq - 1) // CHUNK
    full = jnp.logical_and((k_lo + tk - 1) // CHUNK <= q_lo // CHUNK, k_lo + tk <= k_valid)

    def step(masked):
        kr = kr_ref[0]
        ones = jnp.ones((DENOM_ROWS, tk), BF16)
        for h in range(hb):
            buf = h % s_ref.shape[0]
            q = q_ref[0, :, h * MLA_QDIM:(h + 1) * MLA_QDIM]
            k = jnp.concatenate([kn_ref[0, :, h * QK_NOPE:(h + 1) * QK_NOPE], kr], axis=1)
            s_ref[buf] = lax.dot_general(k, q, (((1,), (1,)), ((), ())), preferred_element_type=F32)
            alphas = []
            for c in range(tq // cg):
                cols = slice(c * cg, (c + 1) * cg)
                if masked:
                    qc = (q_lo + c * cg + lax.broadcasted_iota(jnp.int32, (1, cg), 1)) // CHUNK

                def scores(r):
                    s = s_ref[buf, r * rb:(r + 1) * rb, cols]
                    if masked:
                        kp = k_lo + r * rb + lax.broadcasted_iota(jnp.int32, (rb, 1), 0)
                        s = jnp.where(jnp.logical_and(qc >= kp // CHUNK, kp < k_valid), s, NEG_INF)
                    return s

                part = None
                for r in range(tk // rb):
                    blk = jnp.max(scores(r).reshape(rb // SUBLANES, SUBLANES, cg), axis=0)
                    part = blk if part is None else jnp.maximum(part, blk)
                m_prev = m_ref[h, :, cols]
                m_new = jnp.maximum(m_prev, jnp.max(part, axis=0, keepdims=True))
                alphas.append(jnp.exp2(m_prev - m_new))
                m_ref[h, :, cols] = m_new
                for r in range(tk // rb):
                    p_ref[buf, r * rb:(r + 1) * rb, cols] = jnp.exp2(scores(r) - m_new).astype(BF16)
            vt = jnp.concatenate([vt_ref[0, h * V_DIM:(h + 1) * V_DIM, :], ones], axis=0)
            pv = jnp.dot(vt, p_ref[buf], preferred_element_type=F32)
            acc_ref[h] = jnp.concatenate(alphas, axis=1) * acc_ref[h] + pv

    @pl.when(jnp.logical_and(needed, full))
    def _():
        step(False)

    @pl.when(jnp.logical_and(needed, jnp.logical_not(full)))
    def _():
        step(True)

    @pl.when(ki == nk - 1)
    def _():
        for h in range(hb):
            o_t = acc_ref[h, :V_DIM, :] / acc_ref[h, V_DIM:V_DIM + 1, :]
            o_ref[0, :, h * V_DIM:(h + 1) * V_DIM] = o_t.T.astype(BF16)


def mla_attention(q, kn, kr, vt, *, q_off, k_valid, tq, tk, hb=8, n_buf=2):
    bsz, sq, _ = q.shape
    sk = kn.shape[1]
    nq, nk = sq // tq, sk // tk
    n_hg = MLA_HEADS // hb

    def k_idx(qi, ki):
        last = ((q_off + (qi + 1) * tq - 1) // CHUNK * CHUNK + CHUNK - 1) // tk
        return jnp.minimum(ki, jnp.minimum(last, nk - 1))

    return pl.pallas_call(
        functools.partial(_mla_attn_kernel, hb=hb, tq=tq, tk=tk, nk=nk, q_off=q_off,
                          k_valid=k_valid, cg=min(tq, 2 * LANES), rb=LANES),
        grid=(bsz, n_hg, nq, nk),
        in_specs=[pl.BlockSpec((1, tq, hb * MLA_QDIM), lambda b, g, qi, ki: (b, qi, g)),
                  pl.BlockSpec((1, tk, hb * QK_NOPE), lambda b, g, qi, ki: (b, k_idx(qi, ki), g)),
                  pl.BlockSpec((1, tk, LANES), lambda b, g, qi, ki: (b, k_idx(qi, ki), 0)),
                  pl.BlockSpec((1, hb * V_DIM, tk), lambda b, g, qi, ki: (b, g, k_idx(qi, ki)))],
        out_specs=pl.BlockSpec((1, tq, hb * V_DIM), lambda b, g, qi, ki: (b, qi, g)),
        out_shape=jax.ShapeDtypeStruct((bsz, sq, MLA_HEADS * V_DIM), BF16),
        scratch_shapes=[pltpu.VMEM((hb, 1, tq), F32),
                        pltpu.VMEM((hb, V_DIM + DENOM_ROWS, tq), F32),
                        pltpu.VMEM((n_buf, tk, tq), F32),
                        pltpu.VMEM((n_buf, tk, tq), BF16)],
        compiler_params=_params(4),
        name="mla_flash_attention",
    )(q, kn, kr, vt)


def _nt_kernel(w_ref, x_ref, o_ref):
    o_ref[0] = lax.dot_general(w_ref[...], x_ref[0], (((1,), (1,)), ((), ())),
                               preferred_element_type=F32).astype(o_ref.dtype)


def matmul_nt(wt, layer, x, *, tn, ts):
    _, n, k = wt.shape
    bsz, s, _ = x.shape
    return pl.pallas_call(
        _nt_kernel,
        grid=(bsz, s // ts, n // tn),
        in_specs=[pl.BlockSpec((None, tn, k), lambda b, i, j: (layer, j, 0)),
                  pl.BlockSpec((1, ts, k), lambda b, i, j: (b, i, 0))],
        out_specs=pl.BlockSpec((1, tn, ts), lambda b, i, j: (b, j, i)),
        out_shape=jax.ShapeDtypeStruct((bsz, n, s), BF16),
        compiler_params=_params(3),
        name="mla_v_up_proj_t",
    )(wt, x)


def _swa_attn_kernel(sink_ref, q_ref, ka_ref, kb_ref, vta_ref, vtb_ref, o_ref, ot_ref, *, tq, k_valid):
    t = pl.program_id(1)
    k = jnp.concatenate([ka_ref[0], kb_ref[0]], axis=0)
    vt = jnp.concatenate([vta_ref[0], vtb_ref[0]], axis=1)
    tk = k.shape[0]
    kp = lax.broadcasted_iota(jnp.int32, (tk, 1), 0)
    if k_valid is None:
        kp = kp + (t - 1) * tq
        kc = kp // CHUNK
        qc = (t * tq + lax.broadcasted_iota(jnp.int32, (1, tq), 1)) // CHUNK
        valid = jnp.logical_and(kp >= 0, jnp.logical_and(kc >= qc - WINDOW // CHUNK, kc <= qc))
    else:
        valid = jnp.broadcast_to(kp < k_valid, (tk, tq))
    valid = jnp.concatenate([valid] * SWA_GROUP, axis=1)
    lane_head = lax.broadcasted_iota(jnp.int32, (1, HEAD_SLAB), 1) // SWA_HEAD_DIM
    heads_per_slab = HEAD_SLAB // SWA_HEAD_DIM
    for slab in range(SWA_KV_DIM // HEAD_SLAB):
        k_slab = k[:, slab * HEAD_SLAB:(slab + 1) * HEAD_SLAB]
        qs = jnp.concatenate(
            [q_ref[0, :, g * SWA_KV_DIM + slab * HEAD_SLAB:g * SWA_KV_DIM + (slab + 1) * HEAD_SLAB]
             for g in range(SWA_GROUP)], axis=0)
        for hh in range(heads_per_slab):
            kvh = slab * heads_per_slab + hh
            km = k_slab * (lane_head == hh).astype(BF16)
            vth = vt[kvh * SWA_HEAD_DIM:(kvh + 1) * SWA_HEAD_DIM, :]
            sb = jnp.concatenate([jnp.full((1, tq), sink_ref[kvh, g], F32) for g in range(SWA_GROUP)],
                                 axis=1)
            s = lax.dot_general(km, qs, (((1,), (1,)), ((), ())), preferred_element_type=F32)
            s = jnp.where(valid, s, NEG_INF)
            m = jnp.maximum(jnp.max(s, axis=0, keepdims=True), sb)
            e = jnp.exp(s - m)
            inv = 1.0 / (jnp.sum(e, axis=0, keepdims=True) + jnp.exp(sb - m))
            o_t = jnp.dot(vth, (e * inv).astype(BF16), preferred_element_type=F32)
            for g in range(SWA_GROUP):
                row = g * SWA_KV_DIM + kvh * SWA_HEAD_DIM
                ot_ref[row:row + SWA_HEAD_DIM, :] = o_t[:, g * tq:(g + 1) * tq]
    for c in range(D_MODEL // LANES):
        o_ref[0, :, c * LANES:(c + 1) * LANES] = ot_ref[c * LANES:(c + 1) * LANES, :].T.astype(BF16)


def swa_attention(q, k, vt, sink, *, k_valid=None):
    bsz, sq, _ = q.shape
    tq = WINDOW
    if k_valid is None:
        a_idx = lambda t: jnp.maximum(t - 1, 0)
        b_idx = lambda t: t
    else:
        assert sq == tq and k.shape[1] == 2 * WINDOW
        a_idx = lambda t: 0
        b_idx = lambda t: 1
    return pl.pallas_call(
        functools.partial(_swa_attn_kernel, tq=tq, k_valid=k_valid),
        grid=(bsz, sq // tq),
        in_specs=[pl.BlockSpec(memory_space=pltpu.SMEM),
                  pl.BlockSpec((1, tq, D_MODEL), lambda b, t: (b, t, 0)),
                  pl.BlockSpec((1, WINDOW, SWA_KV_DIM), lambda b, t: (b, a_idx(t), 0)),
                  pl.BlockSpec((1, WINDOW, SWA_KV_DIM), lambda b, t: (b, b_idx(t), 0)),
                  pl.BlockSpec((1, SWA_KV_DIM, WINDOW), lambda b, t: (b, 0, a_idx(t))),
                  pl.BlockSpec((1, SWA_KV_DIM, WINDOW), lambda b, t: (b, 0, b_idx(t)))],
        out_specs=pl.BlockSpec((1, tq, D_MODEL), lambda b, t: (b, t, 0)),
        out_shape=jax.ShapeDtypeStruct((bsz, sq, D_MODEL), BF16),
        scratch_shapes=[pltpu.VMEM((D_MODEL, tq), F32)],
        compiler_params=_params(2),
        name="swa_sink_attention",
    )(sink, q, k, k, vt, vt)


def _cast_pad_cols_kernel(x_ref, o_ref):
    n = x_ref.shape[-1]
    o_ref[0, :, :n] = x_ref[0].astype(BF16)
    o_ref[0, :, n:] = jnp.zeros((o_ref.shape[1], o_ref.shape[2] - n), BF16)


def cast_pad_cols(x, n_pad, tr=256):
    nl, r, c = x.shape
    return pl.pallas_call(
        _cast_pad_cols_kernel,
        grid=(nl, r // tr),
        in_specs=[pl.BlockSpec((1, tr, c), lambda l, i: (l, i, 0))],
        out_specs=pl.BlockSpec((1, tr, n_pad), lambda l, i: (l, i, 0)),
        out_shape=jax.ShapeDtypeStruct((nl, r, n_pad), BF16),
        compiler_params=_params(2),
        name="cast_pad_cols",
    )(x)


def _cast_pad_rows_kernel(x_ref, o_ref, *, n_blocks):
    i = pl.program_id(1)

    @pl.when(i < n_blocks)
    def _():
        o_ref[0] = x_ref[0].astype(BF16)

    @pl.when(i >= n_blocks)
    def _():
        o_ref[0] = jnp.zeros(o_ref.shape[1:], BF16)


def cast_pad_rows(x, r_pad, tr=256):
    nl, r, c = x.shape
    n_blocks = r // tr
    return pl.pallas_call(
        functools.partial(_cast_pad_rows_kernel, n_blocks=n_blocks),
        grid=(nl, r_pad // tr),
        in_specs=[pl.BlockSpec((1, tr, c), lambda l, i: (l, jnp.minimum(i, n_blocks - 1), 0))],
        out_specs=pl.BlockSpec((1, tr, c), lambda l, i: (l, i, 0)),
        out_shape=jax.ShapeDtypeStruct((nl, r_pad, c), BF16),
        compiler_params=_params(2),
        name="cast_pad_rows",
    )(x)


def _rope_slab_cols(w_rope):
    half = QK_ROPE // 2
    x1, x2 = w_rope[..., :half], w_rope[..., half:]
    return jnp.concatenate([x1, x2, x2, x1], axis=-1)


def _mla_tables(pos):
    half = QK_ROPE // 2
    inv = jnp.power(jnp.float32(ROPE_THETA), -jnp.arange(half, dtype=F32) * (2.0 / QK_ROPE))
    ang = pos.astype(F32)[:, None] * inv[None, :]
    c, s = jnp.cos(ang), jnp.sin(ang)
    z = jnp.zeros_like(c)
    return jnp.concatenate([c, c, z, z], axis=1), jnp.concatenate([-s, s, z, z], axis=1)


def _swa_tables(pos):
    half = ROPE_DIM_B // 2
    inv = jnp.power(jnp.float32(ROPE_THETA), -jnp.arange(half, dtype=F32) * (2.0 / ROPE_DIM_B))
    ang = pos.astype(F32)[:, None] * inv[None, :]
    c, s = jnp.cos(ang), jnp.sin(ang)
    n = pos.shape[0]
    rest = SWA_HEAD_DIM - ROPE_DIM_B
    c64 = jnp.concatenate([c, c, jnp.ones((n, rest), F32)], axis=1)
    s1 = jnp.concatenate([-s, jnp.zeros((n, SWA_HEAD_DIM - half), F32)], axis=1)
    s2 = jnp.concatenate([jnp.zeros((n, half), F32), s, jnp.zeros((n, rest), F32)], axis=1)
    rep = LANES // SWA_HEAD_DIM
    return jnp.tile(c64, (1, rep)), jnp.tile(s1, (1, rep)), jnp.tile(s2, (1, rep))


def _prep_weights(p):
    w = {}
    wa = p['mla_w_a'].astype(BF16)
    w['mla_w_a'] = jnp.concatenate(
        [wa[..., :Q_LORA + KV_LORA], _rope_slab_cols(wa[..., Q_LORA + KV_LORA:])], axis=-1)
    wuq = p['mla_w_uq'].astype(BF16).reshape(N_A, Q_LORA, MLA_HEADS, QK_NOPE + QK_ROPE)
    w['mla_w_uq'] = jnp.concatenate(
        [wuq[..., :QK_NOPE], _rope_slab_cols(wuq[..., QK_NOPE:])], axis=-1
    ).reshape(N_A, Q_LORA, MLA_HEADS * MLA_QDIM)
    w['mla_w_uk'] = p['mla_w_uk'].reshape(N_A, KV_LORA, MLA_HEADS * QK_NOPE).astype(BF16)
    w['mla_w_uvt'] = p['mla_w_uv'].reshape(N_A, KV_LORA, MLA_HEADS * V_DIM).transpose(0, 2, 1).astype(BF16)
    w['mla_w_o'] = p['mla_w_o'].astype(BF16)
    w['swa_w_kv'] = p['swa_w_kv'].astype(BF16)
    wq = p['swa_w_q'].reshape(N_B, D_MODEL, SWA_KV_HEADS, SWA_GROUP, SWA_HEAD_DIM)
    w['swa_w_q'] = wq.transpose(0, 1, 3, 2, 4).reshape(N_B, D_MODEL, D_MODEL).astype(BF16)
    wo = p['swa_w_o'].reshape(N_B, SWA_KV_HEADS, SWA_GROUP, SWA_HEAD_DIM, D_MODEL)
    w['swa_w_o'] = wo.transpose(0, 2, 1, 3, 4).reshape(N_B, D_MODEL, D_MODEL).astype(BF16)
    w['swa_sinks'] = p['swa_sinks'].reshape(N_B, SWA_KV_HEADS, SWA_GROUP)
    fpad = F_PAD - D_FF
    w['ffn_w_gate'] = cast_pad_cols(p['ffn_w_gate'], F_PAD)
    w['ffn_w_up'] = cast_pad_cols(p['ffn_w_up'], F_PAD)
    w['ffn_w_down'] = cast_pad_rows(p['ffn_w_down'], F_PAD)
    w['ffn_conv_w'] = jnp.pad(p['ffn_conv_w'], ((0, 0), (0, 0), (0, fpad)))
    w['ffn_conv_b'] = jnp.pad(p['ffn_conv_b'], ((0, 0), (0, fpad)))
    return w


def _trunk(x, pos, p, w, cache):
    bsz, seq, _ = x.shape
    m = bsz * seq
    h = x.reshape(m, D_MODEL)
    tm = _tile(m, 512)
    tm_l = _tile(m, 1024)
    past = 0 if cache is None else cache['ckv'].shape[2]

    mla_cos, mla_sin = (jnp.tile(t, (bsz, 1)) for t in _mla_tables(pos))
    swa_c, swa_s1, swa_s2 = (jnp.tile(t, (bsz, 1)) for t in _swa_tables(pos))
    tab = _row_spec(tm, LANES)
    tab_l = _row_spec(tm_l, LANES)

    def out_proj(o, w_o, layer, h_res):
        return matmul(o, w_o, layer=layer, tm=tm_l, tn=1024, epilogue=_epi_residual,
                      extras=(h_res,), extra_specs=(_tile_spec(tm_l, 1024),),
                      out_shapes=jax.ShapeDtypeStruct((m, D_MODEL), F32),
                      out_specs=_tile_spec(tm_l, 1024), name="out_proj_residual")

    ckv_rows, krope_rows, conv_rows = [], [], []
    ks = vs = ks_b = vs_t = None
    for l in range(DEPTH):
        if l < N_A:
            n_a = w['mla_w_a'].shape[2]
            cq, ckv, ckv_b, krope, kr_b = matmul(
                h, w['mla_w_a'], layer=l, tm=tm, tn=n_a, epilogue=_epi_mla_in, norm_x=True,
                extras=(p['mla_g_q'][l].reshape(1, Q_LORA), p['mla_g_kv'][l].reshape(1, KV_LORA),
                        mla_cos, mla_sin, p['norm_attn'][l].reshape(1, D_MODEL)),
                extra_specs=(_const_spec(Q_LORA), _const_spec(KV_LORA), tab, tab,
                             _const_spec(D_MODEL)),
                out_shapes=[jax.ShapeDtypeStruct((m, Q_LORA), BF16),
                            jax.ShapeDtypeStruct((m, KV_LORA), F32),
                            jax.ShapeDtypeStruct((m, KV_LORA), BF16),
                            jax.ShapeDtypeStruct((m, QK_ROPE), F32),
                            jax.ShapeDtypeStruct((m, LANES), BF16)],
                out_specs=[_row_spec(tm, Q_LORA), _row_spec(tm, KV_LORA), _row_spec(tm, KV_LORA),
                           _row_spec(tm, QK_ROPE), _row_spec(tm, LANES)],
                name="mla_in_proj")
            ckv_rows.append(ckv.reshape(bsz, seq, KV_LORA))
            krope_rows.append(krope.reshape(bsz, seq, QK_ROPE))
            tn_q = 8 * MLA_QDIM
            q = matmul(cq, w['mla_w_uq'], layer=l, tm=tm_l, tn=tn_q, epilogue=_epi_mla_q,
                       extras=(mla_cos, mla_sin), extra_specs=(tab_l, tab_l),
                       out_shapes=jax.ShapeDtypeStruct((m, MLA_HEADS * MLA_QDIM), BF16),
                       out_specs=_tile_spec(tm_l, tn_q), name="mla_q_proj")
            if cache is None:
                ckv_all = ckv_b.reshape(bsz, seq, KV_LORA)
                kr_all = kr_b.reshape(bsz, seq, LANES)
                k_valid = seq
                tq = tk = min(seq, 1024)
            else:
                k_valid = past + seq
                sk = -(-k_valid // LANES) * LANES
                ckv_all = jnp.concatenate(
                    [cache['ckv'][l].astype(BF16), ckv_b.reshape(bsz, seq, KV_LORA),
                     jnp.zeros((bsz, sk - k_valid, KV_LORA), BF16)], axis=1)
                kr_cache = jnp.pad(cache['krope'][l], ((0, 0), (0, 0), (0, LANES - QK_ROPE)))
                kr_all = jnp.concatenate(
                    [kr_cache.astype(BF16), kr_b.reshape(bsz, seq, LANES),
                     jnp.zeros((bsz, sk - k_valid, LANES), BF16)], axis=1)
                tq, tk = LANES, sk
            sk = ckv_all.shape[1]
            mk = bsz * sk
            tmk = _tile(mk, 1024)
            kn = matmul(ckv_all.reshape(mk, KV_LORA), w['mla_w_uk'], layer=l, tm=tmk, tn=2048,
                        epilogue=_epi_cast,
                        out_shapes=jax.ShapeDtypeStruct((mk, MLA_HEADS * QK_NOPE), BF16),
                        out_specs=_tile_spec(tmk, 2048), name="mla_k_up_proj")
            if sk % 512 == 0:
                vt = matmul_nt(w['mla_w_uvt'], l, ckv_all, tn=MLA_HEADS * V_DIM, ts=512)
            else:
                vt = matmul_nt(w['mla_w_uvt'], l, ckv_all, tn=1024, ts=sk)
            q3 = q.reshape(bsz, seq, MLA_HEADS * MLA_QDIM)
            if seq < tq:
                q3 = jnp.pad(q3, ((0, 0), (0, tq - seq), (0, 0)))
            o = mla_attention(q3, kn.reshape(bsz, sk, MLA_HEADS * QK_NOPE), kr_all, vt,
                              q_off=past, k_valid=k_valid, tq=tq, tk=tk)
            h = out_proj(o[:, :seq].reshape(m, D_MODEL), w['mla_w_o'], l, h)
        else:
            i = l - N_A
            if l == N_A:
                ks, vs, ks_b, vs_t = matmul(
                    h, w['swa_w_kv'], tm=tm, tn=2 * SWA_KV_DIM, epilogue=_epi_shared_kv,
                    norm_x=True,
                    extras=(swa_c, swa_s1, swa_s2, p['kv_shared_norm'].reshape(1, D_MODEL)),
                    extra_specs=(tab, tab, tab, _const_spec(D_MODEL)),
                    out_shapes=[jax.ShapeDtypeStruct((m, SWA_KV_DIM), F32)] * 2
                    + [jax.ShapeDtypeStruct((m, SWA_KV_DIM), BF16),
                       jax.ShapeDtypeStruct((SWA_KV_DIM, m), BF16)],
                    out_specs=[_row_spec(tm, SWA_KV_DIM)] * 3
                    + [pl.BlockSpec((SWA_KV_DIM, tm), lambda i, j: (0, i))],
                    name="swa_shared_kv_proj")
            hn = rmsnorm(h, p['norm_attn'][l], BF16)
            q = matmul(hn, w['swa_w_q'], layer=i, tm=tm_l, tn=1024, epilogue=_epi_swa_q,
                       extras=(swa_c, swa_s1, swa_s2), extra_specs=(tab_l, tab_l, tab_l),
                       out_shapes=jax.ShapeDtypeStruct((m, D_MODEL), BF16),
                       out_specs=_tile_spec(tm_l, 1024), name="swa_q_proj")
            q3 = q.reshape(bsz, seq, D_MODEL)
            k3 = ks_b.reshape(bsz, seq, SWA_KV_DIM)
            vt3 = vs_t.reshape(SWA_KV_DIM, bsz, seq).transpose(1, 0, 2)
            sink = w['swa_sinks'][i]
            if cache is None:
                o = swa_attention(q3, k3, vt3, sink)
            else:
                win = cache['swa_k'].shape[1]
                npad = 2 * WINDOW - win - seq
                k_all = jnp.concatenate(
                    [cache['swa_k'].reshape(bsz, win, SWA_KV_DIM).astype(BF16), k3,
                     jnp.zeros((bsz, npad, SWA_KV_DIM), BF16)], axis=1)
                vt_cache = cache['swa_v'].reshape(bsz, win, SWA_KV_DIM).transpose(0, 2, 1)
                vt_all = jnp.concatenate(
                    [vt_cache.astype(BF16), vt3, jnp.zeros((bsz, SWA_KV_DIM, npad), BF16)], axis=2)
                q3 = jnp.pad(q3, ((0, 0), (0, WINDOW - seq), (0, 0)))
                o = swa_attention(q3, k_all, vt_all, sink, k_valid=win + seq)[:, :seq]
            h = out_proj(o.reshape(m, D_MODEL), w['swa_w_o'], i, h)

        hn = rmsnorm(h, p['norm_ffn'][l], BF16)
        if cache is None:
            prev = jnp.zeros((bsz, SUBLANES, F_PAD), F32)
        else:
            prev = jnp.pad(cache['conv'][l],
                           ((0, 0), (SUBLANES - (CONV_W - 1), 0), (0, F_PAD - D_FF)))
        act, gl = ffn_a(hn, w['ffn_w_gate'], w['ffn_w_up'], l, w['ffn_conv_w'][l],
                        w['ffn_conv_b'][l], prev, seq)
        gl = gl.reshape(bsz, -1, SUBLANES, F_PAD)
        conv_rows.append(gl[:, -1, SUBLANES - (CONV_W - 1):, :D_FF])
        h = matmul(act, w['ffn_w_down'], layer=l, tm=tm, tn=512, epilogue=_epi_residual,
                   extras=(h,), extra_specs=(_tile_spec(tm, 512),),
                   out_shapes=jax.ShapeDtypeStruct((m, D_MODEL), F32),
                   out_specs=_tile_spec(tm, 512), name="ffn_down_residual")

    y = rmsnorm(h, p['norm_final'], F32).reshape(bsz, seq, D_MODEL)
    win = min(WINDOW, seq) if cache is None else seq
    ks4 = ks.reshape(bsz, seq, SWA_KV_DIM)[:, seq - win:].reshape(bsz, win, SWA_KV_HEADS, SWA_HEAD_DIM)
    vs4 = vs.reshape(bsz, seq, SWA_KV_DIM)[:, seq - win:].reshape(bsz, win, SWA_KV_HEADS, SWA_HEAD_DIM)
    return y, jnp.stack(ckv_rows), jnp.stack(krope_rows), ks4, vs4, jnp.stack(conv_rows)


def kernel(x_prompt, x_sample, cache_mla_ckv, cache_mla_krope, cache_swa_k, cache_swa_v, state_conv,
           norm_attn, norm_ffn, mla_w_a, mla_g_q, mla_g_kv, mla_w_uq, mla_w_uk, mla_w_uv, mla_w_o,
           kv_shared_norm, swa_w_kv, swa_w_q, swa_sinks, swa_w_o,
           ffn_w_gate, ffn_w_up, ffn_conv_w, ffn_conv_b, ffn_w_down, norm_final):
    p = {
        'norm_attn': norm_attn, 'norm_ffn': norm_ffn,
        'mla_w_a': mla_w_a, 'mla_g_q': mla_g_q, 'mla_g_kv': mla_g_kv, 'mla_w_uq': mla_w_uq,
        'mla_w_uk': mla_w_uk, 'mla_w_uv': mla_w_uv, 'mla_w_o': mla_w_o,
        'kv_shared_norm': kv_shared_norm, 'swa_w_kv': swa_w_kv, 'swa_w_q': swa_w_q,
        'swa_sinks': swa_sinks, 'swa_w_o': swa_w_o,
        'ffn_w_gate': ffn_w_gate, 'ffn_w_up': ffn_w_up, 'ffn_conv_w': ffn_conv_w,
        'ffn_conv_b': ffn_conv_b, 'ffn_w_down': ffn_w_down, 'norm_final': norm_final,
    }
    w = _prep_weights(p)
    pos_p = jnp.arange(x_prompt.shape[1])
    out_p = _trunk(x_prompt, pos_p, p, w, None)
    past = cache_mla_ckv.shape[2]
    pos_s = past + jnp.arange(x_sample.shape[1])
    cache = {'ckv': cache_mla_ckv, 'krope': cache_mla_krope, 'swa_k': cache_swa_k,
             'swa_v': cache_swa_v, 'conv': state_conv}
    out_s = _trunk(x_sample, pos_s, p, w, cache)
    return (out_p[0], out_s[0]) + out_p[1:] + out_s[1:]
```

```python
import functools

import jax
import jax.numpy as jnp
from jax import lax
from jax.experimental import pallas as pl
from jax.experimental.pallas import tpu as pltpu

D_MODEL = 4096
DEPTH = 4
CHUNK = 64
N_A = DEPTH // 2
N_B = DEPTH - N_A
ROPE_THETA = 500000.0
EPS = 1e-6
NEG_INF = -1e30
MLA_HEADS = D_MODEL // 128
Q_LORA = D_MODEL // 4
KV_LORA = 512
QK_NOPE = 128
QK_ROPE = 64
V_DIM = 128
MLA_SCALE = (QK_NOPE + QK_ROPE) ** -0.5
LOG2E = 1.4426950408889634
MLA_Q_SCALE = MLA_SCALE * LOG2E
SWA_HEAD_DIM = 64
SWA_HEADS = D_MODEL // SWA_HEAD_DIM
SWA_KV_HEADS = 8
SWA_GROUP = SWA_HEADS // SWA_KV_HEADS
WINDOW = 128
ROPE_DIM_B = SWA_HEAD_DIM // 4
SWA_SCALE = SWA_HEAD_DIM ** -0.5
D_FF = 256 * ((8 * D_MODEL // 3 + 255) // 256)
CONV_W = 3
FFN_TF = 512
F_PAD = -(-D_FF // FFN_TF) * FFN_TF

LANES = 128
SUBLANES = 8
V7X_VMEM_LIMIT = 56 * 1024 * 1024
V7X_VMEM_LIMIT_FULL_ROWS = 61 * 1024 * 1024

MLA_QDIM = 2 * LANES
DENOM_ROWS = 16
NORM_ROWS = 32
SWA_KV_DIM = SWA_KV_HEADS * SWA_HEAD_DIM
HEAD_SLAB = 2 * LANES
BF16 = jnp.bfloat16
F32 = jnp.float32


def _params(n_grid, vmem_limit=V7X_VMEM_LIMIT):
    return pltpu.CompilerParams(dimension_semantics=("arbitrary",) * n_grid,
                                vmem_limit_bytes=vmem_limit)


def _tile(m, cap):
    for step in (LANES, 16):
        t = (min(cap, m) // step) * step
        while t >= step:
            if m % t == 0:
                return t
            t -= step
    return m


def _rms(x, g):
    return x * lax.rsqrt(jnp.mean(x * x, axis=-1, keepdims=True) + EPS) * g


def _rmsnorm_kernel(x_ref, g_ref, o_ref):
    o_ref[...] = _rms(x_ref[...], g_ref[...]).astype(o_ref.dtype)


def rmsnorm(x, g, out_dtype):
    m, d = x.shape
    tm = _tile(m, 256)
    return pl.pallas_call(
        _rmsnorm_kernel,
        grid=(m // tm,),
        in_specs=[pl.BlockSpec((tm, d), lambda i: (i, 0)),
                  pl.BlockSpec((1, d), lambda i: (0, 0))],
        out_specs=pl.BlockSpec((tm, d), lambda i: (i, 0)),
        out_shape=jax.ShapeDtypeStruct((m, d), out_dtype),
        compiler_params=_params(1),
        name="rmsnorm",
    )(x, g.reshape(1, d))


def _mm_kernel(x_ref, w_ref, *rest, epilogue, n_extra, norm_x):
    extras = rest[:n_extra]
    if norm_x:
        x = _rms(x_ref[...], extras[-1][...]).astype(BF16)
    else:
        x = x_ref[...]
    y = jnp.dot(x, w_ref[...], preferred_element_type=F32)
    epilogue(y, extras, rest[n_extra:])


def matmul(x, w, *, tm, tn, epilogue, extras=(), extra_specs=(), out_shapes, out_specs, name,
           norm_x=False, layer=None):
    m, k = x.shape
    n = w.shape[-1]
    assert not norm_x or tn == n
    w_mode = dict(pipeline_mode=pl.Buffered(1)) if tn == n else {}
    if layer is None:
        w_spec = pl.BlockSpec((k, tn), lambda i, j: (0, j), **w_mode)
    else:
        w_spec = pl.BlockSpec((None, k, tn), lambda i, j: (layer, 0, j), **w_mode)
    return pl.pallas_call(
        functools.partial(_mm_kernel, epilogue=epilogue, n_extra=len(extras), norm_x=norm_x),
        grid=(m // tm, n // tn),
        in_specs=[pl.BlockSpec((tm, k), lambda i, j: (i, 0)), w_spec] + list(extra_specs),
        out_specs=out_specs,
        out_shape=out_shapes,
        compiler_params=_params(2),
        name=name,
    )(x, w, *extras)


def _mm_res_norm_kernel(x_ref, w_ref, res_ref, g_ref, h_ref, hn_ref, *, nk, tn):
    k = pl.program_id(1)

    def accumulate(base_ref):
        x = x_ref[...]
        for c in range(h_ref.shape[1] // tn):
            cols = slice(c * tn, (c + 1) * tn)
            h_ref[:, cols] = base_ref[:, cols] + jnp.dot(x, w_ref[:, cols], preferred_element_type=F32)

    @pl.when(k == 0)
    def _():
        accumulate(res_ref)

    @pl.when(k > 0)
    def _():
        accumulate(h_ref)

    @pl.when(k == nk - 1)
    def _():
        g = g_ref[...]
        for r in range(h_ref.shape[0] // NORM_ROWS):
            rows = slice(r * NORM_ROWS, (r + 1) * NORM_ROWS)
            hn_ref[rows, :] = _rms(h_ref[rows, :], g).astype(hn_ref.dtype)


def matmul_res_norm(x, w, layer, res, gain, *, tm, tk, tn=1024):
    m, kdim = x.shape
    n = w.shape[-1]
    nk = kdim // tk
    return pl.pallas_call(
        functools.partial(_mm_res_norm_kernel, nk=nk, tn=tn),
        grid=(m // tm, nk),
        in_specs=[pl.BlockSpec((tm, tk), lambda i, k: (i, k)),
                  pl.BlockSpec((None, tk, n), lambda i, k: (layer, k, 0)),
                  pl.BlockSpec((tm, n), lambda i, k: (i, 0), pipeline_mode=pl.Buffered(1)),
                  pl.BlockSpec((1, n), lambda i, k: (0, 0))],
        out_specs=[pl.BlockSpec((tm, n), lambda i, k: (i, 0)),
                   pl.BlockSpec((tm, n), lambda i, k: (i, 0))],
        out_shape=[jax.ShapeDtypeStruct((m, n), F32), jax.ShapeDtypeStruct((m, n), BF16)],
        compiler_params=_params(2, V7X_VMEM_LIMIT_FULL_ROWS),
        name="matmul_residual_norm",
    )(x, w, res, gain.reshape(1, n))


def _epi_cast(y, extras, outs):
    outs[0][...] = y.astype(outs[0].dtype)


def _epi_residual(y, extras, outs):
    outs[0][...] = extras[0][...] + y


def _rope_slab(a, cos2, sin2):
    return a * cos2 + pltpu.roll(a, QK_ROPE, axis=1) * sin2


def _epi_mla_in(y, extras, outs):
    gq_ref, gkv_ref, cos_ref, sin_ref = extras[:4]
    cq_ref, ckv_ref, ckvb_ref, kr_ref, krb_ref = outs
    cq_ref[...] = _rms(y[:, :Q_LORA], gq_ref[...]).astype(BF16)
    ckv = _rms(y[:, Q_LORA:Q_LORA + KV_LORA], gkv_ref[...])
    ckv_ref[...] = ckv
    ckvb_ref[...] = ckv.astype(BF16)
    kr = _rope_slab(y[:, Q_LORA + KV_LORA:], cos_ref[...], sin_ref[...])
    kr_ref[...] = kr[:, :QK_ROPE]
    krb_ref[...] = kr.astype(BF16)


def _epi_mla_q(y, extras, outs):
    cos_ref, sin_ref = extras
    cos2 = cos_ref[...] * MLA_Q_SCALE
    sin2 = sin_ref[...] * MLA_Q_SCALE
    for s in range(y.shape[1] // MLA_QDIM):
        lo = s * MLA_QDIM
        outs[0][:, lo:lo + LANES] = (y[:, lo:lo + LANES] * MLA_Q_SCALE).astype(BF16)
        outs[0][:, lo + LANES:lo + MLA_QDIM] = _rope_slab(
            y[:, lo + LANES:lo + MLA_QDIM], cos2, sin2).astype(BF16)


def _rope_b(y, c, s1, s2, scale):
    half = ROPE_DIM_B // 2
    for s in range(y.shape[1] // LANES):
        a = y[:, s * LANES:(s + 1) * LANES]
        r = a * c + pltpu.roll(a, LANES - half, axis=1) * s1 + pltpu.roll(a, half, axis=1) * s2
        yield s, (r * scale if scale != 1.0 else r)


def _epi_swa_q(y, extras, outs):
    c_ref, s1_ref, s2_ref = extras
    for s, r in _rope_b(y, c_ref[...], s1_ref[...], s2_ref[...], SWA_SCALE):
        outs[0][:, s * LANES:(s + 1) * LANES] = r.astype(BF16)


def _epi_shared_kv(y, extras, outs):
    c_ref, s1_ref, s2_ref = extras[:3]
    k_ref, v_ref, kb_ref, vtb_ref = outs
    for s, r in _rope_b(y[:, :SWA_KV_DIM], c_ref[...], s1_ref[...], s2_ref[...], 1.0):
        k_ref[:, s * LANES:(s + 1) * LANES] = r
        kb_ref[:, s * LANES:(s + 1) * LANES] = r.astype(BF16)
    v = y[:, SWA_KV_DIM:]
    v_ref[...] = v
    vtb_ref[...] = v.T.astype(BF16)


def _row_spec(tm, width):
    return pl.BlockSpec((tm, width), lambda i, j: (i, 0))


def _tile_spec(tm, tn):
    return pl.BlockSpec((tm, tn), lambda i, j: (i, j))


def _const_spec(width):
    return pl.BlockSpec((1, width), lambda i, j: (0, 0))


def _ffn_a_kernel(x_ref, wg_ref, wu_ref, cw_ref, cb_ref, prev_ref, act_ref, gl_ref, gbuf, carry,
                  *, seg, nseg, tiles_per_seq, sub, tail_chunks):
    i = pl.program_id(0)
    j = pl.program_id(1)
    if tiles_per_seq > 1:
        @pl.when(jnp.logical_and(i == 0, j == 0))
        def _():
            carry[...] = jnp.zeros(carry.shape, F32)

        gbuf[0:SUBLANES] = jnp.where(i % tiles_per_seq == 0, prev_ref[0], carry[j])
    else:
        for s in range(nseg):
            gbuf[s * SUBLANES:(s + 1) * SUBLANES] = prev_ref[s]
    x = x_ref[...]
    n_sub = act_ref.shape[1] // sub
    row = lax.broadcasted_iota(jnp.int32, (SUBLANES, sub), 0)
    for c in range(n_sub):
        cols = slice(c * sub, (c + 1) * sub)
        w0 = cw_ref[0:1, cols]
        w1 = cw_ref[1:2, cols]
        w2 = cw_ref[2:3, cols]
        b = cb_ref[:, cols]
        n_chunk = tail_chunks if (c == n_sub - 1 and nseg == 1) else 1
        rows = seg // n_chunk
        if n_chunk == 1:
            g = jnp.dot(x, wg_ref[:, cols], preferred_element_type=F32)
            u = jnp.dot(x, wu_ref[:, cols], preferred_element_type=F32)
        for s in range(nseg):
            halo = gbuf[s * SUBLANES:(s + 1) * SUBLANES, cols]
            for k in range(n_chunk):
                r0 = s * seg + k * rows
                if n_chunk == 1:
                    gs, us = g[r0:r0 + rows], u[r0:r0 + rows]
                else:
                    gs = jnp.dot(x[r0:r0 + rows], wg_ref[:, cols], preferred_element_type=F32)
                    us = jnp.dot(x[r0:r0 + rows], wu_ref[:, cols], preferred_element_type=F32)
                r1 = pltpu.roll(gs, 1, axis=0)
                r2 = pltpu.roll(gs, 2, axis=0)
                h1 = jnp.where(row == 0, halo[SUBLANES - 1:SUBLANES], r1[:SUBLANES])
                h2 = jnp.where(row == 0, halo[SUBLANES - 2:SUBLANES - 1],
                               jnp.where(row == 1, halo[SUBLANES - 1:SUBLANES], r2[:SUBLANES]))
                g1 = jnp.concatenate([h1, r1[SUBLANES:]], axis=0)
                g2 = jnp.concatenate([h2, r2[SUBLANES:]], axis=0)
                gc = b + w0 * g2 + w1 * g1 + w2 * gs
                a = gc * jax.nn.sigmoid(gc) * us
                act_ref[r0:r0 + rows, cols] = a.astype(BF16)
                halo = gs[rows - SUBLANES:]
            gl_ref[s, :, cols] = halo
        if tiles_per_seq > 1:
            carry[j, :, cols] = halo


def ffn_a(hn, wg, wu, layer, cw, cb, prev, seq_len, *, tm_cap=1024, tf=FFN_TF, sub=2 * LANES):
    m, d = hn.shape
    f = wg.shape[2]
    tm = _tile(m, tm_cap)
    seg = min(seq_len, tm)
    nseg = tm // seg
    tiles_per_seq = seq_len // seg
    nj = f // tf
    if nseg == 1:
        prev_map = lambda i, j: (i // tiles_per_seq, 0, j)
    else:
        prev_map = lambda i, j: (i, 0, j)
    return pl.pallas_call(
        functools.partial(_ffn_a_kernel, seg=seg, nseg=nseg, tiles_per_seq=tiles_per_seq, sub=sub,
                          tail_chunks=4),
        grid=(m // tm, nj),
        in_specs=[pl.BlockSpec((tm, d), lambda i, j: (i, 0)),
                  pl.BlockSpec((None, d, tf), lambda i, j: (layer, 0, j)),
                  pl.BlockSpec((None, d, tf), lambda i, j: (layer, 0, j)),
                  pl.BlockSpec((CONV_W, tf), lambda i, j: (0, j)),
                  pl.BlockSpec((1, tf), lambda i, j: (0, j)),
                  pl.BlockSpec((nseg, SUBLANES, tf), prev_map)],
        out_specs=[pl.BlockSpec((tm, tf), lambda i, j: (i, j)),
                   pl.BlockSpec((nseg, SUBLANES, tf), lambda i, j: (i, 0, j))],
        out_shape=[jax.ShapeDtypeStruct((m, f), BF16),
                   jax.ShapeDtypeStruct((m // seg, SUBLANES, f), F32)],
        scratch_shapes=[pltpu.VMEM((nseg * SUBLANES, tf), F32),
                        pltpu.VMEM((nj, SUBLANES, tf), F32)],
        compiler_params=_params(2),
        name="ffn_gate_up_conv",
    )(hn, wg, wu, cw, cb.reshape(1, f), prev)


def _mla_attn_kernel(q_ref, kn_ref, kr_ref, vt_ref, o_ref, m_ref, acc_ref, s_ref, p_ref,
                     *, hb, tq, tk, nk, q_off, k_valid, cg, rb):
    qi = pl.program_id(2)
    ki = pl.program_id(3)
    q_lo = q_off + qi * tq
    k_lo = ki * tk

    @pl.when(ki == 0)
    def _():
        m_ref[...] = jnp.full(m_ref.shape, NEG_INF, F32)
        acc_ref[...] = jnp.zeros(acc_ref.shape, F32)

    needed = k_lo // CHUNK <= (q_lo + tq - 1) // CHUNK
    full = jnp.logical_and((k_lo + tk - 1) // CHUNK <= q_lo // CHUNK, k_lo + tk <= k_valid)

    def step(masked):
        kr = kr_ref[0]
        ones = jnp.ones((DENOM_ROWS, tk), BF16)
        for h in range(hb):
            buf = h % s_ref.shape[0]
            q = q_ref[0, :, h * MLA_QDIM:(h + 1) * MLA_QDIM]
            k = jnp.concatenate([kn_ref[0, :, h * QK_NOPE:(h + 1) * QK_NOPE], kr], axis=1)
            s_ref[buf] = lax.dot_general(k, q, (((1,), (1,)), ((), ())), preferred_element_type=F32)
            alphas = []
            for c in range(tq // cg):
                cols = slice(c * cg, (c + 1) * cg)
                if masked:
                    qc = (q_lo + c * cg + lax.broadcasted_iota(jnp.int32, (1, cg), 1)) // CHUNK

                def scores(r):
                    s = s_ref[buf, r * rb:(r + 1) * rb, cols]
                    if masked:
                        kp = k_lo + r * rb + lax.broadcasted_iota(jnp.int32, (rb, 1), 0)
                        s = jnp.where(jnp.logical_and(qc >= kp // CHUNK, kp < k_valid), s, NEG_INF)
                    return s

                part = None
                for r in range(tk // rb):
                    blk = jnp.max(scores(r).reshape(rb // SUBLANES, SUBLANES, cg), axis=0)
                    part = blk if part is None else jnp.maximum(part, blk)
                m_prev = m_ref[h, :, cols]
                m_new = jnp.maximum(m_prev, jnp.max(part, axis=0, keepdims=True))
                alphas.append(jnp.exp2(m_prev - m_new))
                m_ref[h, :, cols] = m_new
                for r in range(tk // rb):
                    p_ref[buf, r * rb:(r + 1) * rb, cols] = jnp.exp2(scores(r) - m_new).astype(BF16)
            vt = jnp.concatenate([vt_ref[0, h * V_DIM:(h + 1) * V_DIM, :], ones], axis=0)
            pv = jnp.dot(vt, p_ref[buf], preferred_element_type=F32)
            acc_ref[h] = jnp.concatenate(alphas, axis=1) * acc_ref[h] + pv

    @pl.when(jnp.logical_and(needed, full))
    def _():
        step(False)

    @pl.when(jnp.logical_and(needed, jnp.logical_not(full)))
    def _():
        step(True)

    @pl.when(ki == nk - 1)
    def _():
        for h in range(hb):
            o_t = acc_ref[h, :V_DIM, :] / acc_ref[h, V_DIM:V_DIM + 1, :]
            o_ref[0, :, h * V_DIM:(h + 1) * V_DIM] = o_t.T.astype(BF16)


def mla_attention(q, kn, kr, vt, *, q_off, k_valid, tq, tk, hb=8, n_buf=2):
    bsz, sq, _ = q.shape
    sk = kn.shape[1]
    nq, nk = sq // tq, sk // tk
    n_hg = MLA_HEADS // hb

    def k_idx(qi, ki):
        last = ((q_off + (qi + 1) * tq - 1) // CHUNK * CHUNK + CHUNK - 1) // tk
        return jnp.minimum(ki, jnp.minimum(last, nk - 1))

    return pl.pallas_call(
        functools.partial(_mla_attn_kernel, hb=hb, tq=tq, tk=tk, nk=nk, q_off=q_off,
                          k_valid=k_valid, cg=min(tq, 2 * LANES), rb=LANES),
        grid=(bsz, n_hg, nq, nk),
        in_specs=[pl.BlockSpec((1, tq, hb * MLA_QDIM), lambda b, g, qi, ki: (b, qi, g)),
                  pl.BlockSpec((1, tk, hb * QK_NOPE), lambda b, g, qi, ki: (b, k_idx(qi, ki), g)),
                  pl.BlockSpec((1, tk, LANES), lambda b, g, qi, ki: (b, k_idx(qi, ki), 0)),
                  pl.BlockSpec((1, hb * V_DIM, tk), lambda b, g, qi, ki: (b, g, k_idx(qi, ki)))],
        out_specs=pl.BlockSpec((1, tq, hb * V_DIM), lambda b, g, qi, ki: (b, qi, g)),
        out_shape=jax.ShapeDtypeStruct((bsz, sq, MLA_HEADS * V_DIM), BF16),
        scratch_shapes=[pltpu.VMEM((hb, 1, tq), F32),
                        pltpu.VMEM((hb, V_DIM + DENOM_ROWS, tq), F32),
                        pltpu.VMEM((n_buf, tk, tq), F32),
                        pltpu.VMEM((n_buf, tk, tq), BF16)],
        compiler_params=_params(4),
        name="mla_flash_attention",
    )(q, kn, kr, vt)


def _nt_kernel(w_ref, x_ref, o_ref):
    o_ref[0] = lax.dot_general(w_ref[...], x_ref[0], (((1,), (1,)), ((), ())),
                               preferred_element_type=F32).astype(o_ref.dtype)


def matmul_nt(wt, layer, x, *, tn, ts):
    _, n, k = wt.shape
    bsz, s, _ = x.shape
    return pl.pallas_call(
        _nt_kernel,
        grid=(bsz, s // ts, n // tn),
        in_specs=[pl.BlockSpec((None, tn, k), lambda b, i, j: (layer, j, 0)),
                  pl.BlockSpec((1, ts, k), lambda b, i, j: (b, i, 0))],
        out_specs=pl.BlockSpec((1, tn, ts), lambda b, i, j: (b, j, i)),
        out_shape=jax.ShapeDtypeStruct((bsz, n, s), BF16),
        compiler_params=_params(3),
        name="mla_v_up_proj_t",
    )(wt, x)


def _swa_attn_kernel(sink_ref, q_ref, ka_ref, kb_ref, vta_ref, vtb_ref, o_ref, ot_ref, *, tq, k_valid):
    t = pl.program_id(1)
    k = jnp.concatenate([ka_ref[0], kb_ref[0]], axis=0)
    vt = jnp.concatenate([vta_ref[0], vtb_ref[0]], axis=1)
    tk = k.shape[0]
    kp = lax.broadcasted_iota(jnp.int32, (tk, 1), 0)
    if k_valid is None:
        kp = kp + (t - 1) * tq
        kc = kp // CHUNK
        qc = (t * tq + lax.broadcasted_iota(jnp.int32, (1, tq), 1)) // CHUNK
        valid = jnp.logical_and(kp >= 0, jnp.logical_and(kc >= qc - WINDOW // CHUNK, kc <= qc))
    else:
        valid = jnp.broadcast_to(kp < k_valid, (tk, tq))
    valid = jnp.concatenate([valid] * SWA_GROUP, axis=1)
    lane_head = lax.broadcasted_iota(jnp.int32, (1, HEAD_SLAB), 1) // SWA_HEAD_DIM
    heads_per_slab = HEAD_SLAB // SWA_HEAD_DIM
    for slab in range(SWA_KV_DIM // HEAD_SLAB):
        k_slab = k[:, slab * HEAD_SLAB:(slab + 1) * HEAD_SLAB]
        qs = jnp.concatenate(
            [q_ref[0, :, g * SWA_KV_DIM + slab * HEAD_SLAB:g * SWA_KV_DIM + (slab + 1) * HEAD_SLAB]
             for g in range(SWA_GROUP)], axis=0)
        for hh in range(heads_per_slab):
            kvh = slab * heads_per_slab + hh
            km = k_slab * (lane_head == hh).astype(BF16)
            vth = vt[kvh * SWA_HEAD_DIM:(kvh + 1) * SWA_HEAD_DIM, :]
            sb = jnp.concatenate([jnp.full((1, tq), sink_ref[kvh, g], F32) for g in range(SWA_GROUP)],
                                 axis=1)
            s = lax.dot_general(km, qs, (((1,), (1,)), ((), ())), preferred_element_type=F32)
            s = jnp.where(valid, s, NEG_INF)
            m = jnp.maximum(jnp.max(s, axis=0, keepdims=True), sb)
            e = jnp.exp(s - m)
            inv = 1.0 / (jnp.sum(e, axis=0, keepdims=True) + jnp.exp(sb - m))
            o_t = jnp.dot(vth, (e * inv).astype(BF16), preferred_element_type=F32)
            for g in range(SWA_GROUP):
                row = g * SWA_KV_DIM + kvh * SWA_HEAD_DIM
                ot_ref[row:row + SWA_HEAD_DIM, :] = o_t[:, g * tq:(g + 1) * tq]
    for c in range(D_MODEL // LANES):
        o_ref[0, :, c * LANES:(c + 1) * LANES] = ot_ref[c * LANES:(c + 1) * LANES, :].T.astype(BF16)


def swa_attention(q, k, vt, sink, *, k_valid=None):
    bsz, sq, _ = q.shape
    tq = WINDOW
    if k_valid is None:
        a_idx = lambda t: jnp.maximum(t - 1, 0)
        b_idx = lambda t: t
    else:
        assert sq == tq and k.shape[1] == 2 * WINDOW
        a_idx = lambda t: 0
        b_idx = lambda t: 1
    return pl.pallas_call(
        functools.partial(_swa_attn_kernel, tq=tq, k_valid=k_valid),
        grid=(bsz, sq // tq),
        in_specs=[pl.BlockSpec(memory_space=pltpu.SMEM),
                  pl.BlockSpec((1, tq, D_MODEL), lambda b, t: (b, t, 0)),
                  pl.BlockSpec((1, WINDOW, SWA_KV_DIM), lambda b, t: (b, a_idx(t), 0)),
                  pl.BlockSpec((1, WINDOW, SWA_KV_DIM), lambda b, t: (b, b_idx(t), 0)),
                  pl.BlockSpec((1, SWA_KV_DIM, WINDOW), lambda b, t: (b, 0, a_idx(t))),
                  pl.BlockSpec((1, SWA_KV_DIM, WINDOW), lambda b, t: (b, 0, b_idx(t)))],
        out_specs=pl.BlockSpec((1, tq, D_MODEL), lambda b, t: (b, t, 0)),
        out_shape=jax.ShapeDtypeStruct((bsz, sq, D_MODEL), BF16),
        scratch_shapes=[pltpu.VMEM((D_MODEL, tq), F32)],
        compiler_params=_params(2),
        name="swa_sink_attention",
    )(sink, q, k, k, vt, vt)


def _cast_pad_cols_kernel(x_ref, o_ref):
    n = x_ref.shape[-1]
    o_ref[0, :, :n] = x_ref[0].astype(BF16)
    o_ref[0, :, n:] = jnp.zeros((o_ref.shape[1], o_ref.shape[2] - n), BF16)


def cast_pad_cols(x, n_pad, tr=256):
    nl, r, c = x.shape
    return pl.pallas_call(
        _cast_pad_cols_kernel,
        grid=(nl, r // tr),
        in_specs=[pl.BlockSpec((1, tr, c), lambda l, i: (l, i, 0))],
        out_specs=pl.BlockSpec((1, tr, n_pad), lambda l, i: (l, i, 0)),
        out_shape=jax.ShapeDtypeStruct((nl, r, n_pad), BF16),
        compiler_params=_params(2),
        name="cast_pad_cols",
    )(x)


def _cast_pad_rows_kernel(x_ref, o_ref, *, n_blocks):
    i = pl.program_id(1)

    @pl.when(i < n_blocks)
    def _():
        o_ref[0] = x_ref[0].astype(BF16)

    @pl.when(i >= n_blocks)
    def _():
        o_ref[0] = jnp.zeros(o_ref.shape[1:], BF16)


def cast_pad_rows(x, r_pad, tr=256):
    nl, r, c = x.shape
    n_blocks = r // tr
    return pl.pallas_call(
        functools.partial(_cast_pad_rows_kernel, n_blocks=n_blocks),
        grid=(nl, r_pad // tr),
        in_specs=[pl.BlockSpec((1, tr, c), lambda l, i: (l, jnp.minimum(i, n_blocks - 1), 0))],
        out_specs=pl.BlockSpec((1, tr, c), lambda l, i: (l, i, 0)),
        out_shape=jax.ShapeDtypeStruct((nl, r_pad, c), BF16),
        compiler_params=_params(2),
        name="cast_pad_rows",
    )(x)


def _rope_slab_cols(w_rope):
    half = QK_ROPE // 2
    x1, x2 = w_rope[..., :half], w_rope[..., half:]
    return jnp.concatenate([x1, x2, x2, x1], axis=-1)


def _mla_tables(pos):
    half = QK_ROPE // 2
    inv = jnp.power(jnp.float32(ROPE_THETA), -jnp.arange(half, dtype=F32) * (2.0 / QK_ROPE))
    ang = pos.astype(F32)[:, None] * inv[None, :]
    c, s = jnp.cos(ang), jnp.sin(ang)
    z = jnp.zeros_like(c)
    return jnp.concatenate([c, c, z, z], axis=1), jnp.concatenate([-s, s, z, z], axis=1)


def _swa_tables(pos):
    half = ROPE_DIM_B // 2
    inv = jnp.power(jnp.float32(ROPE_THETA), -jnp.arange(half, dtype=F32) * (2.0 / ROPE_DIM_B))
    ang = pos.astype(F32)[:, None] * inv[None, :]
    c, s = jnp.cos(ang), jnp.sin(ang)
    n = pos.shape[0]
    rest = SWA_HEAD_DIM - ROPE_DIM_B
    c64 = jnp.concatenate([c, c, jnp.ones((n, rest), F32)], axis=1)
    s1 = jnp.concatenate([-s, jnp.zeros((n, SWA_HEAD_DIM - half), F32)], axis=1)
    s2 = jnp.concatenate([jnp.zeros((n, half), F32), s, jnp.zeros((n, rest), F32)], axis=1)
    rep = LANES // SWA_HEAD_DIM
    return jnp.tile(c64, (1, rep)), jnp.tile(s1, (1, rep)), jnp.tile(s2, (1, rep))


def _prep_weights(p):
    w = {}
    wa = p['mla_w_a'].astype(BF16)
    w['mla_w_a'] = jnp.concatenate(
        [wa[..., :Q_LORA + KV_LORA], _rope_slab_cols(wa[..., Q_LORA + KV_LORA:])], axis=-1)
    wuq = p['mla_w_uq'].astype(BF16).reshape(N_A, Q_LORA, MLA_HEADS, QK_NOPE + QK_ROPE)
    w['mla_w_uq'] = jnp.concatenate(
        [wuq[..., :QK_NOPE], _rope_slab_cols(wuq[..., QK_NOPE:])], axis=-1
    ).reshape(N_A, Q_LORA, MLA_HEADS * MLA_QDIM)
    w['mla_w_uk'] = p['mla_w_uk'].reshape(N_A, KV_LORA, MLA_HEADS * QK_NOPE).astype(BF16)
    w['mla_w_uvt'] = p['mla_w_uv'].reshape(N_A, KV_LORA, MLA_HEADS * V_DIM).transpose(0, 2, 1).astype(BF16)
    w['mla_w_o'] = p['mla_w_o'].astype(BF16)
    w['swa_w_kv'] = p['swa_w_kv'].astype(BF16)
    wq = p['swa_w_q'].reshape(N_B, D_MODEL, SWA_KV_HEADS, SWA_GROUP, SWA_HEAD_DIM)
    w['swa_w_q'] = wq.transpose(0, 1, 3, 2, 4).reshape(N_B, D_MODEL, D_MODEL).astype(BF16)
    wo = p['swa_w_o'].reshape(N_B, SWA_KV_HEADS, SWA_GROUP, SWA_HEAD_DIM, D_MODEL)
    w['swa_w_o'] = wo.transpose(0, 2, 1, 3, 4).reshape(N_B, D_MODEL, D_MODEL).astype(BF16)
    w['swa_sinks'] = p['swa_sinks'].reshape(N_B, SWA_KV_HEADS, SWA_GROUP)
    fpad = F_PAD - D_FF
    w['ffn_w_gate'] = cast_pad_cols(p['ffn_w_gate'], F_PAD)
    w['ffn_w_up'] = cast_pad_cols(p['ffn_w_up'], F_PAD)
    w['ffn_w_down'] = cast_pad_rows(p['ffn_w_down'], F_PAD)
    w['ffn_conv_w'] = jnp.pad(p['ffn_conv_w'], ((0, 0), (0, 0), (0, fpad)))
    w['ffn_conv_b'] = jnp.pad(p['ffn_conv_b'], ((0, 0), (0, fpad)))
    return w


def _trunk(x, pos, p, w, cache):
    bsz, seq, _ = x.shape
    m = bsz * seq
    h = x.reshape(m, D_MODEL)
    tm = _tile(m, 512)
    tm_l = _tile(m, 1024)
    past = 0 if cache is None else cache['ckv'].shape[2]

    mla_cos, mla_sin = (jnp.tile(t, (bsz, 1)) for t in _mla_tables(pos))
    swa_c, swa_s1, swa_s2 = (jnp.tile(t, (bsz, 1)) for t in _swa_tables(pos))
    tab = _row_spec(tm, LANES)
    tab_l = _row_spec(tm_l, LANES)

    ckv_rows, krope_rows, conv_rows = [], [], []
    ks = vs = ks_b = vs_t = None
    hn = None
    for l in range(DEPTH):
        if l < N_A:
            n_a = w['mla_w_a'].shape[2]
            in_extras = (p['mla_g_q'][l].reshape(1, Q_LORA), p['mla_g_kv'][l].reshape(1, KV_LORA),
                         mla_cos, mla_sin)
            in_specs = (_const_spec(Q_LORA), _const_spec(KV_LORA), tab, tab)
            if hn is None:
                in_extras += (p['norm_attn'][l].reshape(1, D_MODEL),)
                in_specs += (_const_spec(D_MODEL),)
            cq, ckv, ckv_b, krope, kr_b = matmul(
                h if hn is None else hn, w['mla_w_a'], layer=l, tm=tm, tn=n_a,
                epilogue=_epi_mla_in, norm_x=hn is None, extras=in_extras, extra_specs=in_specs,
                out_shapes=[jax.ShapeDtypeStruct((m, Q_LORA), BF16),
                            jax.ShapeDtypeStruct((m, KV_LORA), F32),
                            jax.ShapeDtypeStruct((m, KV_LORA), BF16),
                            jax.ShapeDtypeStruct((m, QK_ROPE), F32),
                            jax.ShapeDtypeStruct((m, LANES), BF16)],
                out_specs=[_row_spec(tm, Q_LORA), _row_spec(tm, KV_LORA), _row_spec(tm, KV_LORA),
                           _row_spec(tm, QK_ROPE), _row_spec(tm, LANES)],
                name="mla_in_proj")
            ckv_rows.append(ckv.reshape(bsz, seq, KV_LORA))
            krope_rows.append(krope.reshape(bsz, seq, QK_ROPE))
            tn_q = 8 * MLA_QDIM
            q = matmul(cq, w['mla_w_uq'], layer=l, tm=tm_l, tn=tn_q, epilogue=_epi_mla_q,
                       extras=(mla_cos, mla_sin), extra_specs=(tab_l, tab_l),
                       out_shapes=jax.ShapeDtypeStruct((m, MLA_HEADS * MLA_QDIM), BF16),
                       out_specs=_tile_spec(tm_l, tn_q), name="mla_q_proj")
            if cache is None:
                ckv_all = ckv_b.reshape(bsz, seq, KV_LORA)
                kr_all = kr_b.reshape(bsz, seq, LANES)
                k_valid = seq
                tq = tk = min(seq, 1024)
            else:
                k_valid = past + seq
                sk = -(-k_valid // LANES) * LANES
                ckv_all = jnp.concatenate(
                    [cache['ckv'][l].astype(BF16), ckv_b.reshape(bsz, seq, KV_LORA),
                     jnp.zeros((bsz, sk - k_valid, KV_LORA), BF16)], axis=1)
                kr_cache = jnp.pad(cache['krope'][l], ((0, 0), (0, 0), (0, LANES - QK_ROPE)))
                kr_all = jnp.concatenate(
                    [kr_cache.astype(BF16), kr_b.reshape(bsz, seq, LANES),
                     jnp.zeros((bsz, sk - k_valid, LANES), BF16)], axis=1)
                tq, tk = LANES, sk
            sk = ckv_all.shape[1]
            mk = bsz * sk
            tmk = _tile(mk, 1024)
            kn = matmul(ckv_all.reshape(mk, KV_LORA), w['mla_w_uk'], layer=l, tm=tmk, tn=2048,
                        epilogue=_epi_cast,
                        out_shapes=jax.ShapeDtypeStruct((mk, MLA_HEADS * QK_NOPE), BF16),
                        out_specs=_tile_spec(tmk, 2048), name="mla_k_up_proj")
            if sk % 512 == 0:
                vt = matmul_nt(w['mla_w_uvt'], l, ckv_all, tn=MLA_HEADS * V_DIM, ts=512)
            else:
                vt = matmul_nt(w['mla_w_uvt'], l, ckv_all, tn=1024, ts=sk)
            q3 = q.reshape(bsz, seq, MLA_HEADS * MLA_QDIM)
            if seq < tq:
                q3 = jnp.pad(q3, ((0, 0), (0, tq - seq), (0, 0)))
            o = mla_attention(q3, kn.reshape(bsz, sk, MLA_HEADS * QK_NOPE), kr_all, vt,
                              q_off=past, k_valid=k_valid, tq=tq, tk=tk)
            h, hn = matmul_res_norm(o[:, :seq].reshape(m, D_MODEL), w['mla_w_o'], l, h,
                                    p['norm_ffn'][l], tm=tm, tk=1024)
        else:
            i = l - N_A
            if l == N_A:
                ks, vs, ks_b, vs_t = matmul(
                    h, w['swa_w_kv'], tm=tm, tn=2 * SWA_KV_DIM, epilogue=_epi_shared_kv,
                    norm_x=True,
                    extras=(swa_c, swa_s1, swa_s2, p['kv_shared_norm'].reshape(1, D_MODEL)),
                    extra_specs=(tab, tab, tab, _const_spec(D_MODEL)),
                    out_shapes=[jax.ShapeDtypeStruct((m, SWA_KV_DIM), F32)] * 2
                    + [jax.ShapeDtypeStruct((m, SWA_KV_DIM), BF16),
                       jax.ShapeDtypeStruct((SWA_KV_DIM, m), BF16)],
                    out_specs=[_row_spec(tm, SWA_KV_DIM)] * 3
                    + [pl.BlockSpec((SWA_KV_DIM, tm), lambda i, j: (0, i))],
                    name="swa_shared_kv_proj")
            q = matmul(hn, w['swa_w_q'], layer=i, tm=tm_l, tn=1024, epilogue=_epi_swa_q,
                       extras=(swa_c, swa_s1, swa_s2), extra_specs=(tab_l, tab_l, tab_l),
                       out_shapes=jax.ShapeDtypeStruct((m, D_MODEL), BF16),
                       out_specs=_tile_spec(tm_l, 1024), name="swa_q_proj")
            q3 = q.reshape(bsz, seq, D_MODEL)
            k3 = ks_b.reshape(bsz, seq, SWA_KV_DIM)
            vt3 = vs_t.reshape(SWA_KV_DIM, bsz, seq).transpose(1, 0, 2)
            sink = w['swa_sinks'][i]
            if cache is None:
                o = swa_attention(q3, k3, vt3, sink)
            else:
                win = cache['swa_k'].shape[1]
                npad = 2 * WINDOW - win - seq
                k_all = jnp.concatenate(
                    [cache['swa_k'].reshape(bsz, win, SWA_KV_DIM).astype(BF16), k3,
                     jnp.zeros((bsz, npad, SWA_KV_DIM), BF16)], axis=1)
                vt_cache = cache['swa_v'].reshape(bsz, win, SWA_KV_DIM).transpose(0, 2, 1)
                vt_all = jnp.concatenate(
                    [vt_cache.astype(BF16), vt3, jnp.zeros((bsz, SWA_KV_DIM, npad), BF16)], axis=2)
                q3 = jnp.pad(q3, ((0, 0), (0, WINDOW - seq), (0, 0)))
                o = swa_attention(q3, k_all, vt_all, sink, k_valid=win + seq)[:, :seq]
            h, hn = matmul_res_norm(o.reshape(m, D_MODEL), w['swa_w_o'], i, h,
                                    p['norm_ffn'][l], tm=tm, tk=1024)

        if cache is None:
            prev = jnp.zeros((bsz, SUBLANES, F_PAD), F32)
        else:
            prev = jnp.pad(cache['conv'][l],
                           ((0, 0), (SUBLANES - (CONV_W - 1), 0), (0, F_PAD - D_FF)))
        act, gl = ffn_a(hn, w['ffn_w_gate'], w['ffn_w_up'], l, w['ffn_conv_w'][l],
                        w['ffn_conv_b'][l], prev, seq)
        gl = gl.reshape(bsz, -1, SUBLANES, F_PAD)
        conv_rows.append(gl[:, -1, SUBLANES - (CONV_W - 1):, :D_FF])
        if l + 1 < DEPTH:
            h, hn = matmul_res_norm(act, w['ffn_w_down'], l, h, p['norm_attn'][l + 1], tm=tm, tk=1024)
        else:
            h = matmul(act, w['ffn_w_down'], layer=l, tm=tm, tn=512, epilogue=_epi_residual,
                       extras=(h,), extra_specs=(_tile_spec(tm, 512),),
                       out_shapes=jax.ShapeDtypeStruct((m, D_MODEL), F32),
                       out_specs=_tile_spec(tm, 512), name="ffn_down_residual")

    y = rmsnorm(h, p['norm_final'], F32).reshape(bsz, seq, D_MODEL)
    win = min(WINDOW, seq) if cache is None else seq
    ks4 = ks.reshape(bsz, seq, SWA_KV_DIM)[:, seq - win:].reshape(bsz, win, SWA_KV_HEADS, SWA_HEAD_DIM)
    vs4 = vs.reshape(bsz, seq, SWA_KV_DIM)[:, seq - win:].reshape(bsz, win, SWA_KV_HEADS, SWA_HEAD_DIM)
    return y, jnp.stack(ckv_rows), jnp.stack(krope_rows), ks4, vs4, jnp.stack(conv_rows)


def kernel(x_prompt, x_sample, cache_mla_ckv, cache_mla_krope, cache_swa_k, cache_swa_v, state_conv,
           norm_attn, norm_ffn, mla_w_a, mla_g_q, mla_g_kv, mla_w_uq, mla_w_uk, mla_w_uv, mla_w_o,
           kv_shared_norm, swa_w_kv, swa_w_q, swa_sinks, swa_w_o,
           ffn_w_gate, ffn_w_up, ffn_conv_w, ffn_conv_b, ffn_w_down, norm_final):
    p = {
        'norm_attn': norm_attn, 'norm_ffn': norm_ffn,
        'mla_w_a': mla_w_a, 'mla_g_q': mla_g_q, 'mla_g_kv': mla_g_kv, 'mla_w_uq': mla_w_uq,
        'mla_w_uk': mla_w_uk, 'mla_w_uv': mla_w_uv, 'mla_w_o': mla_w_o,
        'kv_shared_norm': kv_shared_norm, 'swa_w_kv': swa_w_kv, 'swa_w_q': swa_w_q,
        'swa_sinks': swa_sinks, 'swa_w_o': swa_w_o,
        'ffn_w_gate': ffn_w_gate, 'ffn_w_up': ffn_w_up, 'ffn_conv_w': ffn_conv_w,
        'ffn_conv_b': ffn_conv_b, 'ffn_w_down': ffn_w_down, 'norm_final': norm_final,
    }
    w = _prep_weights(p)
    pos_p = jnp.arange(x_prompt.shape[1])
    out_p = _trunk(x_prompt, pos_p, p, w, None)
    past = cache_mla_ckv.shape[2]
    pos_s = past + jnp.arange(x_sample.shape[1])
    cache = {'ckv': cache_mla_ckv, 'krope': cache_mla_krope, 'swa_k': cache_swa_k,
             'swa_v': cache_swa_v, 'conv': state_conv}
    out_s = _trunk(x_sample, pos_s, p, w, cache)
    return (out_p[0], out_s[0]) + out_p[1:] + out_s[1:]
```

```python
import functools

import jax
import jax.numpy as jnp
from jax import lax
from jax.experimental import pallas as pl
from jax.experimental.pallas import tpu as pltpu

D_MODEL = 4096
DEPTH = 4
CHUNK = 64
N_A = DEPTH // 2
N_B = DEPTH - N_A
ROPE_THETA = 500000.0
EPS = 1e-6
NEG_INF = -1e30
MLA_HEADS = D_MODEL // 128
Q_LORA = D_MODEL // 4
KV_LORA = 512
QK_NOPE = 128
QK_ROPE = 64
V_DIM = 128
MLA_SCALE = (QK_NOPE + QK_ROPE) ** -0.5
LOG2E = 1.4426950408889634
MLA_Q_SCALE = MLA_SCALE * LOG2E
SWA_HEAD_DIM = 64
SWA_HEADS = D_MODEL // SWA_HEAD_DIM
SWA_KV_HEADS = 8
SWA_GROUP = SWA_HEADS // SWA_KV_HEADS
WINDOW = 128
ROPE_DIM_B = SWA_HEAD_DIM // 4
SWA_SCALE = SWA_HEAD_DIM ** -0.5
D_FF = 256 * ((8 * D_MODEL // 3 + 255) // 256)
CONV_W = 3
FFN_TF = 512
F_PAD = -(-D_FF // FFN_TF) * FFN_TF

LANES = 128
SUBLANES = 8
V7X_VMEM_LIMIT = 56 * 1024 * 1024

MLA_QDIM = 2 * LANES
DENOM_ROWS = 16
SWA_KV_DIM = SWA_KV_HEADS * SWA_HEAD_DIM
HEAD_SLAB = 2 * LANES
BF16 = jnp.bfloat16
F32 = jnp.float32


def _params(n_grid):
    return pltpu.CompilerParams(dimension_semantics=("arbitrary",) * n_grid,
                                vmem_limit_bytes=V7X_VMEM_LIMIT)


def _tile(m, cap):
    for step in (LANES, 16):
        t = (min(cap, m) // step) * step
        while t >= step:
            if m % t == 0:
                return t
            t -= step
    return m


def _rms(x, g):
    return x * lax.rsqrt(jnp.mean(x * x, axis=-1, keepdims=True) + EPS) * g


def _rmsnorm_kernel(x_ref, g_ref, o_ref):
    o_ref[...] = _rms(x_ref[...], g_ref[...]).astype(o_ref.dtype)


def rmsnorm(x, g, out_dtype):
    m, d = x.shape
    tm = _tile(m, 512)
    return pl.pallas_call(
        _rmsnorm_kernel,
        grid=(m // tm,),
        in_specs=[pl.BlockSpec((tm, d), lambda i: (i, 0)),
                  pl.BlockSpec((1, d), lambda i: (0, 0))],
        out_specs=pl.BlockSpec((tm, d), lambda i: (i, 0)),
        out_shape=jax.ShapeDtypeStruct((m, d), out_dtype),
        compiler_params=_params(1),
        name="rmsnorm",
    )(x, g.reshape(1, d))


def _mm_kernel(x_ref, w_ref, *rest, epilogue, n_extra, norm_x):
    extras = rest[:n_extra]
    if norm_x:
        x = _rms(x_ref[...], extras[-1][...]).astype(BF16)
    else:
        x = x_ref[...]
    y = jnp.dot(x, w_ref[...], preferred_element_type=F32)
    epilogue(y, extras, rest[n_extra:])


def matmul(x, w, *, tm, tn, epilogue, extras=(), extra_specs=(), out_shapes, out_specs, name,
           norm_x=False, layer=None):
    m, k = x.shape
    n = w.shape[-1]
    assert not norm_x or tn == n
    w_mode = dict(pipeline_mode=pl.Buffered(1)) if tn == n else {}
    if layer is None:
        w_spec = pl.BlockSpec((k, tn), lambda i, j: (0, j), **w_mode)
    else:
        w_spec = pl.BlockSpec((None, k, tn), lambda i, j: (layer, 0, j), **w_mode)
    return pl.pallas_call(
        functools.partial(_mm_kernel, epilogue=epilogue, n_extra=len(extras), norm_x=norm_x),
        grid=(m // tm, n // tn),
        in_specs=[pl.BlockSpec((tm, k), lambda i, j: (i, 0)), w_spec] + list(extra_specs),
        out_specs=out_specs,
        out_shape=out_shapes,
        compiler_params=_params(2),
        name=name,
    )(x, w, *extras)


def _epi_cast(y, extras, outs):
    outs[0][...] = y.astype(outs[0].dtype)


def _epi_residual(y, extras, outs):
    outs[0][...] = extras[0][...] + y


def _rope_slab(a, cos2, sin2):
    return a * cos2 + pltpu.roll(a, QK_ROPE, axis=1) * sin2


def _epi_mla_in(y, extras, outs):
    gq_ref, gkv_ref, cos_ref, sin_ref = extras[:4]
    cq_ref, ckv_ref, ckvb_ref, kr_ref, krb_ref = outs
    cq_ref[...] = _rms(y[:, :Q_LORA], gq_ref[...]).astype(BF16)
    ckv = _rms(y[:, Q_LORA:Q_LORA + KV_LORA], gkv_ref[...])
    ckv_ref[...] = ckv
    ckvb_ref[...] = ckv.astype(BF16)
    kr = _rope_slab(y[:, Q_LORA + KV_LORA:], cos_ref[...], sin_ref[...])
    kr_ref[...] = kr[:, :QK_ROPE]
    krb_ref[...] = kr.astype(BF16)


def _epi_mla_q(y, extras, outs):
    cos_ref, sin_ref = extras
    cos2 = cos_ref[...] * MLA_Q_SCALE
    sin2 = sin_ref[...] * MLA_Q_SCALE
    for s in range(y.shape[1] // MLA_QDIM):
        lo = s * MLA_QDIM
        outs[0][:, lo:lo + LANES] = (y[:, lo:lo + LANES] * MLA_Q_SCALE).astype(BF16)
        outs[0][:, lo + LANES:lo + MLA_QDIM] = _rope_slab(
            y[:, lo + LANES:lo + MLA_QDIM], cos2, sin2).astype(BF16)


def _rope_b(y, c, s1, s2, scale):
    half = ROPE_DIM_B // 2
    for s in range(y.shape[1] // LANES):
        a = y[:, s * LANES:(s + 1) * LANES]
        r = a * c + pltpu.roll(a, LANES - half, axis=1) * s1 + pltpu.roll(a, half, axis=1) * s2
        yield s, (r * scale if scale != 1.0 else r)


def _epi_swa_q(y, extras, outs):
    c_ref, s1_ref, s2_ref = extras
    for s, r in _rope_b(y, c_ref[...], s1_ref[...], s2_ref[...], SWA_SCALE):
        outs[0][:, s * LANES:(s + 1) * LANES] = r.astype(BF16)


def _epi_shared_kv(y, extras, outs):
    c_ref, s1_ref, s2_ref = extras[:3]
    k_ref, v_ref, kb_ref, vtb_ref = outs
    for s, r in _rope_b(y[:, :SWA_KV_DIM], c_ref[...], s1_ref[...], s2_ref[...], 1.0):
        k_ref[:, s * LANES:(s + 1) * LANES] = r
        kb_ref[:, s * LANES:(s + 1) * LANES] = r.astype(BF16)
    v = y[:, SWA_KV_DIM:]
    v_ref[...] = v
    vtb_ref[...] = v.T.astype(BF16)


def _row_spec(tm, width):
    return pl.BlockSpec((tm, width), lambda i, j: (i, 0))


def _tile_spec(tm, tn):
    return pl.BlockSpec((tm, tn), lambda i, j: (i, j))


def _const_spec(width):
    return pl.BlockSpec((1, width), lambda i, j: (0, 0))


def _ffn_a_kernel(x_ref, wg_ref, wu_ref, cw_ref, cb_ref, prev_ref, act_ref, gl_ref, gbuf, carry,
                  *, seg, nseg, tiles_per_seq, sub, tail_chunks):
    i = pl.program_id(0)
    j = pl.program_id(1)
    if tiles_per_seq > 1:
        @pl.when(jnp.logical_and(i == 0, j == 0))
        def _():
            carry[...] = jnp.zeros(carry.shape, F32)

        gbuf[0:SUBLANES] = jnp.where(i % tiles_per_seq == 0, prev_ref[0], carry[j])
    else:
        for s in range(nseg):
            gbuf[s * SUBLANES:(s + 1) * SUBLANES] = prev_ref[s]
    x = x_ref[...]
    n_sub = act_ref.shape[1] // sub
    row = lax.broadcasted_iota(jnp.int32, (SUBLANES, sub), 0)
    for c in range(n_sub):
        cols = slice(c * sub, (c + 1) * sub)
        w0 = cw_ref[0:1, cols]
        w1 = cw_ref[1:2, cols]
        w2 = cw_ref[2:3, cols]
        b = cb_ref[:, cols]
        n_chunk = tail_chunks if (c == n_sub - 1 and nseg == 1) else 1
        rows = seg // n_chunk
        if n_chunk == 1:
            g = jnp.dot(x, wg_ref[:, cols], preferred_element_type=F32)
            u = jnp.dot(x, wu_ref[:, cols], preferred_element_type=F32)
        for s in range(nseg):
            halo = gbuf[s * SUBLANES:(s + 1) * SUBLANES, cols]
            for k in range(n_chunk):
                r0 = s * seg + k * rows
                if n_chunk == 1:
                    gs, us = g[r0:r0 + rows], u[r0:r0 + rows]
                else:
                    gs = jnp.dot(x[r0:r0 + rows], wg_ref[:, cols], preferred_element_type=F32)
                    us = jnp.dot(x[r0:r0 + rows], wu_ref[:, cols], preferred_element_type=F32)
                r1 = pltpu.roll(gs, 1, axis=0)
                r2 = pltpu.roll(gs, 2, axis=0)
                h1 = jnp.where(row == 0, halo[SUBLANES - 1:SUBLANES], r1[:SUBLANES])
                h2 = jnp.where(row == 0, halo[SUBLANES - 2:SUBLANES - 1],
                               jnp.where(row == 1, halo[SUBLANES - 1:SUBLANES], r2[:SUBLANES]))
                g1 = jnp.concatenate([h1, r1[SUBLANES:]], axis=0)
                g2 = jnp.concatenate([h2, r2[SUBLANES:]], axis=0)
                gc = b + w0 * g2 + w1 * g1 + w2 * gs
                a = gc * jax.nn.sigmoid(gc) * us
                act_ref[r0:r0 + rows, cols] = a.astype(BF16)
                halo = gs[rows - SUBLANES:]
            gl_ref[s, :, cols] = halo
        if tiles_per_seq > 1:
            carry[j, :, cols] = halo


def ffn_a(hn, wg, wu, layer, cw, cb, prev, seq_len, *, tm_cap=1024, tf=FFN_TF, sub=2 * LANES):
    m, d = hn.shape
    f = wg.shape[2]
    tm = _tile(m, tm_cap)
    seg = min(seq_len, tm)
    nseg = tm // seg
    tiles_per_seq = seq_len // seg
    nj = f // tf
    if nseg == 1:
        prev_map = lambda i, j: (i // tiles_per_seq, 0, j)
    else:
        prev_map = lambda i, j: (i, 0, j)
    return pl.pallas_call(
        functools.partial(_ffn_a_kernel, seg=seg, nseg=nseg, tiles_per_seq=tiles_per_seq, sub=sub,
                          tail_chunks=4),
        grid=(m // tm, nj),
        in_specs=[pl.BlockSpec((tm, d), lambda i, j: (i, 0)),
                  pl.BlockSpec((None, d, tf), lambda i, j: (layer, 0, j)),
                  pl.BlockSpec((None, d, tf), lambda i, j: (layer, 0, j)),
                  pl.BlockSpec((CONV_W, tf), lambda i, j: (0, j)),
                  pl.BlockSpec((1, tf), lambda i, j: (0, j)),
                  pl.BlockSpec((nseg, SUBLANES, tf), prev_map)],
        out_specs=[pl.BlockSpec((tm, tf), lambda i, j: (i, j)),
                   pl.BlockSpec((nseg, SUBLANES, tf), lambda i, j: (i, 0, j))],
        out_shape=[jax.ShapeDtypeStruct((m, f), BF16),
                   jax.ShapeDtypeStruct((m // seg, SUBLANES, f), F32)],
        scratch_shapes=[pltpu.VMEM((nseg * SUBLANES, tf), F32),
                        pltpu.VMEM((nj, SUBLANES, tf), F32)],
        compiler_params=_params(2),
        name="ffn_gate_up_conv",
    )(hn, wg, wu, cw, cb.reshape(1, f), prev)


def _mla_attn_kernel(q_ref, kn_ref, kr_ref, vt_ref, o_ref, m_ref, acc_ref, s_ref, p_ref,
                     *, hb, tq, tk, nk, q_off, k_valid, cg, rb):
    qi = pl.program_id(2)
    ki = pl.program_id(3)
    q_lo = q_off + qi * tq
    k_lo = ki * tk

    @pl.when(ki == 0)
    def _():
        m_ref[...] = jnp.full(m_ref.shape, NEG_INF, F32)
        acc_ref[...] = jnp.zeros(acc_ref.shape, F32)

    needed = k_lo // CHUNK <= (q_lo + tq - 1) // CHUNK
    full = jnp.logical_and((k_lo + tk - 1) // CHUNK <= q_lo // CHUNK, k_lo + tk <= k_valid)

    def step(masked):
        kr = kr_ref[0]
        ones = jnp.ones((DENOM_ROWS, tk), BF16)
        for h in range(hb):
            buf = h % s_ref.shape[0]
            q = q_ref[0, :, h * MLA_QDIM:(h + 1) * MLA_QDIM]
            k = jnp.concatenate([kn_ref[0, :, h * QK_NOPE:(h + 1) * QK_NOPE], kr], axis=1)
            s_ref[buf] = lax.dot_general(k, q, (((1,), (1,)), ((), ())), preferred_element_type=F32)
            alphas = []
            for c in range(tq // cg):
                cols = slice(c * cg, (c + 1) * cg)
                if masked:
                    qc = (q_lo + c * cg + lax.broadcasted_iota(jnp.int32, (1, cg), 1)) // CHUNK

                def scores(r):
                    s = s_ref[buf, r * rb:(r + 1) * rb, cols]
                    if masked:
                        kp = k_lo + r * rb + lax.broadcasted_iota(jnp.int32, (rb, 1), 0)
                        s = jnp.where(jnp.logical_and(qc >= kp // CHUNK, kp < k_valid), s, NEG_INF)
                    return s

                part = None
                for r in range(tk // rb):
                    blk = jnp.max(scores(r).reshape(rb // SUBLANES, SUBLANES, cg), axis=0)
                    part = blk if part is None else jnp.maximum(part, blk)
                m_prev = m_ref[h, :, cols]
                m_new = jnp.maximum(m_prev, jnp.max(part, axis=0, keepdims=True))
                alphas.append(jnp.exp2(m_prev - m_new))
                m_ref[h, :, cols] = m_new
                for r in range(tk // rb):
                    p_ref[buf, r * rb:(r + 1) * rb, cols] = jnp.exp2(scores(r) - m_new).astype(BF16)
            vt = jnp.concatenate([vt_ref[0, h * V_DIM:(h + 1) * V_DIM, :], ones], axis=0)
            pv = jnp.dot(vt, p_ref[buf], preferred_element_type=F32)
            acc_ref[h] = jnp.concatenate(alphas, axis=1) * acc_ref[h] + pv

    @pl.when(jnp.logical_and(needed, full))
    def _():
        step(False)

    @pl.when(jnp.logical_and(needed, jnp.logical_not(full)))
    def _():
        step(True)

    @pl.when(ki == nk - 1)
    def _():
        for h in range(hb):
            o_t = acc_ref[h, :V_DIM, :] / acc_ref[h, V_DIM:V_DIM + 1, :]
            o_ref[0, :, h * V_DIM:(h + 1) * V_DIM] = o_t.T.astype(BF16)


def mla_attention(q, kn, kr, vt, *, q_off, k_valid, tq, tk, hb=8, n_buf=2):
    bsz, sq, _ = q.shape
    sk = kn.shape[1]
    nq, nk = sq // tq, sk // tk
    n_hg = MLA_HEADS // hb

    def k_idx(qi, ki):
        last = ((q_off + (qi + 1) * tq - 1) // CHUNK * CHUNK + CHUNK - 1) // tk
        return jnp.minimum(ki, jnp.minimum(last, nk - 1))

    return pl.pallas_call(
        functools.partial(_mla_attn_kernel, hb=hb, tq=tq, tk=tk, nk=nk, q_off=q_off,
                          k_valid=k_valid, cg=min(tq, 2 * LANES), rb=LANES),
        grid=(bsz, n_hg, nq, nk),
        in_specs=[pl.BlockSpec((1, tq, hb * MLA_QDIM), lambda b, g, qi, ki: (b, qi, g)),
                  pl.BlockSpec((1, tk, hb * QK_NOPE), lambda b, g, qi, ki: (b, k_idx(qi, ki), g)),
                  pl.BlockSpec((1, tk, LANES), lambda b, g, qi, ki: (b, k_idx(qi, ki), 0)),
                  pl.BlockSpec((1, hb * V_DIM, tk), lambda b, g, qi, ki: (b, g, k_idx(qi, ki)))],
        out_specs=pl.BlockSpec((1, tq, hb * V_DIM), lambda b, g, qi, ki: (b, qi, g)),
        out_shape=jax.ShapeDtypeStruct((bsz, sq, MLA_HEADS * V_DIM), BF16),
        scratch_shapes=[pltpu.VMEM((hb, 1, tq), F32),
                        pltpu.VMEM((hb, V_DIM + DENOM_ROWS, tq), F32),
                        pltpu.VMEM((n_buf, tk, tq), F32),
                        pltpu.VMEM((n_buf, tk, tq), BF16)],
        compiler_params=_params(4),
        name="mla_flash_attention",
    )(q, kn, kr, vt)


def _nt_kernel(w_ref, x_ref, o_ref):
    o_ref[0] = lax.dot_general(w_ref[...], x_ref[0], (((1,), (1,)), ((), ())),
                               preferred_element_type=F32).astype(o_ref.dtype)


def matmul_nt(wt, layer, x, *, tn, ts):
    _, n, k = wt.shape
    bsz, s, _ = x.shape
    return pl.pallas_call(
        _nt_kernel,
        grid=(bsz, s // ts, n // tn),
        in_specs=[pl.BlockSpec((None, tn, k), lambda b, i, j: (layer, j, 0)),
                  pl.BlockSpec((1, ts, k), lambda b, i, j: (b, i, 0))],
        out_specs=pl.BlockSpec((1, tn, ts), lambda b, i, j: (b, j, i)),
        out_shape=jax.ShapeDtypeStruct((bsz, n, s), BF16),
        compiler_params=_params(3),
        name="mla_v_up_proj_t",
    )(wt, x)


def _swa_attn_kernel(sink_ref, q_ref, ka_ref, kb_ref, vta_ref, vtb_ref, o_ref, ot_ref, *, tq, k_valid):
    t = pl.program_id(1)
    k = jnp.concatenate([ka_ref[0], kb_ref[0]], axis=0)
    vt = jnp.concatenate([vta_ref[0], vtb_ref[0]], axis=1)
    tk = k.shape[0]
    kp = lax.broadcasted_iota(jnp.int32, (tk, 1), 0)
    if k_valid is None:
        kp = kp + (t - 1) * tq
        kc = kp // CHUNK
        qc = (t * tq + lax.broadcasted_iota(jnp.int32, (1, tq), 1)) // CHUNK
        valid = jnp.logical_and(kp >= 0, jnp.logical_and(kc >= qc - WINDOW // CHUNK, kc <= qc))
    else:
        valid = jnp.broadcast_to(kp < k_valid, (tk, tq))
    valid = jnp.concatenate([valid] * SWA_GROUP, axis=1)
    lane_head = lax.broadcasted_iota(jnp.int32, (1, HEAD_SLAB), 1) // SWA_HEAD_DIM
    heads_per_slab = HEAD_SLAB // SWA_HEAD_DIM
    for slab in range(SWA_KV_DIM // HEAD_SLAB):
        k_slab = k[:, slab * HEAD_SLAB:(slab + 1) * HEAD_SLAB]
        qs = jnp.concatenate(
            [q_ref[0, :, g * SWA_KV_DIM + slab * HEAD_SLAB:g * SWA_KV_DIM + (slab + 1) * HEAD_SLAB]
             for g in range(SWA_GROUP)], axis=0)
        for hh in range(heads_per_slab):
            kvh = slab * heads_per_slab + hh
            km = k_slab * (lane_head == hh).astype(BF16)
            vth = vt[kvh * SWA_HEAD_DIM:(kvh + 1) * SWA_HEAD_DIM, :]
            sb = jnp.concatenate([jnp.full((1, tq), sink_ref[kvh, g], F32) for g in range(SWA_GROUP)],
                                 axis=1)
            s = lax.dot_general(km, qs, (((1,), (1,)), ((), ())), preferred_element_type=F32)
            s = jnp.where(valid, s, NEG_INF)
            m = jnp.maximum(jnp.max(s, axis=0, keepdims=True), sb)
            e = jnp.exp(s - m)
            inv = 1.0 / (jnp.sum(e, axis=0, keepdims=True) + jnp.exp(sb - m))
            o_t = jnp.dot(vth, (e * inv).astype(BF16), preferred_element_type=F32)
            for g in range(SWA_GROUP):
                row = g * SWA_KV_DIM + kvh * SWA_HEAD_DIM
                ot_ref[row:row + SWA_HEAD_DIM, :] = o_t[:, g * tq:(g + 1) * tq]
    for c in range(D_MODEL // LANES):
        o_ref[0, :, c * LANES:(c + 1) * LANES] = ot_ref[c * LANES:(c + 1) * LANES, :].T.astype(BF16)


def swa_attention(q, k, vt, sink, *, k_valid=None):
    bsz, sq, _ = q.shape
    tq = WINDOW
    if k_valid is None:
        a_idx = lambda t: jnp.maximum(t - 1, 0)
        b_idx = lambda t: t
    else:
        assert sq == tq and k.shape[1] == 2 * WINDOW
        a_idx = lambda t: 0
        b_idx = lambda t: 1
    return pl.pallas_call(
        functools.partial(_swa_attn_kernel, tq=tq, k_valid=k_valid),
        grid=(bsz, sq // tq),
        in_specs=[pl.BlockSpec(memory_space=pltpu.SMEM),
                  pl.BlockSpec((1, tq, D_MODEL), lambda b, t: (b, t, 0)),
                  pl.BlockSpec((1, WINDOW, SWA_KV_DIM), lambda b, t: (b, a_idx(t), 0)),
                  pl.BlockSpec((1, WINDOW, SWA_KV_DIM), lambda b, t: (b, b_idx(t), 0)),
                  pl.BlockSpec((1, SWA_KV_DIM, WINDOW), lambda b, t: (b, 0, a_idx(t))),
                  pl.BlockSpec((1, SWA_KV_DIM, WINDOW), lambda b, t: (b, 0, b_idx(t)))],
        out_specs=pl.BlockSpec((1, tq, D_MODEL), lambda b, t: (b, t, 0)),
        out_shape=jax.ShapeDtypeStruct((bsz, sq, D_MODEL), BF16),
        scratch_shapes=[pltpu.VMEM((D_MODEL, tq), F32)],
        compiler_params=_params(2),
        name="swa_sink_attention",
    )(sink, q, k, k, vt, vt)


def _cast_pad_cols_kernel(x_ref, o_ref):
    n = x_ref.shape[-1]
    o_ref[0, :, :n] = x_ref[0].astype(BF16)
    o_ref[0, :, n:] = jnp.zeros((o_ref.shape[1], o_ref.shape[2] - n), BF16)


def cast_pad_cols(x, n_pad, tr=256):
    nl, r, c = x.shape
    return pl.pallas_call(
        _cast_pad_cols_kernel,
        grid=(nl, r // tr),
        in_specs=[pl.BlockSpec((1, tr, c), lambda l, i: (l, i, 0))],
        out_specs=pl.BlockSpec((1, tr, n_pad), lambda l, i: (l, i, 0)),
        out_shape=jax.ShapeDtypeStruct((nl, r, n_pad), BF16),
        compiler_params=_params(2),
        name="cast_pad_cols",
    )(x)


def _cast_pad_rows_kernel(x_ref, o_ref, *, n_blocks):
    i = pl.program_id(1)

    @pl.when(i < n_blocks)
    def _():
        o_ref[0] = x_ref[0].astype(BF16)

    @pl.when(i >= n_blocks)
    def _():
        o_ref[0] = jnp.zeros(o_ref.shape[1:], BF16)


def cast_pad_rows(x, r_pad, tr=256):
    nl, r, c = x.shape
    n_blocks = r // tr
    return pl.pallas_call(
        functools.partial(_cast_pad_rows_kernel, n_blocks=n_blocks),
        grid=(nl, r_pad // tr),
        in_specs=[pl.BlockSpec((1, tr, c), lambda l, i: (l, jnp.minimum(i, n_blocks - 1), 0))],
        out_specs=pl.BlockSpec((1, tr, c), lambda l, i: (l, i, 0)),
        out_shape=jax.ShapeDtypeStruct((nl, r_pad, c), BF16),
        compiler_params=_params(2),
        name="cast_pad_rows",
    )(x)


def _rope_slab_cols(w_rope):
    half = QK_ROPE // 2
    x1, x2 = w_rope[..., :half], w_rope[..., half:]
    return jnp.concatenate([x1, x2, x2, x1], axis=-1)


def _mla_tables(pos):
    half = QK_ROPE // 2
    inv = jnp.power(jnp.float32(ROPE_THETA), -jnp.arange(half, dtype=F32) * (2.0 / QK_ROPE))
    ang = pos.astype(F32)[:, None] * inv[None, :]
    c, s = jnp.cos(ang), jnp.sin(ang)
    z = jnp.zeros_like(c)
    return jnp.concatenate([c, c, z, z], axis=1), jnp.concatenate([-s, s, z, z], axis=1)


def _swa_tables(pos):
    half = ROPE_DIM_B // 2
    inv = jnp.power(jnp.float32(ROPE_THETA), -jnp.arange(half, dtype=F32) * (2.0 / ROPE_DIM_B))
    ang = pos.astype(F32)[:, None] * inv[None, :]
    c, s = jnp.cos(ang), jnp.sin(ang)
    n = pos.shape[0]
    rest = SWA_HEAD_DIM - ROPE_DIM_B
    c64 = jnp.concatenate([c, c, jnp.ones((n, rest), F32)], axis=1)
    s1 = jnp.concatenate([-s, jnp.zeros((n, SWA_HEAD_DIM - half), F32)], axis=1)
    s2 = jnp.concatenate([jnp.zeros((n, half), F32), s, jnp.zeros((n, rest), F32)], axis=1)
    rep = LANES // SWA_HEAD_DIM
    return jnp.tile(c64, (1, rep)), jnp.tile(s1, (1, rep)), jnp.tile(s2, (1, rep))


def _prep_weights(p):
    w = {}
    wa = p['mla_w_a'].astype(BF16)
    w['mla_w_a'] = jnp.concatenate(
        [wa[..., :Q_LORA + KV_LORA], _rope_slab_cols(wa[..., Q_LORA + KV_LORA:])], axis=-1)
    wuq = p['mla_w_uq'].astype(BF16).reshape(N_A, Q_LORA, MLA_HEADS, QK_NOPE + QK_ROPE)
    w['mla_w_uq'] = jnp.concatenate(
        [wuq[..., :QK_NOPE], _rope_slab_cols(wuq[..., QK_NOPE:])], axis=-1
    ).reshape(N_A, Q_LORA, MLA_HEADS * MLA_QDIM)
    w['mla_w_uk'] = p['mla_w_uk'].reshape(N_A, KV_LORA, MLA_HEADS * QK_NOPE).astype(BF16)
    w['mla_w_uvt'] = p['mla_w_uv'].reshape(N_A, KV_LORA, MLA_HEADS * V_DIM).transpose(0, 2, 1).astype(BF16)
    w['mla_w_o'] = p['mla_w_o'].astype(BF16)
    w['swa_w_kv'] = p['swa_w_kv'].astype(BF16)
    wq = p['swa_w_q'].reshape(N_B, D_MODEL, SWA_KV_HEADS, SWA_GROUP, SWA_HEAD_DIM)
    w['swa_w_q'] = wq.transpose(0, 1, 3, 2, 4).reshape(N_B, D_MODEL, D_MODEL).astype(BF16)
    wo = p['swa_w_o'].reshape(N_B, SWA_KV_HEADS, SWA_GROUP, SWA_HEAD_DIM, D_MODEL)
    w['swa_w_o'] = wo.transpose(0, 2, 1, 3, 4).reshape(N_B, D_MODEL, D_MODEL).astype(BF16)
    w['swa_sinks'] = p['swa_sinks'].reshape(N_B, SWA_KV_HEADS, SWA_GROUP)
    fpad = F_PAD - D_FF
    w['ffn_w_gate'] = cast_pad_cols(p['ffn_w_gate'], F_PAD)
    w['ffn_w_up'] = cast_pad_cols(p['ffn_w_up'], F_PAD)
    w['ffn_w_down'] = cast_pad_rows(p['ffn_w_down'], F_PAD)
    w['ffn_conv_w'] = jnp.pad(p['ffn_conv_w'], ((0, 0), (0, 0), (0, fpad)))
    w['ffn_conv_b'] = jnp.pad(p['ffn_conv_b'], ((0, 0), (0, fpad)))
    return w


def _trunk(x, pos, p, w, cache):
    bsz, seq, _ = x.shape
    m = bsz * seq
    h = x.reshape(m, D_MODEL)
    tm = _tile(m, 512)
    tm_l = _tile(m, 1024)
    past = 0 if cache is None else cache['ckv'].shape[2]

    mla_cos, mla_sin = (jnp.tile(t, (bsz, 1)) for t in _mla_tables(pos))
    swa_c, swa_s1, swa_s2 = (jnp.tile(t, (bsz, 1)) for t in _swa_tables(pos))
    tab = _row_spec(tm, LANES)
    tab_l = _row_spec(tm_l, LANES)

    def out_proj(o, w_o, layer, h_res):
        return matmul(o, w_o, layer=layer, tm=tm_l, tn=1024, epilogue=_epi_residual,
                      extras=(h_res,), extra_specs=(_tile_spec(tm_l, 1024),),
                      out_shapes=jax.ShapeDtypeStruct((m, D_MODEL), F32),
                      out_specs=_tile_spec(tm_l, 1024), name="out_proj_residual")

    ckv_rows, krope_rows, conv_rows = [], [], []
    ks = vs = ks_b = vs_t = None
    for l in range(DEPTH):
        if l < N_A:
            n_a = w['mla_w_a'].shape[2]
            cq, ckv, ckv_b, krope, kr_b = matmul(
                h, w['mla_w_a'], layer=l, tm=tm, tn=n_a, epilogue=_epi_mla_in, norm_x=True,
                extras=(p['mla_g_q'][l].reshape(1, Q_LORA), p['mla_g_kv'][l].reshape(1, KV_LORA),
                        mla_cos, mla_sin, p['norm_attn'][l].reshape(1, D_MODEL)),
                extra_specs=(_const_spec(Q_LORA), _const_spec(KV_LORA), tab, tab,
                             _const_spec(D_MODEL)),
                out_shapes=[jax.ShapeDtypeStruct((m, Q_LORA), BF16),
                            jax.ShapeDtypeStruct((m, KV_LORA), F32),
                            jax.ShapeDtypeStruct((m, KV_LORA), BF16),
                            jax.ShapeDtypeStruct((m, QK_ROPE), F32),
                            jax.ShapeDtypeStruct((m, LANES), BF16)],
                out_specs=[_row_spec(tm, Q_LORA), _row_spec(tm, KV_LORA), _row_spec(tm, KV_LORA),
                           _row_spec(tm, QK_ROPE), _row_spec(tm, LANES)],
                name="mla_in_proj")
            ckv_rows.append(ckv.reshape(bsz, seq, KV_LORA))
            krope_rows.append(krope.reshape(bsz, seq, QK_ROPE))
            tn_q = 8 * MLA_QDIM
            q = matmul(cq, w['mla_w_uq'], layer=l, tm=tm_l, tn=tn_q, epilogue=_epi_mla_q,
                       extras=(mla_cos, mla_sin), extra_specs=(tab_l, tab_l),
                       out_shapes=jax.ShapeDtypeStruct((m, MLA_HEADS * MLA_QDIM), BF16),
                       out_specs=_tile_spec(tm_l, tn_q), name="mla_q_proj")
            if cache is None:
                ckv_all = ckv_b.reshape(bsz, seq, KV_LORA)
                kr_all = kr_b.reshape(bsz, seq, LANES)
                k_valid = seq
                tq = tk = min(seq, 1024)
            else:
                k_valid = past + seq
                sk = -(-k_valid // LANES) * LANES
                ckv_all = jnp.concatenate(
                    [cache['ckv'][l].astype(BF16), ckv_b.reshape(bsz, seq, KV_LORA),
                     jnp.zeros((bsz, sk - k_valid, KV_LORA), BF16)], axis=1)
                kr_cache = jnp.pad(cache['krope'][l], ((0, 0), (0, 0), (0, LANES - QK_ROPE)))
                kr_all = jnp.concatenate(
                    [kr_cache.astype(BF16), kr_b.reshape(bsz, seq, LANES),
                     jnp.zeros((bsz, sk - k_valid, LANES), BF16)], axis=1)
                tq, tk = LANES, sk
            sk = ckv_all.shape[1]
            mk = bsz * sk
            tmk = _tile(mk, 1024)
            kn = matmul(ckv_all.reshape(mk, KV_LORA), w['mla_w_uk'], layer=l, tm=tmk, tn=2048,
                        epilogue=_epi_cast,
                        out_shapes=jax.ShapeDtypeStruct((mk, MLA_HEADS * QK_NOPE), BF16),
                        out_specs=_tile_spec(tmk, 2048), name="mla_k_up_proj")
            if sk % 512 == 0:
                vt = matmul_nt(w['mla_w_uvt'], l, ckv_all, tn=MLA_HEADS * V_DIM, ts=512)
            else:
                vt = matmul_nt(w['mla_w_uvt'], l, ckv_all, tn=1024, ts=sk)
            q3 = q.reshape(bsz, seq, MLA_HEADS * MLA_QDIM)
            if seq < tq:
                q3 = jnp.pad(q3, ((0, 0), (0, tq - seq), (0, 0)))
            o = mla_attention(q3, kn.reshape(bsz, sk, MLA_HEADS * QK_NOPE), kr_all, vt,
                              q_off=past, k_valid=k_valid, tq=tq, tk=tk)
            h = out_proj(o[:, :seq].reshape(m, D_MODEL), w['mla_w_o'], l, h)
        else:
            i = l - N_A
            if l == N_A:
                ks, vs, ks_b, vs_t = matmul(
                    h, w['swa_w_kv'], tm=tm, tn=2 * SWA_KV_DIM, epilogue=_epi_shared_kv,
                    norm_x=True,
                    extras=(swa_c, swa_s1, swa_s2, p['kv_shared_norm'].reshape(1, D_MODEL)),
                    extra_specs=(tab, tab, tab, _const_spec(D_MODEL)),
                    out_shapes=[jax.ShapeDtypeStruct((m, SWA_KV_DIM), F32)] * 2
                    + [jax.ShapeDtypeStruct((m, SWA_KV_DIM), BF16),
                       jax.ShapeDtypeStruct((SWA_KV_DIM, m), BF16)],
                    out_specs=[_row_spec(tm, SWA_KV_DIM)] * 3
                    + [pl.BlockSpec((SWA_KV_DIM, tm), lambda i, j: (0, i))],
                    name="swa_shared_kv_proj")
            hn = rmsnorm(h, p['norm_attn'][l], BF16)
            q = matmul(hn, w['swa_w_q'], layer=i, tm=tm_l, tn=1024, epilogue=_epi_swa_q,
                       extras=(swa_c, swa_s1, swa_s2), extra_specs=(tab_l, tab_l, tab_l),
                       out_shapes=jax.ShapeDtypeStruct((m, D_MODEL), BF16),
                       out_specs=_tile_spec(tm_l, 1024), name="swa_q_proj")
            q3 = q.reshape(bsz, seq, D_MODEL)
            k3 = ks_b.reshape(bsz, seq, SWA_KV_DIM)
            vt3 = vs_t.reshape(SWA_KV_DIM, bsz, seq).transpose(1, 0, 2)
            sink = w['swa_sinks'][i]
            if cache is None:
                o = swa_attention(q3, k3, vt3, sink)
            else:
                win = cache['swa_k'].shape[1]
                npad = 2 * WINDOW - win - seq
                k_all = jnp.concatenate(
                    [cache['swa_k'].reshape(bsz, win, SWA_KV_DIM).astype(BF16), k3,
                     jnp.zeros((bsz, npad, SWA_KV_DIM), BF16)], axis=1)
                vt_cache = cache['swa_v'].reshape(bsz, win, SWA_KV_DIM).transpose(0, 2, 1)
                vt_all = jnp.concatenate(
                    [vt_cache.astype(BF16), vt3, jnp.zeros((bsz, SWA_KV_DIM, npad), BF16)], axis=2)
                q3 = jnp.pad(q3, ((0, 0), (0, WINDOW - seq), (0, 0)))
                o = swa_attention(q3, k_all, vt_all, sink, k_valid=win + seq)[:, :seq]
            h = out_proj(o.reshape(m, D_MODEL), w['swa_w_o'], i, h)

        hn = rmsnorm(h, p['norm_ffn'][l], BF16)
        if cache is None:
            prev = jnp.zeros((bsz, SUBLANES, F_PAD), F32)
        else:
            prev = jnp.pad(cache['conv'][l],
                           ((0, 0), (SUBLANES - (CONV_W - 1), 0), (0, F_PAD - D_FF)))
        act, gl = ffn_a(hn, w['ffn_w_gate'], w['ffn_w_up'], l, w['ffn_conv_w'][l],
                        w['ffn_conv_b'][l], prev, seq)
        gl = gl.reshape(bsz, -1, SUBLANES, F_PAD)
        conv_rows.append(gl[:, -1, SUBLANES - (CONV_W - 1):, :D_FF])
        h = matmul(act, w['ffn_w_down'], layer=l, tm=tm, tn=512, epilogue=_epi_residual,
                   extras=(h,), extra_specs=(_tile_spec(tm, 512),),
                   out_shapes=jax.ShapeDtypeStruct((m, D_MODEL), F32),
                   out_specs=_tile_spec(tm, 512), name="ffn_down_residual")

    y = rmsnorm(h, p['norm_final'], F32).reshape(bsz, seq, D_MODEL)
    win = min(WINDOW, seq) if cache is None else seq
    ks4 = ks.reshape(bsz, seq, SWA_KV_DIM)[:, seq - win:].reshape(bsz, win, SWA_KV_HEADS, SWA_HEAD_DIM)
    vs4 = vs.reshape(bsz, seq, SWA_KV_DIM)[:, seq - win:].reshape(bsz, win, SWA_KV_HEADS, SWA_HEAD_DIM)
    return y, jnp.stack(ckv_rows), jnp.stack(krope_rows), ks4, vs4, jnp.stack(conv_rows)


def kernel(x_prompt, x_sample, cache_mla_ckv, cache_mla_krope, cache_swa_k, cache_swa_v, state_conv,
           norm_attn, norm_ffn, mla_w_a, mla_g_q, mla_g_kv, mla_w_uq, mla_w_uk, mla_w_uv, mla_w_o,
           kv_shared_norm, swa_w_kv, swa_w_q, swa_sinks, swa_w_o,
           ffn_w_gate, ffn_w_up, ffn_conv_w, ffn_conv_b, ffn_w_down, norm_final):
    p = {
        'norm_attn': norm_attn, 'norm_ffn': norm_ffn,
        'mla_w_a': mla_w_a, 'mla_g_q': mla_g_q, 'mla_g_kv': mla_g_kv, 'mla_w_uq': mla_w_uq,
        'mla_w_uk': mla_w_uk, 'mla_w_uv': mla_w_uv, 'mla_w_o': mla_w_o,
        'kv_shared_norm': kv_shared_norm, 'swa_w_kv': swa_w_kv, 'swa_w_q': swa_w_q,
        'swa_sinks': swa_sinks, 'swa_w_o': swa_w_o,
        'ffn_w_gate': ffn_w_gate, 'ffn_w_up': ffn_w_up, 'ffn_conv_w': ffn_conv_w,
        'ffn_conv_b': ffn_conv_b, 'ffn_w_down': ffn_w_down, 'norm_final': norm_final,
    }
    w = _prep_weights(p)
    pos_p = jnp.arange(x_prompt.shape[1])
    out_p = _trunk(x_prompt, pos_p, p, w, None)
    past = cache_mla_ckv.shape[2]
    pos_s = past + jnp.arange(x_sample.shape[1])
    cache = {'ckv': cache_mla_ckv, 'krope': cache_mla_krope, 'swa_k': cache_swa_k,
             'swa_v': cache_swa_v, 'conv': state_conv}
    out_s = _trunk(x_sample, pos_s, p, w, cache)
    return (out_p[0], out_s[0]) + out_p[1:] + out_s[1:]
```

```python
import functools

import jax
import jax.numpy as jnp
from jax import lax
from jax.experimental import pallas as pl
from jax.experimental.pallas import tpu as pltpu

D_MODEL = 4096
DEPTH = 4
CHUNK = 64
N_A = DEPTH // 2
N_B = DEPTH - N_A
ROPE_THETA = 500000.0
EPS = 1e-6
NEG_INF = -1e30
MLA_HEADS = D_MODEL // 128
Q_LORA = D_MODEL // 4
KV_LORA = 512
QK_NOPE = 128
QK_ROPE = 64
V_DIM = 128
MLA_SCALE = (QK_NOPE + QK_ROPE) ** -0.5
LOG2E = 1.4426950408889634
MLA_Q_SCALE = MLA_SCALE * LOG2E
SWA_HEAD_DIM = 64
SWA_HEADS = D_MODEL // SWA_HEAD_DIM
SWA_KV_HEADS = 8
SWA_GROUP = SWA_HEADS // SWA_KV_HEADS
WINDOW = 128
ROPE_DIM_B = SWA_HEAD_DIM // 4
SWA_SCALE = SWA_HEAD_DIM ** -0.5
D_FF = 256 * ((8 * D_MODEL // 3 + 255) // 256)
CONV_W = 3
FFN_TF = 512
F_PAD = -(-D_FF // FFN_TF) * FFN_TF

LANES = 128
SUBLANES = 8
V7X_VMEM_LIMIT = 56 * 1024 * 1024

MLA_QDIM = 2 * LANES
DENOM_ROWS = 16
SWA_KV_DIM = SWA_KV_HEADS * SWA_HEAD_DIM
HEAD_SLAB = 2 * LANES
BF16 = jnp.bfloat16
F32 = jnp.float32


def _params(n_grid):
    return pltpu.CompilerParams(dimension_semantics=("arbitrary",) * n_grid,
                                vmem_limit_bytes=V7X_VMEM_LIMIT)


def _tile(m, cap):
    for step in (LANES, 16):
        t = (min(cap, m) // step) * step
        while t >= step:
            if m % t == 0:
                return t
            t -= step
    return m


def _rms(x, g):
    return x * lax.rsqrt(jnp.mean(x * x, axis=-1, keepdims=True) + EPS) * g


def _rmsnorm_kernel(x_ref, g_ref, o_ref):
    o_ref[...] = _rms(x_ref[...], g_ref[...]).astype(o_ref.dtype)


def rmsnorm(x, g, out_dtype):
    m, d = x.shape
    tm = _tile(m, 512)
    return pl.pallas_call(
        _rmsnorm_kernel,
        grid=(m // tm,),
        in_specs=[pl.BlockSpec((tm, d), lambda i: (i, 0)),
                  pl.BlockSpec((1, d), lambda i: (0, 0))],
        out_specs=pl.BlockSpec((tm, d), lambda i: (i, 0)),
        out_shape=jax.ShapeDtypeStruct((m, d), out_dtype),
        compiler_params=_params(1),
        name="rmsnorm",
    )(x, g.reshape(1, d))


def _mm_kernel(x_ref, w_ref, *rest, epilogue, n_extra, norm_x):
    extras = rest[:n_extra]
    if norm_x:
        x = _rms(x_ref[...], extras[-1][...]).astype(BF16)
    else:
        x = x_ref[...]
    y = jnp.dot(x, w_ref[...], preferred_element_type=F32)
    epilogue(y, extras, rest[n_extra:])


def matmul(x, w, *, tm, tn, epilogue, extras=(), extra_specs=(), out_shapes, out_specs, name,
           norm_x=False, layer=None):
    m, k = x.shape
    n = w.shape[-1]
    assert not norm_x or tn == n
    w_mode = dict(pipeline_mode=pl.Buffered(1)) if tn == n else {}
    if layer is None:
        w_spec = pl.BlockSpec((k, tn), lambda i, j: (0, j), **w_mode)
    else:
        w_spec = pl.BlockSpec((None, k, tn), lambda i, j: (layer, 0, j), **w_mode)
    return pl.pallas_call(
        functools.partial(_mm_kernel, epilogue=epilogue, n_extra=len(extras), norm_x=norm_x),
        grid=(m // tm, n // tn),
        in_specs=[pl.BlockSpec((tm, k), lambda i, j: (i, 0)), w_spec] + list(extra_specs),
        out_specs=out_specs,
        out_shape=out_shapes,
        compiler_params=_params(2),
        name=name,
    )(x, w, *extras)


def _epi_cast(y, extras, outs):
    outs[0][...] = y.astype(outs[0].dtype)


def _epi_residual(y, extras, outs):
    outs[0][...] = extras[0][...] + y


def _epi_residual_stats(y, extras, outs):
    h_ref, hb_ref, ssq_ref = outs
    h = extras[0][...] + y
    h_ref[...] = h
    hb_ref[...] = h.astype(BF16)
    sq = h * h
    part = sq[:, :LANES]
    for c in range(1, sq.shape[1] // LANES):
        part = part + sq[:, c * LANES:(c + 1) * LANES]
    j = pl.program_id(1)

    @pl.when(j == 0)
    def _():
        ssq_ref[...] = part

    @pl.when(j > 0)
    def _():
        ssq_ref[...] += part


def _rope_slab(a, cos2, sin2):
    return a * cos2 + pltpu.roll(a, QK_ROPE, axis=1) * sin2


def _epi_mla_in(y, extras, outs):
    gq_ref, gkv_ref, cos_ref, sin_ref = extras[:4]
    cq_ref, ckv_ref, ckvb_ref, kr_ref, krb_ref = outs
    cq_ref[...] = _rms(y[:, :Q_LORA], gq_ref[...]).astype(BF16)
    ckv = _rms(y[:, Q_LORA:Q_LORA + KV_LORA], gkv_ref[...])
    ckv_ref[...] = ckv
    ckvb_ref[...] = ckv.astype(BF16)
    kr = _rope_slab(y[:, Q_LORA + KV_LORA:], cos_ref[...], sin_ref[...])
    kr_ref[...] = kr[:, :QK_ROPE]
    krb_ref[...] = kr.astype(BF16)


def _epi_mla_q(y, extras, outs):
    cos_ref, sin_ref = extras
    cos2 = cos_ref[...] * MLA_Q_SCALE
    sin2 = sin_ref[...] * MLA_Q_SCALE
    for s in range(y.shape[1] // MLA_QDIM):
        lo = s * MLA_QDIM
        outs[0][:, lo:lo + LANES] = (y[:, lo:lo + LANES] * MLA_Q_SCALE).astype(BF16)
        outs[0][:, lo + LANES:lo + MLA_QDIM] = _rope_slab(
            y[:, lo + LANES:lo + MLA_QDIM], cos2, sin2).astype(BF16)


def _rope_b(y, c, s1, s2, scale):
    half = ROPE_DIM_B // 2
    for s in range(y.shape[1] // LANES):
        a = y[:, s * LANES:(s + 1) * LANES]
        r = a * c + pltpu.roll(a, LANES - half, axis=1) * s1 + pltpu.roll(a, half, axis=1) * s2
        yield s, (r * scale if scale != 1.0 else r)


def _epi_swa_q(y, extras, outs):
    c_ref, s1_ref, s2_ref = extras
    for s, r in _rope_b(y, c_ref[...], s1_ref[...], s2_ref[...], SWA_SCALE):
        outs[0][:, s * LANES:(s + 1) * LANES] = r.astype(BF16)


def _epi_shared_kv(y, extras, outs):
    c_ref, s1_ref, s2_ref = extras[:3]
    k_ref, v_ref, kb_ref, vtb_ref = outs
    for s, r in _rope_b(y[:, :SWA_KV_DIM], c_ref[...], s1_ref[...], s2_ref[...], 1.0):
        k_ref[:, s * LANES:(s + 1) * LANES] = r
        kb_ref[:, s * LANES:(s + 1) * LANES] = r.astype(BF16)
    v = y[:, SWA_KV_DIM:]
    v_ref[...] = v
    vtb_ref[...] = v.T.astype(BF16)


def _row_spec(tm, width):
    return pl.BlockSpec((tm, width), lambda i, j: (i, 0))


def _tile_spec(tm, tn):
    return pl.BlockSpec((tm, tn), lambda i, j: (i, j))


def _const_spec(width):
    return pl.BlockSpec((1, width), lambda i, j: (0, 0))


def _ffn_a_kernel(x_ref, ssq_ref, wg_ref, wu_ref, cw_ref, cb_ref, prev_ref, act_ref, gl_ref, gbuf,
                  carry, xn_ref, *, seg, nseg, tiles_per_seq, sub, tail_chunks):
    i = pl.program_id(0)
    j = pl.program_id(1)
    if tiles_per_seq > 1:
        @pl.when(jnp.logical_and(i == 0, j == 0))
        def _():
            carry[...] = jnp.zeros(carry.shape, F32)

        gbuf[0:SUBLANES] = jnp.where(i % tiles_per_seq == 0, prev_ref[0], carry[j])
    else:
        for s in range(nseg):
            gbuf[s * SUBLANES:(s + 1) * SUBLANES] = prev_ref[s]
    @pl.when(j == 0)
    def _():
        tm, d = x_ref.shape
        for r in range(tm // LANES):
            rows = slice(r * LANES, (r + 1) * LANES)
            rstd = lax.rsqrt(jnp.sum(ssq_ref[rows, :], axis=-1, keepdims=True) * (1.0 / d) + EPS)
            xn_ref[rows, :] = (x_ref[rows, :].astype(F32) * rstd).astype(BF16)

    x = xn_ref[...]
    n_sub = act_ref.shape[1] // sub
    row = lax.broadcasted_iota(jnp.int32, (SUBLANES, sub), 0)
    for c in range(n_sub):
        cols = slice(c * sub, (c + 1) * sub)
        w0 = cw_ref[0:1, cols]
        w1 = cw_ref[1:2, cols]
        w2 = cw_ref[2:3, cols]
        b = cb_ref[:, cols]
        n_chunk = tail_chunks if (c == n_sub - 1 and nseg == 1) else 1
        rows = seg // n_chunk
        if n_chunk == 1:
            g = jnp.dot(x, wg_ref[:, cols], preferred_element_type=F32)
            u = jnp.dot(x, wu_ref[:, cols], preferred_element_type=F32)
        for s in range(nseg):
            halo = gbuf[s * SUBLANES:(s + 1) * SUBLANES, cols]
            for k in range(n_chunk):
                r0 = s * seg + k * rows
                if n_chunk == 1:
                    gs, us = g[r0:r0 + rows], u[r0:r0 + rows]
                else:
                    gs = jnp.dot(x[r0:r0 + rows], wg_ref[:, cols], preferred_element_type=F32)
                    us = jnp.dot(x[r0:r0 + rows], wu_ref[:, cols], preferred_element_type=F32)
                r1 = pltpu.roll(gs, 1, axis=0)
                r2 = pltpu.roll(gs, 2, axis=0)
                h1 = jnp.where(row == 0, halo[SUBLANES - 1:SUBLANES], r1[:SUBLANES])
                h2 = jnp.where(row == 0, halo[SUBLANES - 2:SUBLANES - 1],
                               jnp.where(row == 1, halo[SUBLANES - 1:SUBLANES], r2[:SUBLANES]))
                g1 = jnp.concatenate([h1, r1[SUBLANES:]], axis=0)
                g2 = jnp.concatenate([h2, r2[SUBLANES:]], axis=0)
                gc = b + w0 * g2 + w1 * g1 + w2 * gs
                a = gc * jax.nn.sigmoid(gc) * us
                act_ref[r0:r0 + rows, cols] = a.astype(BF16)
                halo = gs[rows - SUBLANES:]
            gl_ref[s, :, cols] = halo
        if tiles_per_seq > 1:
            carry[j, :, cols] = halo


def ffn_a(hb, ssq, wg, wu, layer, cw, cb, prev, seq_len, *, tm_cap=1024, tf=FFN_TF, sub=2 * LANES):
    m, d = hb.shape
    f = wg.shape[2]
    tm = _tile(m, tm_cap)
    seg = min(seq_len, tm)
    nseg = tm // seg
    tiles_per_seq = seq_len // seg
    nj = f // tf
    if nseg == 1:
        prev_map = lambda i, j: (i // tiles_per_seq, 0, j)
    else:
        prev_map = lambda i, j: (i, 0, j)
    return pl.pallas_call(
        functools.partial(_ffn_a_kernel, seg=seg, nseg=nseg, tiles_per_seq=tiles_per_seq, sub=sub,
                          tail_chunks=4),
        grid=(m // tm, nj),
        in_specs=[pl.BlockSpec((tm, d), lambda i, j: (i, 0)),
                  pl.BlockSpec((tm, LANES), lambda i, j: (i, 0)),
                  pl.BlockSpec((None, d, tf), lambda i, j: (layer, 0, j)),
                  pl.BlockSpec((None, d, tf), lambda i, j: (layer, 0, j)),
                  pl.BlockSpec((CONV_W, tf), lambda i, j: (0, j)),
                  pl.BlockSpec((1, tf), lambda i, j: (0, j)),
                  pl.BlockSpec((nseg, SUBLANES, tf), prev_map)],
        out_specs=[pl.BlockSpec((tm, tf), lambda i, j: (i, j)),
                   pl.BlockSpec((nseg, SUBLANES, tf), lambda i, j: (i, 0, j))],
        out_shape=[jax.ShapeDtypeStruct((m, f), BF16),
                   jax.ShapeDtypeStruct((m // seg, SUBLANES, f), F32)],
        scratch_shapes=[pltpu.VMEM((nseg * SUBLANES, tf), F32),
                        pltpu.VMEM((nj, SUBLANES, tf), F32),
                        pltpu.VMEM((tm, d), BF16)],
        compiler_params=_params(2),
        name="ffn_gate_up_conv",
    )(hb, ssq, wg, wu, cw, cb.reshape(1, f), prev)


def _mla_attn_kernel(q_ref, kn_ref, kr_ref, vt_ref, o_ref, m_ref, acc_ref, s_ref, p_ref,
                     *, hb, tq, tk, nk, q_off, k_valid, cg, rb):
    qi = pl.program_id(2)
    ki = pl.program_id(3)
    q_lo = q_off + qi * tq
    k_lo = ki * tk

    @pl.when(ki == 0)
    def _():
        m_ref[...] = jnp.full(m_ref.shape, NEG_INF, F32)
        acc_ref[...] = jnp.zeros(acc_ref.shape, F32)

    needed = k_lo // CHUNK <= (q_lo + tq - 1) // CHUNK
    full = jnp.logical_and((k_lo + tk - 1) // CHUNK <= q_lo // CHUNK, k_lo + tk <= k_valid)

    def step(masked):
        kr = kr_ref[0]
        ones = jnp.ones((DENOM_ROWS, tk), BF16)
        for h in range(hb):
            buf = h % s_ref.shape[0]
            q = q_ref[0, :, h * MLA_QDIM:(h + 1) * MLA_QDIM]
            k = jnp.concatenate([kn_ref[0, :, h * QK_NOPE:(h + 1) * QK_NOPE], kr], axis=1)
            s_ref[buf] = lax.dot_general(k, q, (((1,), (1,)), ((), ())), preferred_element_type=F32)
            alphas = []
            for c in range(tq // cg):
                cols = slice(c * cg, (c + 1) * cg)
                if masked:
                    qc = (q_lo + c * cg + lax.broadcasted_iota(jnp.int32, (1, cg), 1)) // CHUNK

                def scores(r):
                    s = s_ref[buf, r * rb:(r + 1) * rb, cols]
                    if masked:
                        kp = k_lo + r * rb + lax.broadcasted_iota(jnp.int32, (rb, 1), 0)
                        s = jnp.where(jnp.logical_and(qc >= kp // CHUNK, kp < k_valid), s, NEG_INF)
                    return s

                part = None
                for r in range(tk // rb):
                    blk = jnp.max(scores(r).reshape(rb // SUBLANES, SUBLANES, cg), axis=0)
                    part = blk if part is None else jnp.maximum(part, blk)
                m_prev = m_ref[h, :, cols]
                m_new = jnp.maximum(m_prev, jnp.max(part, axis=0, keepdims=True))
                alphas.append(jnp.exp2(m_prev - m_new))
                m_ref[h, :, cols] = m_new
                for r in range(tk // rb):
                    p_ref[buf, r * rb:(r + 1) * rb, cols] = jnp.exp2(scores(r) - m_new).astype(BF16)
            vt = jnp.concatenate([vt_ref[0, h * V_DIM:(h + 1) * V_DIM, :], ones], axis=0)
            pv = jnp.dot(vt, p_ref[buf], preferred_element_type=F32)
            acc_ref[h] = jnp.concatenate(alphas, axis=1) * acc_ref[h] + pv

    @pl.when(jnp.logical_and(needed, full))
    def _():
        step(False)

    @pl.when(jnp.logical_and(needed, jnp.logical_not(full)))
    def _():
        step(True)

    @pl.when(ki == nk - 1)
    def _():
        for h in range(hb):
            o_t = acc_ref[h, :V_DIM, :] / acc_ref[h, V_DIM:V_DIM + 1, :]
            o_ref[0, :, h * V_DIM:(h + 1) * V_DIM] = o_t.T.astype(BF16)


def mla_attention(q, kn, kr, vt, *, q_off, k_valid, tq, tk, hb=8, n_buf=2):
    bsz, sq, _ = q.shape
    sk = kn.shape[1]
    nq, nk = sq // tq, sk // tk
    n_hg = MLA_HEADS // hb

    def k_idx(qi, ki):
        last = ((q_off + (qi + 1) * tq - 1) // CHUNK * CHUNK + CHUNK - 1) // tk
        return jnp.minimum(ki, jnp.minimum(last, nk - 1))

    return pl.pallas_call(
        functools.partial(_mla_attn_kernel, hb=hb, tq=tq, tk=tk, nk=nk, q_off=q_off,
                          k_valid=k_valid, cg=min(tq, 2 * LANES), rb=LANES),
        grid=(bsz, n_hg, nq, nk),
        in_specs=[pl.BlockSpec((1, tq, hb * MLA_QDIM), lambda b, g, qi, ki: (b, qi, g)),
                  pl.BlockSpec((1, tk, hb * QK_NOPE), lambda b, g, qi, ki: (b, k_idx(qi, ki), g)),
                  pl.BlockSpec((1, tk, LANES), lambda b, g, qi, ki: (b, k_idx(qi, ki), 0)),
                  pl.BlockSpec((1, hb * V_DIM, tk), lambda b, g, qi, ki: (b, g, k_idx(qi, ki)))],
        out_specs=pl.BlockSpec((1, tq, hb * V_DIM), lambda b, g, qi, ki: (b, qi, g)),
        out_shape=jax.ShapeDtypeStruct((bsz, sq, MLA_HEADS * V_DIM), BF16),
        scratch_shapes=[pltpu.VMEM((hb, 1, tq), F32),
                        pltpu.VMEM((hb, V_DIM + DENOM_ROWS, tq), F32),
                        pltpu.VMEM((n_buf, tk, tq), F32),
                        pltpu.VMEM((n_buf, tk, tq), BF16)],
        compiler_params=_params(4),
        name="mla_flash_attention",
    )(q, kn, kr, vt)


def _nt_kernel(w_ref, x_ref, o_ref):
    o_ref[0] = lax.dot_general(w_ref[...], x_ref[0], (((1,), (1,)), ((), ())),
                               preferred_element_type=F32).astype(o_ref.dtype)


def matmul_nt(wt, layer, x, *, tn, ts):
    _, n, k = wt.shape
    bsz, s, _ = x.shape
    return pl.pallas_call(
        _nt_kernel,
        grid=(bsz, s // ts, n // tn),
        in_specs=[pl.BlockSpec((None, tn, k), lambda b, i, j: (layer, j, 0)),
                  pl.BlockSpec((1, ts, k), lambda b, i, j: (b, i, 0))],
        out_specs=pl.BlockSpec((1, tn, ts), lambda b, i, j: (b, j, i)),
        out_shape=jax.ShapeDtypeStruct((bsz, n, s), BF16),
        compiler_params=_params(3),
        name="mla_v_up_proj_t",
    )(wt, x)


def _swa_attn_kernel(sink_ref, q_ref, ka_ref, kb_ref, vta_ref, vtb_ref, o_ref, ot_ref, *, tq, k_valid):
    t = pl.program_id(1)
    k = jnp.concatenate([ka_ref[0], kb_ref[0]], axis=0)
    vt = jnp.concatenate([vta_ref[0], vtb_ref[0]], axis=1)
    tk = k.shape[0]
    kp = lax.broadcasted_iota(jnp.int32, (tk, 1), 0)
    if k_valid is None:
        kp = kp + (t - 1) * tq
        kc = kp // CHUNK
        qc = (t * tq + lax.broadcasted_iota(jnp.int32, (1, tq), 1)) // CHUNK
        valid = jnp.logical_and(kp >= 0, jnp.logical_and(kc >= qc - WINDOW // CHUNK, kc <= qc))
    else:
        valid = jnp.broadcast_to(kp < k_valid, (tk, tq))
    valid = jnp.concatenate([valid] * SWA_GROUP, axis=1)
    lane_head = lax.broadcasted_iota(jnp.int32, (1, HEAD_SLAB), 1) // SWA_HEAD_DIM
    heads_per_slab = HEAD_SLAB // SWA_HEAD_DIM
    for slab in range(SWA_KV_DIM // HEAD_SLAB):
        k_slab = k[:, slab * HEAD_SLAB:(slab + 1) * HEAD_SLAB]
        qs = jnp.concatenate(
            [q_ref[0, :, g * SWA_KV_DIM + slab * HEAD_SLAB:g * SWA_KV_DIM + (slab + 1) * HEAD_SLAB]
             for g in range(SWA_GROUP)], axis=0)
        for hh in range(heads_per_slab):
            kvh = slab * heads_per_slab + hh
            km = k_slab * (lane_head == hh).astype(BF16)
            vth = vt[kvh * SWA_HEAD_DIM:(kvh + 1) * SWA_HEAD_DIM, :]
            sb = jnp.concatenate([jnp.full((1, tq), sink_ref[kvh, g], F32) for g in range(SWA_GROUP)],
                                 axis=1)
            s = lax.dot_general(km, qs, (((1,), (1,)), ((), ())), preferred_element_type=F32)
            s = jnp.where(valid, s, NEG_INF)
            m = jnp.maximum(jnp.max(s, axis=0, keepdims=True), sb)
            e = jnp.exp(s - m)
            inv = 1.0 / (jnp.sum(e, axis=0, keepdims=True) + jnp.exp(sb - m))
            o_t = jnp.dot(vth, (e * inv).astype(BF16), preferred_element_type=F32)
            for g in range(SWA_GROUP):
                row = g * SWA_KV_DIM + kvh * SWA_HEAD_DIM
                ot_ref[row:row + SWA_HEAD_DIM, :] = o_t[:, g * tq:(g + 1) * tq]
    for c in range(D_MODEL // LANES):
        o_ref[0, :, c * LANES:(c + 1) * LANES] = ot_ref[c * LANES:(c + 1) * LANES, :].T.astype(BF16)


def swa_attention(q, k, vt, sink, *, k_valid=None):
    bsz, sq, _ = q.shape
    tq = WINDOW
    if k_valid is None:
        a_idx = lambda t: jnp.maximum(t - 1, 0)
        b_idx = lambda t: t
    else:
        assert sq == tq and k.shape[1] == 2 * WINDOW
        a_idx = lambda t: 0
        b_idx = lambda t: 1
    return pl.pallas_call(
        functools.partial(_swa_attn_kernel, tq=tq, k_valid=k_valid),
        grid=(bsz, sq // tq),
        in_specs=[pl.BlockSpec(memory_space=pltpu.SMEM),
                  pl.BlockSpec((1, tq, D_MODEL), lambda b, t: (b, t, 0)),
                  pl.BlockSpec((1, WINDOW, SWA_KV_DIM), lambda b, t: (b, a_idx(t), 0)),
                  pl.BlockSpec((1, WINDOW, SWA_KV_DIM), lambda b, t: (b, b_idx(t), 0)),
                  pl.BlockSpec((1, SWA_KV_DIM, WINDOW), lambda b, t: (b, 0, a_idx(t))),
                  pl.BlockSpec((1, SWA_KV_DIM, WINDOW), lambda b, t: (b, 0, b_idx(t)))],
        out_specs=pl.BlockSpec((1, tq, D_MODEL), lambda b, t: (b, t, 0)),
        out_shape=jax.ShapeDtypeStruct((bsz, sq, D_MODEL), BF16),
        scratch_shapes=[pltpu.VMEM((D_MODEL, tq), F32)],
        compiler_params=_params(2),
        name="swa_sink_attention",
    )(sink, q, k, k, vt, vt)


def _cast_pad_cols_kernel(x_ref, s_ref, o_ref):
    n = x_ref.shape[-1]
    o_ref[0, :, :n] = (x_ref[0] * s_ref[0]).astype(BF16)
    o_ref[0, :, n:] = jnp.zeros((o_ref.shape[1], o_ref.shape[2] - n), BF16)


def cast_pad_cols(x, row_scale, n_pad, tr=256):
    nl, r, c = x.shape
    return pl.pallas_call(
        _cast_pad_cols_kernel,
        grid=(nl, r // tr),
        in_specs=[pl.BlockSpec((1, tr, c), lambda l, i: (l, i, 0)),
                  pl.BlockSpec((1, tr, 1), lambda l, i: (l, i, 0))],
        out_specs=pl.BlockSpec((1, tr, n_pad), lambda l, i: (l, i, 0)),
        out_shape=jax.ShapeDtypeStruct((nl, r, n_pad), BF16),
        compiler_params=_params(2),
        name="cast_pad_cols",
    )(x, row_scale.reshape(nl, r, 1))


def _cast_pad_rows_kernel(x_ref, o_ref, *, n_blocks):
    i = pl.program_id(1)

    @pl.when(i < n_blocks)
    def _():
        o_ref[0] = x_ref[0].astype(BF16)

    @pl.when(i >= n_blocks)
    def _():
        o_ref[0] = jnp.zeros(o_ref.shape[1:], BF16)


def cast_pad_rows(x, r_pad, tr=256):
    nl, r, c = x.shape
    n_blocks = r // tr
    return pl.pallas_call(
        functools.partial(_cast_pad_rows_kernel, n_blocks=n_blocks),
        grid=(nl, r_pad // tr),
        in_specs=[pl.BlockSpec((1, tr, c), lambda l, i: (l, jnp.minimum(i, n_blocks - 1), 0))],
        out_specs=pl.BlockSpec((1, tr, c), lambda l, i: (l, i, 0)),
        out_shape=jax.ShapeDtypeStruct((nl, r_pad, c), BF16),
        compiler_params=_params(2),
        name="cast_pad_rows",
    )(x)


def _rope_slab_cols(w_rope):
    half = QK_ROPE // 2
    x1, x2 = w_rope[..., :half], w_rope[..., half:]
    return jnp.concatenate([x1, x2, x2, x1], axis=-1)


def _mla_tables(pos):
    half = QK_ROPE // 2
    inv = jnp.power(jnp.float32(ROPE_THETA), -jnp.arange(half, dtype=F32) * (2.0 / QK_ROPE))
    ang = pos.astype(F32)[:, None] * inv[None, :]
    c, s = jnp.cos(ang), jnp.sin(ang)
    z = jnp.zeros_like(c)
    return jnp.concatenate([c, c, z, z], axis=1), jnp.concatenate([-s, s, z, z], axis=1)


def _swa_tables(pos):
    half = ROPE_DIM_B // 2
    inv = jnp.power(jnp.float32(ROPE_THETA), -jnp.arange(half, dtype=F32) * (2.0 / ROPE_DIM_B))
    ang = pos.astype(F32)[:, None] * inv[None, :]
    c, s = jnp.cos(ang), jnp.sin(ang)
    n = pos.shape[0]
    rest = SWA_HEAD_DIM - ROPE_DIM_B
    c64 = jnp.concatenate([c, c, jnp.ones((n, rest), F32)], axis=1)
    s1 = jnp.concatenate([-s, jnp.zeros((n, SWA_HEAD_DIM - half), F32)], axis=1)
    s2 = jnp.concatenate([jnp.zeros((n, half), F32), s, jnp.zeros((n, rest), F32)], axis=1)
    rep = LANES // SWA_HEAD_DIM
    return jnp.tile(c64, (1, rep)), jnp.tile(s1, (1, rep)), jnp.tile(s2, (1, rep))


def _prep_weights(p):
    w = {}
    wa = p['mla_w_a'].astype(BF16)
    w['mla_w_a'] = jnp.concatenate(
        [wa[..., :Q_LORA + KV_LORA], _rope_slab_cols(wa[..., Q_LORA + KV_LORA:])], axis=-1)
    wuq = p['mla_w_uq'].astype(BF16).reshape(N_A, Q_LORA, MLA_HEADS, QK_NOPE + QK_ROPE)
    w['mla_w_uq'] = jnp.concatenate(
        [wuq[..., :QK_NOPE], _rope_slab_cols(wuq[..., QK_NOPE:])], axis=-1
    ).reshape(N_A, Q_LORA, MLA_HEADS * MLA_QDIM)
    w['mla_w_uk'] = p['mla_w_uk'].reshape(N_A, KV_LORA, MLA_HEADS * QK_NOPE).astype(BF16)
    w['mla_w_uvt'] = p['mla_w_uv'].reshape(N_A, KV_LORA, MLA_HEADS * V_DIM).transpose(0, 2, 1).astype(BF16)
    w['mla_w_o'] = p['mla_w_o'].astype(BF16)
    w['swa_w_kv'] = p['swa_w_kv'].astype(BF16)
    wq = p['swa_w_q'].reshape(N_B, D_MODEL, SWA_KV_HEADS, SWA_GROUP, SWA_HEAD_DIM)
    w['swa_w_q'] = wq.transpose(0, 1, 3, 2, 4).reshape(N_B, D_MODEL, D_MODEL).astype(BF16)
    wo = p['swa_w_o'].reshape(N_B, SWA_KV_HEADS, SWA_GROUP, SWA_HEAD_DIM, D_MODEL)
    w['swa_w_o'] = wo.transpose(0, 2, 1, 3, 4).reshape(N_B, D_MODEL, D_MODEL).astype(BF16)
    w['swa_sinks'] = p['swa_sinks'].reshape(N_B, SWA_KV_HEADS, SWA_GROUP)
    fpad = F_PAD - D_FF
    w['ffn_w_gate'] = cast_pad_cols(p['ffn_w_gate'], p['norm_ffn'], F_PAD)
    w['ffn_w_up'] = cast_pad_cols(p['ffn_w_up'], p['norm_ffn'], F_PAD)
    w['ffn_w_down'] = cast_pad_rows(p['ffn_w_down'], F_PAD)
    w['ffn_conv_w'] = jnp.pad(p['ffn_conv_w'], ((0, 0), (0, 0), (0, fpad)))
    w['ffn_conv_b'] = jnp.pad(p['ffn_conv_b'], ((0, 0), (0, fpad)))
    return w


def _trunk(x, pos, p, w, cache):
    bsz, seq, _ = x.shape
    m = bsz * seq
    h = x.reshape(m, D_MODEL)
    tm = _tile(m, 512)
    tm_l = _tile(m, 1024)
    past = 0 if cache is None else cache['ckv'].shape[2]

    mla_cos, mla_sin = (jnp.tile(t, (bsz, 1)) for t in _mla_tables(pos))
    swa_c, swa_s1, swa_s2 = (jnp.tile(t, (bsz, 1)) for t in _swa_tables(pos))
    tab = _row_spec(tm, LANES)
    tab_l = _row_spec(tm_l, LANES)

    def out_proj(o, w_o, layer, h_res):
        return matmul(o, w_o, layer=layer, tm=tm, tn=1024, epilogue=_epi_residual_stats,
                      extras=(h_res,), extra_specs=(_tile_spec(tm, 1024),),
                      out_shapes=[jax.ShapeDtypeStruct((m, D_MODEL), F32),
                                  jax.ShapeDtypeStruct((m, D_MODEL), BF16),
                                  jax.ShapeDtypeStruct((m, LANES), F32)],
                      out_specs=[_tile_spec(tm, 1024), _tile_spec(tm, 1024), _row_spec(tm, LANES)],
                      name="out_proj_residual")

    ckv_rows, krope_rows, conv_rows = [], [], []
    ks = vs = ks_b = vs_t = None
    for l in range(DEPTH):
        if l < N_A:
            n_a = w['mla_w_a'].shape[2]
            cq, ckv, ckv_b, krope, kr_b = matmul(
                h, w['mla_w_a'], layer=l, tm=tm, tn=n_a, epilogue=_epi_mla_in, norm_x=True,
                extras=(p['mla_g_q'][l].reshape(1, Q_LORA), p['mla_g_kv'][l].reshape(1, KV_LORA),
                        mla_cos, mla_sin, p['norm_attn'][l].reshape(1, D_MODEL)),
                extra_specs=(_const_spec(Q_LORA), _const_spec(KV_LORA), tab, tab,
                             _const_spec(D_MODEL)),
                out_shapes=[jax.ShapeDtypeStruct((m, Q_LORA), BF16),
                            jax.ShapeDtypeStruct((m, KV_LORA), F32),
                            jax.ShapeDtypeStruct((m, KV_LORA), BF16),
                            jax.ShapeDtypeStruct((m, QK_ROPE), F32),
                            jax.ShapeDtypeStruct((m, LANES), BF16)],
                out_specs=[_row_spec(tm, Q_LORA), _row_spec(tm, KV_LORA), _row_spec(tm, KV_LORA),
                           _row_spec(tm, QK_ROPE), _row_spec(tm, LANES)],
                name="mla_in_proj")
            ckv_rows.append(ckv.reshape(bsz, seq, KV_LORA))
            krope_rows.append(krope.reshape(bsz, seq, QK_ROPE))
            tn_q = 8 * MLA_QDIM
            q = matmul(cq, w['mla_w_uq'], layer=l, tm=tm_l, tn=tn_q, epilogue=_epi_mla_q,
                       extras=(mla_cos, mla_sin), extra_specs=(tab_l, tab_l),
                       out_shapes=jax.ShapeDtypeStruct((m, MLA_HEADS * MLA_QDIM), BF16),
                       out_specs=_tile_spec(tm_l, tn_q), name="mla_q_proj")
            if cache is None:
                ckv_all = ckv_b.reshape(bsz, seq, KV_LORA)
                kr_all = kr_b.reshape(bsz, seq, LANES)
                k_valid = seq
                tq = tk = min(seq, 1024)
            else:
                k_valid = past + seq
                sk = -(-k_valid // LANES) * LANES
                ckv_all = jnp.concatenate(
                    [cache['ckv'][l].astype(BF16), ckv_b.reshape(bsz, seq, KV_LORA),
                     jnp.zeros((bsz, sk - k_valid, KV_LORA), BF16)], axis=1)
                kr_cache = jnp.pad(cache['krope'][l], ((0, 0), (0, 0), (0, LANES - QK_ROPE)))
                kr_all = jnp.concatenate(
                    [kr_cache.astype(BF16), kr_b.reshape(bsz, seq, LANES),
                     jnp.zeros((bsz, sk - k_valid, LANES), BF16)], axis=1)
                tq, tk = LANES, sk
            sk = ckv_all.shape[1]
            mk = bsz * sk
            tmk = _tile(mk, 1024)
            kn = matmul(ckv_all.reshape(mk, KV_LORA), w['mla_w_uk'], layer=l, tm=tmk, tn=2048,
                        epilogue=_epi_cast,
                        out_shapes=jax.ShapeDtypeStruct((mk, MLA_HEADS * QK_NOPE), BF16),
                        out_specs=_tile_spec(tmk, 2048), name="mla_k_up_proj")
            if sk % 512 == 0:
                vt = matmul_nt(w['mla_w_uvt'], l, ckv_all, tn=MLA_HEADS * V_DIM, ts=512)
            else:
                vt = matmul_nt(w['mla_w_uvt'], l, ckv_all, tn=1024, ts=sk)
            q3 = q.reshape(bsz, seq, MLA_HEADS * MLA_QDIM)
            if seq < tq:
                q3 = jnp.pad(q3, ((0, 0), (0, tq - seq), (0, 0)))
            o = mla_attention(q3, kn.reshape(bsz, sk, MLA_HEADS * QK_NOPE), kr_all, vt,
                              q_off=past, k_valid=k_valid, tq=tq, tk=tk)
            h, hb, ssq = out_proj(o[:, :seq].reshape(m, D_MODEL), w['mla_w_o'], l, h)
        else:
            i = l - N_A
            if l == N_A:
                ks, vs, ks_b, vs_t = matmul(
                    h, w['swa_w_kv'], tm=tm, tn=2 * SWA_KV_DIM, epilogue=_epi_shared_kv,
                    norm_x=True,
                    extras=(swa_c, swa_s1, swa_s2, p['kv_shared_norm'].reshape(1, D_MODEL)),
                    extra_specs=(tab, tab, tab, _const_spec(D_MODEL)),
                    out_shapes=[jax.ShapeDtypeStruct((m, SWA_KV_DIM), F32)] * 2
                    + [jax.ShapeDtypeStruct((m, SWA_KV_DIM), BF16),
                       jax.ShapeDtypeStruct((SWA_KV_DIM, m), BF16)],
                    out_specs=[_row_spec(tm, SWA_KV_DIM)] * 3
                    + [pl.BlockSpec((SWA_KV_DIM, tm), lambda i, j: (0, i))],
                    name="swa_shared_kv_proj")
            hn = rmsnorm(h, p['norm_attn'][l], BF16)
            q = matmul(hn, w['swa_w_q'], layer=i, tm=tm_l, tn=1024, epilogue=_epi_swa_q,
                       extras=(swa_c, swa_s1, swa_s2), extra_specs=(tab_l, tab_l, tab_l),
                       out_shapes=jax.ShapeDtypeStruct((m, D_MODEL), BF16),
                       out_specs=_tile_spec(tm_l, 1024), name="swa_q_proj")
            q3 = q.reshape(bsz, seq, D_MODEL)
            k3 = ks_b.reshape(bsz, seq, SWA_KV_DIM)
            vt3 = vs_t.reshape(SWA_KV_DIM, bsz, seq).transpose(1, 0, 2)
            sink = w['swa_sinks'][i]
            if cache is None:
                o = swa_attention(q3, k3, vt3, sink)
            else:
                win = cache['swa_k'].shape[1]
                npad = 2 * WINDOW - win - seq
                k_all = jnp.concatenate(
                    [cache['swa_k'].reshape(bsz, win, SWA_KV_DIM).astype(BF16), k3,
                     jnp.zeros((bsz, npad, SWA_KV_DIM), BF16)], axis=1)
                vt_cache = cache['swa_v'].reshape(bsz, win, SWA_KV_DIM).transpose(0, 2, 1)
                vt_all = jnp.concatenate(
                    [vt_cache.astype(BF16), vt3, jnp.zeros((bsz, SWA_KV_DIM, npad), BF16)], axis=2)
                q3 = jnp.pad(q3, ((0, 0), (0, WINDOW - seq), (0, 0)))
                o = swa_attention(q3, k_all, vt_all, sink, k_valid=win + seq)[:, :seq]
            h, hb, ssq = out_proj(o.reshape(m, D_MODEL), w['swa_w_o'], i, h)

        if cache is None:
            prev = jnp.zeros((bsz, SUBLANES, F_PAD), F32)
        else:
            prev = jnp.pad(cache['conv'][l],
                           ((0, 0), (SUBLANES - (CONV_W - 1), 0), (0, F_PAD - D_FF)))
        act, gl = ffn_a(hb, ssq, w['ffn_w_gate'], w['ffn_w_up'], l, w['ffn_conv_w'][l],
                        w['ffn_conv_b'][l], prev, seq)
        gl = gl.reshape(bsz, -1, SUBLANES, F_PAD)
        conv_rows.append(gl[:, -1, SUBLANES - (CONV_W - 1):, :D_FF])
        h = matmul(act, w['ffn_w_down'], layer=l, tm=tm, tn=512, epilogue=_epi_residual,
                   extras=(h,), extra_specs=(_tile_spec(tm, 512),),
                   out_shapes=jax.ShapeDtypeStruct((m, D_MODEL), F32),
                   out_specs=_tile_spec(tm, 512), name="ffn_down_residual")

    y = rmsnorm(h, p['norm_final'], F32).reshape(bsz, seq, D_MODEL)
    win = min(WINDOW, seq) if cache is None else seq
    ks4 = ks.reshape(bsz, seq, SWA_KV_DIM)[:, seq - win:].reshape(bsz, win, SWA_KV_HEADS, SWA_HEAD_DIM)
    vs4 = vs.reshape(bsz, seq, SWA_KV_DIM)[:, seq - win:].reshape(bsz, win, SWA_KV_HEADS, SWA_HEAD_DIM)
    return y, jnp.stack(ckv_rows), jnp.stack(krope_rows), ks4, vs4, jnp.stack(conv_rows)


def kernel(x_prompt, x_sample, cache_mla_ckv, cache_mla_krope, cache_swa_k, cache_swa_v, state_conv,
           norm_attn, norm_ffn, mla_w_a, mla_g_q, mla_g_kv, mla_w_uq, mla_w_uk, mla_w_uv, mla_w_o,
           kv_shared_norm, swa_w_kv, swa_w_q, swa_sinks, swa_w_o,
           ffn_w_gate, ffn_w_up, ffn_conv_w, ffn_conv_b, ffn_w_down, norm_final):
    p = {
        'norm_attn': norm_attn, 'norm_ffn': norm_ffn,
        'mla_w_a': mla_w_a, 'mla_g_q': mla_g_q, 'mla_g_kv': mla_g_kv, 'mla_w_uq': mla_w_uq,
        'mla_w_uk': mla_w_uk, 'mla_w_uv': mla_w_uv, 'mla_w_o': mla_w_o,
        'kv_shared_norm': kv_shared_norm, 'swa_w_kv': swa_w_kv, 'swa_w_q': swa_w_q,
        'swa_sinks': swa_sinks, 'swa_w_o': swa_w_o,
        'ffn_w_gate': ffn_w_gate, 'ffn_w_up': ffn_w_up, 'ffn_conv_w': ffn_conv_w,
        'ffn_conv_b': ffn_conv_b, 'ffn_w_down': ffn_w_down, 'norm_final': norm_final,
    }
    w = _prep_weights(p)
    pos_p = jnp.arange(x_prompt.shape[1])
    out_p = _trunk(x_prompt, pos_p, p, w, None)
    past = cache_mla_ckv.shape[2]
    pos_s = past + jnp.arange(x_sample.shape[1])
    cache = {'ckv': cache_mla_ckv, 'krope': cache_mla_krope, 'swa_k': cache_swa_k,
             'swa_v': cache_swa_v, 'conv': state_conv}
    out_s = _trunk(x_sample, pos_s, p, w, cache)
    return (out_p[0], out_s[0]) + out_p[1:] + out_s[1:]
```

```python
import functools

import jax
import jax.numpy as jnp
from jax import lax
from jax.experimental import pallas as pl
from jax.experimental.pallas import tpu as pltpu

D_MODEL = 4096
DEPTH = 4
CHUNK = 64
N_A = DEPTH // 2
N_B = DEPTH - N_A
ROPE_THETA = 500000.0
EPS = 1e-6
NEG_INF = -1e30
MLA_HEADS = D_MODEL // 128
Q_LORA = D_MODEL // 4
KV_LORA = 512
QK_NOPE = 128
QK_ROPE = 64
V_DIM = 128
MLA_SCALE = (QK_NOPE + QK_ROPE) ** -0.5
LOG2E = 1.4426950408889634
MLA_Q_SCALE = MLA_SCALE * LOG2E
SWA_HEAD_DIM = 64
SWA_HEADS = D_MODEL // SWA_HEAD_DIM
SWA_KV_HEADS = 8
SWA_GROUP = SWA_HEADS // SWA_KV_HEADS
WINDOW = 128
ROPE_DIM_B = SWA_HEAD_DIM // 4
SWA_SCALE = SWA_HEAD_DIM ** -0.5
D_FF = 256 * ((8 * D_MODEL // 3 + 255) // 256)
CONV_W = 3
FFN_TF = 512
F_PAD = -(-D_FF // FFN_TF) * FFN_TF

LANES = 128
SUBLANES = 8
V7X_VMEM_LIMIT = 56 * 1024 * 1024

MLA_QDIM = 2 * LANES
DENOM_ROWS = 16
SWA_KV_DIM = SWA_KV_HEADS * SWA_HEAD_DIM
HEAD_SLAB = 2 * LANES
BF16 = jnp.bfloat16
F32 = jnp.float32


def _params(n_grid):
    return pltpu.CompilerParams(dimension_semantics=("arbitrary",) * n_grid,
                                vmem_limit_bytes=V7X_VMEM_LIMIT)


def _tile(m, cap):
    for step in (LANES, 16):
        t = (min(cap, m) // step) * step
        while t >= step:
            if m % t == 0:
                return t
            t -= step
    return m


def _rms(x, g):
    return x * lax.rsqrt(jnp.mean(x * x, axis=-1, keepdims=True) + EPS) * g


def _rmsnorm_kernel(x_ref, g_ref, o_ref):
    o_ref[...] = _rms(x_ref[...], g_ref[...]).astype(o_ref.dtype)


def rmsnorm(x, g, out_dtype):
    m, d = x.shape
    tm = _tile(m, 512)
    return pl.pallas_call(
        _rmsnorm_kernel,
        grid=(m // tm,),
        in_specs=[pl.BlockSpec((tm, d), lambda i: (i, 0)),
                  pl.BlockSpec((1, d), lambda i: (0, 0))],
        out_specs=pl.BlockSpec((tm, d), lambda i: (i, 0)),
        out_shape=jax.ShapeDtypeStruct((m, d), out_dtype),
        compiler_params=_params(1),
        name="rmsnorm",
    )(x, g.reshape(1, d))


def _mm_kernel(x_ref, w_ref, *rest, epilogue, n_extra, norm_x):
    extras = rest[:n_extra]
    if norm_x:
        x = _rms(x_ref[...], extras[-1][...]).astype(BF16)
    else:
        x = x_ref[...]
    y = jnp.dot(x, w_ref[...], preferred_element_type=F32)
    epilogue(y, extras, rest[n_extra:])


def matmul(x, w, *, tm, tn, epilogue, extras=(), extra_specs=(), out_shapes, out_specs, name,
           norm_x=False, layer=None):
    m, k = x.shape
    n = w.shape[-1]
    assert not norm_x or tn == n
    w_mode = dict(pipeline_mode=pl.Buffered(1)) if tn == n else {}
    if layer is None:
        w_spec = pl.BlockSpec((k, tn), lambda i, j: (0, j), **w_mode)
    else:
        w_spec = pl.BlockSpec((None, k, tn), lambda i, j: (layer, 0, j), **w_mode)
    return pl.pallas_call(
        functools.partial(_mm_kernel, epilogue=epilogue, n_extra=len(extras), norm_x=norm_x),
        grid=(m // tm, n // tn),
        in_specs=[pl.BlockSpec((tm, k), lambda i, j: (i, 0)), w_spec] + list(extra_specs),
        out_specs=out_specs,
        out_shape=out_shapes,
        compiler_params=_params(2),
        name=name,
    )(x, w, *extras)


def _epi_cast(y, extras, outs):
    outs[0][...] = y.astype(outs[0].dtype)


def _epi_residual(y, extras, outs):
    outs[0][...] = extras[0][...] + y


def _rope_slab(a, cos2, sin2):
    return a * cos2 + pltpu.roll(a, QK_ROPE, axis=1) * sin2


def _epi_mla_in(y, extras, outs):
    gq_ref, gkv_ref, cos_ref, sin_ref = extras[:4]
    cq_ref, ckv_ref, ckvb_ref, kr_ref, krb_ref = outs
    cq_ref[...] = _rms(y[:, :Q_LORA], gq_ref[...]).astype(BF16)
    ckv = _rms(y[:, Q_LORA:Q_LORA + KV_LORA], gkv_ref[...])
    ckv_ref[...] = ckv
    ckvb_ref[...] = ckv.astype(BF16)
    kr = _rope_slab(y[:, Q_LORA + KV_LORA:], cos_ref[...], sin_ref[...])
    kr_ref[...] = kr[:, :QK_ROPE]
    krb_ref[...] = kr.astype(BF16)


def _epi_mla_q(y, extras, outs):
    cos_ref, sin_ref = extras
    cos2 = cos_ref[...] * MLA_Q_SCALE
    sin2 = sin_ref[...] * MLA_Q_SCALE
    for s in range(y.shape[1] // MLA_QDIM):
        lo = s * MLA_QDIM
        outs[0][:, lo:lo + LANES] = (y[:, lo:lo + LANES] * MLA_Q_SCALE).astype(BF16)
        outs[0][:, lo + LANES:lo + MLA_QDIM] = _rope_slab(
            y[:, lo + LANES:lo + MLA_QDIM], cos2, sin2).astype(BF16)


def _rope_b(y, c, s1, s2, scale):
    half = ROPE_DIM_B // 2
    for s in range(y.shape[1] // LANES):
        a = y[:, s * LANES:(s + 1) * LANES]
        r = a * c + pltpu.roll(a, LANES - half, axis=1) * s1 + pltpu.roll(a, half, axis=1) * s2
        yield s, (r * scale if scale != 1.0 else r)


def _epi_swa_q(y, extras, outs):
    c_ref, s1_ref, s2_ref = extras
    for s, r in _rope_b(y, c_ref[...], s1_ref[...], s2_ref[...], SWA_SCALE):
        outs[0][:, s * LANES:(s + 1) * LANES] = r.astype(BF16)


def _epi_shared_kv(y, extras, outs):
    c_ref, s1_ref, s2_ref = extras[:3]
    k_ref, v_ref, kb_ref, vtb_ref = outs
    for s, r in _rope_b(y[:, :SWA_KV_DIM], c_ref[...], s1_ref[...], s2_ref[...], 1.0):
        k_ref[:, s * LANES:(s + 1) * LANES] = r
        kb_ref[:, s * LANES:(s + 1) * LANES] = r.astype(BF16)
    v = y[:, SWA_KV_DIM:]
    v_ref[...] = v
    vtb_ref[...] = v.T.astype(BF16)


def _row_spec(tm, width):
    return pl.BlockSpec((tm, width), lambda i, j: (i, 0))


def _tile_spec(tm, tn):
    return pl.BlockSpec((tm, tn), lambda i, j: (i, j))


def _const_spec(width):
    return pl.BlockSpec((1, width), lambda i, j: (0, 0))


def _ffn_a_kernel(x_ref, wg_ref, wu_ref, cw_ref, cb_ref, prev_ref, act_ref, gl_ref, gbuf, carry,
                  *, seg, nseg, tiles_per_seq, sub, tail_chunks):
    i = pl.program_id(0)
    j = pl.program_id(1)
    if tiles_per_seq > 1:
        @pl.when(jnp.logical_and(i == 0, j == 0))
        def _():
            carry[...] = jnp.zeros(carry.shape, F32)

        gbuf[0:SUBLANES] = jnp.where(i % tiles_per_seq == 0, prev_ref[0], carry[j])
    else:
        for s in range(nseg):
            gbuf[s * SUBLANES:(s + 1) * SUBLANES] = prev_ref[s]
    x = x_ref[...]
    n_sub = act_ref.shape[1] // sub
    row = lax.broadcasted_iota(jnp.int32, (SUBLANES, sub), 0)
    for c in range(n_sub):
        cols = slice(c * sub, (c + 1) * sub)
        w0 = cw_ref[0:1, cols]
        w1 = cw_ref[1:2, cols]
        w2 = cw_ref[2:3, cols]
        b = cb_ref[:, cols]
        n_chunk = tail_chunks if (c == n_sub - 1 and nseg == 1) else 1
        rows = seg // n_chunk
        if n_chunk == 1:
            g = jnp.dot(x, wg_ref[:, cols], preferred_element_type=F32)
            u = jnp.dot(x, wu_ref[:, cols], preferred_element_type=F32)
        for s in range(nseg):
            halo = gbuf[s * SUBLANES:(s + 1) * SUBLANES, cols]
            for k in range(n_chunk):
                r0 = s * seg + k * rows
                if n_chunk == 1:
                    gs, us = g[r0:r0 + rows], u[r0:r0 + rows]
                else:
                    gs = jnp.dot(x[r0:r0 + rows], wg_ref[:, cols], preferred_element_type=F32)
                    us = jnp.dot(x[r0:r0 + rows], wu_ref[:, cols], preferred_element_type=F32)
                r1 = pltpu.roll(gs, 1, axis=0)
                r2 = pltpu.roll(gs, 2, axis=0)
                h1 = jnp.where(row == 0, halo[SUBLANES - 1:SUBLANES], r1[:SUBLANES])
                h2 = jnp.where(row == 0, halo[SUBLANES - 2:SUBLANES - 1],
                               jnp.where(row == 1, halo[SUBLANES - 1:SUBLANES], r2[:SUBLANES]))
                g1 = jnp.concatenate([h1, r1[SUBLANES:]], axis=0)
                g2 = jnp.concatenate([h2, r2[SUBLANES:]], axis=0)
                gc = b + w0 * g2 + w1 * g1 + w2 * gs
                a = gc * jax.nn.sigmoid(gc) * us
                act_ref[r0:r0 + rows, cols] = a.astype(BF16)
                halo = gs[rows - SUBLANES:]
            gl_ref[s, :, cols] = halo
        if tiles_per_seq > 1:
            carry[j, :, cols] = halo


def ffn_a(hn, wg, wu, layer, cw, cb, prev, seq_len, *, tm_cap=1024, tf=FFN_TF, sub=2 * LANES):
    m, d = hn.shape
    f = wg.shape[2]
    tm = _tile(m, tm_cap)
    seg = min(seq_len, tm)
    nseg = tm // seg
    tiles_per_seq = seq_len // seg
    nj = f // tf
    if nseg == 1:
        prev_map = lambda i, j: (i // tiles_per_seq, 0, j)
    else:
        prev_map = lambda i, j: (i, 0, j)
    return pl.pallas_call(
        functools.partial(_ffn_a_kernel, seg=seg, nseg=nseg, tiles_per_seq=tiles_per_seq, sub=sub,
                          tail_chunks=4),
        grid=(m // tm, nj),
        in_specs=[pl.BlockSpec((tm, d), lambda i, j: (i, 0)),
                  pl.BlockSpec((None, d, tf), lambda i, j: (layer, 0, j)),
                  pl.BlockSpec((None, d, tf), lambda i, j: (layer, 0, j)),
                  pl.BlockSpec((CONV_W, tf), lambda i, j: (0, j)),
                  pl.BlockSpec((1, tf), lambda i, j: (0, j)),
                  pl.BlockSpec((nseg, SUBLANES, tf), prev_map)],
        out_specs=[pl.BlockSpec((tm, tf), lambda i, j: (i, j)),
                   pl.BlockSpec((nseg, SUBLANES, tf), lambda i, j: (i, 0, j))],
        out_shape=[jax.ShapeDtypeStruct((m, f), BF16),
                   jax.ShapeDtypeStruct((m // seg, SUBLANES, f), F32)],
        scratch_shapes=[pltpu.VMEM((nseg * SUBLANES, tf), F32),
                        pltpu.VMEM((nj, SUBLANES, tf), F32)],
        compiler_params=_params(2),
        name="ffn_gate_up_conv",
    )(hn, wg, wu, cw, cb.reshape(1, f), prev)


def _tile_pair(a, u, nq, paired):
    if not paired:
        return a, u
    first = u <= a
    return jnp.where(first, a, nq - 1 - a), jnp.where(first, u, u - a - 1)


def _mla_attn_kernel(q_ref, kn_ref, kr_ref, vt_ref, o_ref, m_ref, acc_ref, s_ref, p_ref,
                     *, hb, tq, tk, nq, nk, q_off, k_valid, cg, rb, paired):
    qi, ki = _tile_pair(pl.program_id(2), pl.program_id(3), nq, paired)
    q_lo = q_off + qi * tq
    k_lo = ki * tk
    k_last = jnp.minimum(((q_lo + tq - 1) // CHUNK * CHUNK + CHUNK - 1) // tk, nk - 1)

    @pl.when(ki == 0)
    def _():
        m_ref[...] = jnp.full(m_ref.shape, NEG_INF, F32)
        acc_ref[...] = jnp.zeros(acc_ref.shape, F32)

    needed = k_lo // CHUNK <= (q_lo + tq - 1) // CHUNK
    full = jnp.logical_and((k_lo + tk - 1) // CHUNK <= q_lo // CHUNK, k_lo + tk <= k_valid)

    def step(masked):
        kr = kr_ref[0]
        ones = jnp.ones((DENOM_ROWS, tk), BF16)
        for h in range(hb):
            buf = h % s_ref.shape[0]
            q = q_ref[0, :, h * MLA_QDIM:(h + 1) * MLA_QDIM]
            k = jnp.concatenate([kn_ref[0, :, h * QK_NOPE:(h + 1) * QK_NOPE], kr], axis=1)
            s_ref[buf] = lax.dot_general(k, q, (((1,), (1,)), ((), ())), preferred_element_type=F32)
            alphas = []
            for c in range(tq // cg):
                cols = slice(c * cg, (c + 1) * cg)
                if masked:
                    qc = (q_lo + c * cg + lax.broadcasted_iota(jnp.int32, (1, cg), 1)) // CHUNK

                def scores(r):
                    s = s_ref[buf, r * rb:(r + 1) * rb, cols]
                    if masked:
                        kp = k_lo + r * rb + lax.broadcasted_iota(jnp.int32, (rb, 1), 0)
                        s = jnp.where(jnp.logical_and(qc >= kp // CHUNK, kp < k_valid), s, NEG_INF)
                    return s

                part = None
                for r in range(tk // rb):
                    blk = jnp.max(scores(r).reshape(rb // SUBLANES, SUBLANES, cg), axis=0)
                    part = blk if part is None else jnp.maximum(part, blk)
                m_prev = m_ref[h, :, cols]
                m_new = jnp.maximum(m_prev, jnp.max(part, axis=0, keepdims=True))
                alphas.append(jnp.exp2(m_prev - m_new))
                m_ref[h, :, cols] = m_new
                for r in range(tk // rb):
                    p_ref[buf, r * rb:(r + 1) * rb, cols] = jnp.exp2(scores(r) - m_new).astype(BF16)
            vt = jnp.concatenate([vt_ref[0, h * V_DIM:(h + 1) * V_DIM, :], ones], axis=0)
            pv = jnp.dot(vt, p_ref[buf], preferred_element_type=F32)
            acc_ref[h] = jnp.concatenate(alphas, axis=1) * acc_ref[h] + pv

    @pl.when(jnp.logical_and(needed, full))
    def _():
        step(False)

    @pl.when(jnp.logical_and(needed, jnp.logical_not(full)))
    def _():
        step(True)

    @pl.when(ki == k_last)
    def _():
        for h in range(hb):
            o_t = acc_ref[h, :V_DIM, :] / acc_ref[h, V_DIM:V_DIM + 1, :]
            o_ref[0, :, h * V_DIM:(h + 1) * V_DIM] = o_t.T.astype(BF16)


def mla_attention(q, kn, kr, vt, *, q_off, k_valid, tq, tk, hb=8, n_buf=2):
    bsz, sq, _ = q.shape
    sk = kn.shape[1]
    nq, nk = sq // tq, sk // tk
    n_hg = MLA_HEADS // hb

    paired = q_off == 0 and tq == tk and nq == nk and nq % 2 == 0

    def q_idx(a, u):
        return _tile_pair(a, u, nq, paired)[0]

    def k_idx(a, u):
        qi, ki = _tile_pair(a, u, nq, paired)
        last = ((q_off + (qi + 1) * tq - 1) // CHUNK * CHUNK + CHUNK - 1) // tk
        return jnp.minimum(ki, jnp.minimum(last, nk - 1))

    return pl.pallas_call(
        functools.partial(_mla_attn_kernel, hb=hb, tq=tq, tk=tk, nq=nq, nk=nk, q_off=q_off,
                          k_valid=k_valid, cg=min(tq, 2 * LANES), rb=LANES, paired=paired),
        grid=(bsz, n_hg, nq // 2, nq + 1) if paired else (bsz, n_hg, nq, nk),
        in_specs=[pl.BlockSpec((1, tq, hb * MLA_QDIM), lambda b, g, a, u: (b, q_idx(a, u), g)),
                  pl.BlockSpec((1, tk, hb * QK_NOPE), lambda b, g, a, u: (b, k_idx(a, u), g)),
                  pl.BlockSpec((1, tk, LANES), lambda b, g, a, u: (b, k_idx(a, u), 0)),
                  pl.BlockSpec((1, hb * V_DIM, tk), lambda b, g, a, u: (b, g, k_idx(a, u)))],
        out_specs=pl.BlockSpec((1, tq, hb * V_DIM), lambda b, g, a, u: (b, q_idx(a, u), g)),
        out_shape=jax.ShapeDtypeStruct((bsz, sq, MLA_HEADS * V_DIM), BF16),
        scratch_shapes=[pltpu.VMEM((hb, 1, tq), F32),
                        pltpu.VMEM((hb, V_DIM + DENOM_ROWS, tq), F32),
                        pltpu.VMEM((n_buf, tk, tq), F32),
                        pltpu.VMEM((n_buf, tk, tq), BF16)],
        compiler_params=_params(4),
        name="mla_flash_attention",
    )(q, kn, kr, vt)


def _nt_kernel(w_ref, x_ref, o_ref):
    o_ref[0] = lax.dot_general(w_ref[...], x_ref[0], (((1,), (1,)), ((), ())),
                               preferred_element_type=F32).astype(o_ref.dtype)


def matmul_nt(wt, layer, x, *, tn, ts):
    _, n, k = wt.shape
    bsz, s, _ = x.shape
    return pl.pallas_call(
        _nt_kernel,
        grid=(bsz, s // ts, n // tn),
        in_specs=[pl.BlockSpec((None, tn, k), lambda b, i, j: (layer, j, 0)),
                  pl.BlockSpec((1, ts, k), lambda b, i, j: (b, i, 0))],
        out_specs=pl.BlockSpec((1, tn, ts), lambda b, i, j: (b, j, i)),
        out_shape=jax.ShapeDtypeStruct((bsz, n, s), BF16),
        compiler_params=_params(3),
        name="mla_v_up_proj_t",
    )(wt, x)


def _swa_attn_kernel(sink_ref, q_ref, ka_ref, kb_ref, vta_ref, vtb_ref, o_ref, ot_ref, *, tq, k_valid):
    t = pl.program_id(1)
    k = jnp.concatenate([ka_ref[0], kb_ref[0]], axis=0)
    vt = jnp.concatenate([vta_ref[0], vtb_ref[0]], axis=1)
    tk = k.shape[0]
    kp = lax.broadcasted_iota(jnp.int32, (tk, 1), 0)
    if k_valid is None:
        kp = kp + (t - 1) * tq
        kc = kp // CHUNK
        qc = (t * tq + lax.broadcasted_iota(jnp.int32, (1, tq), 1)) // CHUNK
        valid = jnp.logical_and(kp >= 0, jnp.logical_and(kc >= qc - WINDOW // CHUNK, kc <= qc))
    else:
        valid = jnp.broadcast_to(kp < k_valid, (tk, tq))
    valid = jnp.concatenate([valid] * SWA_GROUP, axis=1)
    lane_head = lax.broadcasted_iota(jnp.int32, (1, HEAD_SLAB), 1) // SWA_HEAD_DIM
    heads_per_slab = HEAD_SLAB // SWA_HEAD_DIM
    for slab in range(SWA_KV_DIM // HEAD_SLAB):
        k_slab = k[:, slab * HEAD_SLAB:(slab + 1) * HEAD_SLAB]
        qs = jnp.concatenate(
            [q_ref[0, :, g * SWA_KV_DIM + slab * HEAD_SLAB:g * SWA_KV_DIM + (slab + 1) * HEAD_SLAB]
             for g in range(SWA_GROUP)], axis=0)
        for hh in range(heads_per_slab):
            kvh = slab * heads_per_slab + hh
            km = k_slab * (lane_head == hh).astype(BF16)
            vth = vt[kvh * SWA_HEAD_DIM:(kvh + 1) * SWA_HEAD_DIM, :]
            sb = jnp.concatenate([jnp.full((1, tq), sink_ref[kvh, g], F32) for g in range(SWA_GROUP)],
                                 axis=1)
            s = lax.dot_general(km, qs, (((1,), (1,)), ((), ())), preferred_element_type=F32)
            s = jnp.where(valid, s, NEG_INF)
            m = jnp.maximum(jnp.max(s, axis=0, keepdims=True), sb)
            e = jnp.exp(s - m)
            inv = 1.0 / (jnp.sum(e, axis=0, keepdims=True) + jnp.exp(sb - m))
            o_t = jnp.dot(vth, (e * inv).astype(BF16), preferred_element_type=F32)
            for g in range(SWA_GROUP):
                row = g * SWA_KV_DIM + kvh * SWA_HEAD_DIM
                ot_ref[row:row + SWA_HEAD_DIM, :] = o_t[:, g * tq:(g + 1) * tq]
    for c in range(D_MODEL // LANES):
        o_ref[0, :, c * LANES:(c + 1) * LANES] = ot_ref[c * LANES:(c + 1) * LANES, :].T.astype(BF16)


def swa_attention(q, k, vt, sink, *, k_valid=None):
    bsz, sq, _ = q.shape
    tq = WINDOW
    if k_valid is None:
        a_idx = lambda t: jnp.maximum(t - 1, 0)
        b_idx = lambda t: t
    else:
        assert sq == tq and k.shape[1] == 2 * WINDOW
        a_idx = lambda t: 0
        b_idx = lambda t: 1
    return pl.pallas_call(
        functools.partial(_swa_attn_kernel, tq=tq, k_valid=k_valid),
        grid=(bsz, sq // tq),
        in_specs=[pl.BlockSpec(memory_space=pltpu.SMEM),
                  pl.BlockSpec((1, tq, D_MODEL), lambda b, t: (b, t, 0)),
                  pl.BlockSpec((1, WINDOW, SWA_KV_DIM), lambda b, t: (b, a_idx(t), 0)),
                  pl.BlockSpec((1, WINDOW, SWA_KV_DIM), lambda b, t: (b, b_idx(t), 0)),
                  pl.BlockSpec((1, SWA_KV_DIM, WINDOW), lambda b, t: (b, 0, a_idx(t))),
                  pl.BlockSpec((1, SWA_KV_DIM, WINDOW), lambda b, t: (b, 0, b_idx(t)))],
        out_specs=pl.BlockSpec((1, tq, D_MODEL), lambda b, t: (b, t, 0)),
        out_shape=jax.ShapeDtypeStruct((bsz, sq, D_MODEL), BF16),
        scratch_shapes=[pltpu.VMEM((D_MODEL, tq), F32)],
        compiler_params=_params(2),
        name="swa_sink_attention",
    )(sink, q, k, k, vt, vt)


def _cast_pad_cols_kernel(x_ref, o_ref):
    n = x_ref.shape[-1]
    o_ref[0, :, :n] = x_ref[0].astype(BF16)
    o_ref[0, :, n:] = jnp.zeros((o_ref.shape[1], o_ref.shape[2] - n), BF16)


def cast_pad_cols(x, n_pad, tr=256):
    nl, r, c = x.shape
    return pl.pallas_call(
        _cast_pad_cols_kernel,
        grid=(nl, r // tr),
        in_specs=[pl.BlockSpec((1, tr, c), lambda l, i: (l, i, 0))],
        out_specs=pl.BlockSpec((1, tr, n_pad), lambda l, i: (l, i, 0)),
        out_shape=jax.ShapeDtypeStruct((nl, r, n_pad), BF16),
        compiler_params=_params(2),
        name="cast_pad_cols",
    )(x)


def _cast_pad_rows_kernel(x_ref, o_ref, *, n_blocks):
    i = pl.program_id(1)

    @pl.when(i < n_blocks)
    def _():
        o_ref[0] = x_ref[0].astype(BF16)

    @pl.when(i >= n_blocks)
    def _():
        o_ref[0] = jnp.zeros(o_ref.shape[1:], BF16)


def cast_pad_rows(x, r_pad, tr=256):
    nl, r, c = x.shape
    n_blocks = r // tr
    return pl.pallas_call(
        functools.partial(_cast_pad_rows_kernel, n_blocks=n_blocks),
        grid=(nl, r_pad // tr),
        in_specs=[pl.BlockSpec((1, tr, c), lambda l, i: (l, jnp.minimum(i, n_blocks - 1), 0))],
        out_specs=pl.BlockSpec((1, tr, c), lambda l, i: (l, i, 0)),
        out_shape=jax.ShapeDtypeStruct((nl, r_pad, c), BF16),
        compiler_params=_params(2),
        name="cast_pad_rows",
    )(x)


def _rope_slab_cols(w_rope):
    half = QK_ROPE // 2
    x1, x2 = w_rope[..., :half], w_rope[..., half:]
    return jnp.concatenate([x1, x2, x2, x1], axis=-1)


def _mla_tables(pos):
    half = QK_ROPE // 2
    inv = jnp.power(jnp.float32(ROPE_THETA), -jnp.arange(half, dtype=F32) * (2.0 / QK_ROPE))
    ang = pos.astype(F32)[:, None] * inv[None, :]
    c, s = jnp.cos(ang), jnp.sin(ang)
    z = jnp.zeros_like(c)
    return jnp.concatenate([c, c, z, z], axis=1), jnp.concatenate([-s, s, z, z], axis=1)


def _swa_tables(pos):
    half = ROPE_DIM_B // 2
    inv = jnp.power(jnp.float32(ROPE_THETA), -jnp.arange(half, dtype=F32) * (2.0 / ROPE_DIM_B))
    ang = pos.astype(F32)[:, None] * inv[None, :]
    c, s = jnp.cos(ang), jnp.sin(ang)
    n = pos.shape[0]
    rest = SWA_HEAD_DIM - ROPE_DIM_B
    c64 = jnp.concatenate([c, c, jnp.ones((n, rest), F32)], axis=1)
    s1 = jnp.concatenate([-s, jnp.zeros((n, SWA_HEAD_DIM - half), F32)], axis=1)
    s2 = jnp.concatenate([jnp.zeros((n, half), F32), s, jnp.zeros((n, rest), F32)], axis=1)
    rep = LANES // SWA_HEAD_DIM
    return jnp.tile(c64, (1, rep)), jnp.tile(s1, (1, rep)), jnp.tile(s2, (1, rep))


def _prep_weights(p):
    w = {}
    wa = p['mla_w_a'].astype(BF16)
    w['mla_w_a'] = jnp.concatenate(
        [wa[..., :Q_LORA + KV_LORA], _rope_slab_cols(wa[..., Q_LORA + KV_LORA:])], axis=-1)
    wuq = p['mla_w_uq'].astype(BF16).reshape(N_A, Q_LORA, MLA_HEADS, QK_NOPE + QK_ROPE)
    w['mla_w_uq'] = jnp.concatenate(
        [wuq[..., :QK_NOPE], _rope_slab_cols(wuq[..., QK_NOPE:])], axis=-1
    ).reshape(N_A, Q_LORA, MLA_HEADS * MLA_QDIM)
    w['mla_w_uk'] = p['mla_w_uk'].reshape(N_A, KV_LORA, MLA_HEADS * QK_NOPE).astype(BF16)
    w['mla_w_uvt'] = p['mla_w_uv'].reshape(N_A, KV_LORA, MLA_HEADS * V_DIM).transpose(0, 2, 1).astype(BF16)
    w['mla_w_o'] = p['mla_w_o'].astype(BF16)
    w['swa_w_kv'] = p['swa_w_kv'].astype(BF16)
    wq = p['swa_w_q'].reshape(N_B, D_MODEL, SWA_KV_HEADS, SWA_GROUP, SWA_HEAD_DIM)
    w['swa_w_q'] = wq.transpose(0, 1, 3, 2, 4).reshape(N_B, D_MODEL, D_MODEL).astype(BF16)
    wo = p['swa_w_o'].reshape(N_B, SWA_KV_HEADS, SWA_GROUP, SWA_HEAD_DIM, D_MODEL)
    w['swa_w_o'] = wo.transpose(0, 2, 1, 3, 4).reshape(N_B, D_MODEL, D_MODEL).astype(BF16)
    w['swa_sinks'] = p['swa_sinks'].reshape(N_B, SWA_KV_HEADS, SWA_GROUP)
    fpad = F_PAD - D_FF
    w['ffn_w_gate'] = cast_pad_cols(p['ffn_w_gate'], F_PAD)
    w['ffn_w_up'] = cast_pad_cols(p['ffn_w_up'], F_PAD)
    w['ffn_w_down'] = cast_pad_rows(p['ffn_w_down'], F_PAD)
    w['ffn_conv_w'] = jnp.pad(p['ffn_conv_w'], ((0, 0), (0, 0), (0, fpad)))
    w['ffn_conv_b'] = jnp.pad(p['ffn_conv_b'], ((0, 0), (0, fpad)))
    return w


def _trunk(x, pos, p, w, cache):
    bsz, seq, _ = x.shape
    m = bsz * seq
    h = x.reshape(m, D_MODEL)
    tm = _tile(m, 512)
    tm_l = _tile(m, 1024)
    past = 0 if cache is None else cache['ckv'].shape[2]

    mla_cos, mla_sin = (jnp.tile(t, (bsz, 1)) for t in _mla_tables(pos))
    swa_c, swa_s1, swa_s2 = (jnp.tile(t, (bsz, 1)) for t in _swa_tables(pos))
    tab = _row_spec(tm, LANES)
    tab_l = _row_spec(tm_l, LANES)

    def out_proj(o, w_o, layer, h_res):
        return matmul(o, w_o, layer=layer, tm=tm_l, tn=1024, epilogue=_epi_residual,
                      extras=(h_res,), extra_specs=(_tile_spec(tm_l, 1024),),
                      out_shapes=jax.ShapeDtypeStruct((m, D_MODEL), F32),
                      out_specs=_tile_spec(tm_l, 1024), name="out_proj_residual")

    ckv_rows, krope_rows, conv_rows = [], [], []
    ks = vs = ks_b = vs_t = None
    for l in range(DEPTH):
        if l < N_A:
            n_a = w['mla_w_a'].shape[2]
            cq, ckv, ckv_b, krope, kr_b = matmul(
                h, w['mla_w_a'], layer=l, tm=tm, tn=n_a, epilogue=_epi_mla_in, norm_x=True,
                extras=(p['mla_g_q'][l].reshape(1, Q_LORA), p['mla_g_kv'][l].reshape(1, KV_LORA),
                        mla_cos, mla_sin, p['norm_attn'][l].reshape(1, D_MODEL)),
                extra_specs=(_const_spec(Q_LORA), _const_spec(KV_LORA), tab, tab,
                             _const_spec(D_MODEL)),
                out_shapes=[jax.ShapeDtypeStruct((m, Q_LORA), BF16),
                            jax.ShapeDtypeStruct((m, KV_LORA), F32),
                            jax.ShapeDtypeStruct((m, KV_LORA), BF16),
                            jax.ShapeDtypeStruct((m, QK_ROPE), F32),
                            jax.ShapeDtypeStruct((m, LANES), BF16)],
                out_specs=[_row_spec(tm, Q_LORA), _row_spec(tm, KV_LORA), _row_spec(tm, KV_LORA),
                           _row_spec(tm, QK_ROPE), _row_spec(tm, LANES)],
                name="mla_in_proj")
            ckv_rows.append(ckv.reshape(bsz, seq, KV_LORA))
            krope_rows.append(krope.reshape(bsz, seq, QK_ROPE))
            tn_q = 8 * MLA_QDIM
            q = matmul(cq, w['mla_w_uq'], layer=l, tm=tm_l, tn=tn_q, epilogue=_epi_mla_q,
                       extras=(mla_cos, mla_sin), extra_specs=(tab_l, tab_l),
                       out_shapes=jax.ShapeDtypeStruct((m, MLA_HEADS * MLA_QDIM), BF16),
                       out_specs=_tile_spec(tm_l, tn_q), name="mla_q_proj")
            if cache is None:
                ckv_all = ckv_b.reshape(bsz, seq, KV_LORA)
                kr_all = kr_b.reshape(bsz, seq, LANES)
                k_valid = seq
                tq = tk = min(seq, 1024)
            else:
                k_valid = past + seq
                sk = -(-k_valid // LANES) * LANES
                ckv_all = jnp.concatenate(
                    [cache['ckv'][l].astype(BF16), ckv_b.reshape(bsz, seq, KV_LORA),
                     jnp.zeros((bsz, sk - k_valid, KV_LORA), BF16)], axis=1)
                kr_cache = jnp.pad(cache['krope'][l], ((0, 0), (0, 0), (0, LANES - QK_ROPE)))
                kr_all = jnp.concatenate(
                    [kr_cache.astype(BF16), kr_b.reshape(bsz, seq, LANES),
                     jnp.zeros((bsz, sk - k_valid, LANES), BF16)], axis=1)
                tq, tk = LANES, sk
            sk = ckv_all.shape[1]
            mk = bsz * sk
            tmk = _tile(mk, 1024)
            kn = matmul(ckv_all.reshape(mk, KV_LORA), w['mla_w_uk'], layer=l, tm=tmk, tn=2048,
                        epilogue=_epi_cast,
                        out_shapes=jax.ShapeDtypeStruct((mk, MLA_HEADS * QK_NOPE), BF16),
                        out_specs=_tile_spec(tmk, 2048), name="mla_k_up_proj")
            if sk % 512 == 0:
                vt = matmul_nt(w['mla_w_uvt'], l, ckv_all, tn=MLA_HEADS * V_DIM, ts=512)
            else:
                vt = matmul_nt(w['mla_w_uvt'], l, ckv_all, tn=1024, ts=sk)
            q3 = q.reshape(bsz, seq, MLA_HEADS * MLA_QDIM)
            if seq < tq:
                q3 = jnp.pad(q3, ((0, 0), (0, tq - seq), (0, 0)))
            o = mla_attention(q3, kn.reshape(bsz, sk, MLA_HEADS * QK_NOPE), kr_all, vt,
                              q_off=past, k_valid=k_valid, tq=tq, tk=tk)
            h = out_proj(o[:, :seq].reshape(m, D_MODEL), w['mla_w_o'], l, h)
        else:
            i = l - N_A
            if l == N_A:
                ks, vs, ks_b, vs_t = matmul(
                    h, w['swa_w_kv'], tm=tm, tn=2 * SWA_KV_DIM, epilogue=_epi_shared_kv,
                    norm_x=True,
                    extras=(swa_c, swa_s1, swa_s2, p['kv_shared_norm'].reshape(1, D_MODEL)),
                    extra_specs=(tab, tab, tab, _const_spec(D_MODEL)),
                    out_shapes=[jax.ShapeDtypeStruct((m, SWA_KV_DIM), F32)] * 2
                    + [jax.ShapeDtypeStruct((m, SWA_KV_DIM), BF16),
                       jax.ShapeDtypeStruct((SWA_KV_DIM, m), BF16)],
                    out_specs=[_row_spec(tm, SWA_KV_DIM)] * 3
                    + [pl.BlockSpec((SWA_KV_DIM, tm), lambda i, j: (0, i))],
                    name="swa_shared_kv_proj")
            hn = rmsnorm(h, p['norm_attn'][l], BF16)
            q = matmul(hn, w['swa_w_q'], layer=i, tm=tm_l, tn=1024, epilogue=_epi_swa_q,
                       extras=(swa_c, swa_s1, swa_s2), extra_specs=(tab_l, tab_l, tab_l),
                       out_shapes=jax.ShapeDtypeStruct((m, D_MODEL), BF16),
                       out_specs=_tile_spec(tm_l, 1024), name="swa_q_proj")
            q3 = q.reshape(bsz, seq, D_MODEL)
            k3 = ks_b.reshape(bsz, seq, SWA_KV_DIM)
            vt3 = vs_t.reshape(SWA_KV_DIM, bsz, seq).transpose(1, 0, 2)
            sink = w['swa_sinks'][i]
            if cache is None:
                o = swa_attention(q3, k3, vt3, sink)
            else:
                win = cache['swa_k'].shape[1]
                npad = 2 * WINDOW - win - seq
                k_all = jnp.concatenate(
                    [cache['swa_k'].reshape(bsz, win, SWA_KV_DIM).astype(BF16), k3,
                     jnp.zeros((bsz, npad, SWA_KV_DIM), BF16)], axis=1)
                vt_cache = cache['swa_v'].reshape(bsz, win, SWA_KV_DIM).transpose(0, 2, 1)
                vt_all = jnp.concatenate(
                    [vt_cache.astype(BF16), vt3, jnp.zeros((bsz, SWA_KV_DIM, npad), BF16)], axis=2)
                q3 = jnp.pad(q3, ((0, 0), (0, WINDOW - seq), (0, 0)))
                o = swa_attention(q3, k_all, vt_all, sink, k_valid=win + seq)[:, :seq]
            h = out_proj(o.reshape(m, D_MODEL), w['swa_w_o'], i, h)

        hn = rmsnorm(h, p['norm_ffn'][l], BF16)
        if cache is None:
            prev = jnp.zeros((bsz, SUBLANES, F_PAD), F32)
        else:
            prev = jnp.pad(cache['conv'][l],
                           ((0, 0), (SUBLANES - (CONV_W - 1), 0), (0, F_PAD - D_FF)))
        act, gl = ffn_a(hn, w['ffn_w_gate'], w['ffn_w_up'], l, w['ffn_conv_w'][l],
                        w['ffn_conv_b'][l], prev, seq)
        gl = gl.reshape(bsz, -1, SUBLANES, F_PAD)
        conv_rows.append(gl[:, -1, SUBLANES - (CONV_W - 1):, :D_FF])
        h = matmul(act, w['ffn_w_down'], layer=l, tm=tm, tn=512, epilogue=_epi_residual,
                   extras=(h,), extra_specs=(_tile_spec(tm, 512),),
                   out_shapes=jax.ShapeDtypeStruct((m, D_MODEL), F32),
                   out_specs=_tile_spec(tm, 512), name="ffn_down_residual")

    y = rmsnorm(h, p['norm_final'], F32).reshape(bsz, seq, D_MODEL)
    win = min(WINDOW, seq) if cache is None else seq
    ks4 = ks.reshape(bsz, seq, SWA_KV_DIM)[:, seq - win:].reshape(bsz, win, SWA_KV_HEADS, SWA_HEAD_DIM)
    vs4 = vs.reshape(bsz, seq, SWA_KV_DIM)[:, seq - win:].reshape(bsz, win, SWA_KV_HEADS, SWA_HEAD_DIM)
    return y, jnp.stack(ckv_rows), jnp.stack(krope_rows), ks4, vs4, jnp.stack(conv_rows)


def kernel(x_prompt, x_sample, cache_mla_ckv, cache_mla_krope, cache_swa_k, cache_swa_v, state_conv,
           norm_attn, norm_ffn, mla_w_a, mla_g_q, mla_g_kv, mla_w_uq, mla_w_uk, mla_w_uv, mla_w_o,
           kv_shared_norm, swa_w_kv, swa_w_q, swa_sinks, swa_w_o,
           ffn_w_gate, ffn_w_up, ffn_conv_w, ffn_conv_b, ffn_w_down, norm_final):
    p = {
        'norm_attn': norm_attn, 'norm_ffn': norm_ffn,
        'mla_w_a': mla_w_a, 'mla_g_q': mla_g_q, 'mla_g_kv': mla_g_kv, 'mla_w_uq': mla_w_uq,
        'mla_w_uk': mla_w_uk, 'mla_w_uv': mla_w_uv, 'mla_w_o': mla_w_o,
        'kv_shared_norm': kv_shared_norm, 'swa_w_kv': swa_w_kv, 'swa_w_q': swa_w_q,
        'swa_sinks': swa_sinks, 'swa_w_o': swa_w_o,
        'ffn_w_gate': ffn_w_gate, 'ffn_w_up': ffn_w_up, 'ffn_conv_w': ffn_conv_w,
        'ffn_conv_b': ffn_conv_b, 'ffn_w_down': ffn_w_down, 'norm_final': norm_final,
    }
    w = _prep_weights(p)
    pos_p = jnp.arange(x_prompt.shape[1])
    out_p = _trunk(x_prompt, pos_p, p, w, None)
    past = cache_mla_ckv.shape[2]
    pos_s = past + jnp.arange(x_sample.shape[1])
    cache = {'ckv': cache_mla_ckv, 'krope': cache_mla_krope, 'swa_k': cache_swa_k,
             'swa_v': cache_swa_v, 'conv': state_conv}
    out_s = _trunk(x_sample, pos_s, p, w, cache)
    return (out_p[0], out_s[0]) + out_p[1:] + out_s[1:]
```

```python
import functools

import jax
import jax.numpy as jnp
from jax import lax
from jax.experimental import pallas as pl
from jax.experimental.pallas import tpu as pltpu

D_MODEL = 4096
DEPTH = 4
CHUNK = 64
N_A = DEPTH // 2
N_B = DEPTH - N_A
ROPE_THETA = 500000.0
EPS = 1e-6
NEG_INF = -1e30
MLA_HEADS = D_MODEL // 128
Q_LORA = D_MODEL // 4
KV_LORA = 512
QK_NOPE = 128
QK_ROPE = 64
V_DIM = 128
MLA_SCALE = (QK_NOPE + QK_ROPE) ** -0.5
LOG2E = 1.4426950408889634
MLA_Q_SCALE = MLA_SCALE * LOG2E
SWA_HEAD_DIM = 64
SWA_HEADS = D_MODEL // SWA_HEAD_DIM
SWA_KV_HEADS = 8
SWA_GROUP = SWA_HEADS // SWA_KV_HEADS
WINDOW = 128
ROPE_DIM_B = SWA_HEAD_DIM // 4
SWA_SCALE = SWA_HEAD_DIM ** -0.5
D_FF = 256 * ((8 * D_MODEL // 3 + 255) // 256)
CONV_W = 3
FFN_TF = 512
F_PAD = -(-D_FF // FFN_TF) * FFN_TF

LANES = 128
SUBLANES = 8
V7X_VMEM_LIMIT = 56 * 1024 * 1024

MLA_QDIM = 2 * LANES
DENOM_ROWS = 16
SWA_KV_DIM = SWA_KV_HEADS * SWA_HEAD_DIM
HEAD_SLAB = 2 * LANES
BF16 = jnp.bfloat16
F32 = jnp.float32


def _params(n_grid):
    return pltpu.CompilerParams(dimension_semantics=("arbitrary",) * n_grid,
                                vmem_limit_bytes=V7X_VMEM_LIMIT)


def _tile(m, cap):
    for step in (LANES, 16):
        t = (min(cap, m) // step) * step
        while t >= step:
            if m % t == 0:
                return t
            t -= step
    return m


def _rms(x, g):
    return x * lax.rsqrt(jnp.mean(x * x, axis=-1, keepdims=True) + EPS) * g


def _rmsnorm_kernel(x_ref, g_ref, o_ref):
    o_ref[...] = _rms(x_ref[...], g_ref[...]).astype(o_ref.dtype)


def rmsnorm(x, g, out_dtype):
    m, d = x.shape
    tm = _tile(m, 512)
    return pl.pallas_call(
        _rmsnorm_kernel,
        grid=(m // tm,),
        in_specs=[pl.BlockSpec((tm, d), lambda i: (i, 0)),
                  pl.BlockSpec((1, d), lambda i: (0, 0))],
        out_specs=pl.BlockSpec((tm, d), lambda i: (i, 0)),
        out_shape=jax.ShapeDtypeStruct((m, d), out_dtype),
        compiler_params=_params(1),
        name="rmsnorm",
    )(x, g.reshape(1, d))


def _mm_kernel(x_ref, w_ref, *rest, epilogue, n_extra, norm_x):
    extras = rest[:n_extra]
    if norm_x:
        x = _rms(x_ref[...], extras[-1][...]).astype(BF16)
    else:
        x = x_ref[...]
    y = jnp.dot(x, w_ref[...], preferred_element_type=F32)
    epilogue(y, extras, rest[n_extra:])


def matmul(x, w, *, tm, tn, epilogue, extras=(), extra_specs=(), out_shapes, out_specs, name,
           norm_x=False, layer=None):
    m, k = x.shape
    n = w.shape[-1]
    assert not norm_x or tn == n
    w_mode = dict(pipeline_mode=pl.Buffered(1)) if tn == n else {}
    if layer is None:
        w_spec = pl.BlockSpec((k, tn), lambda i, j: (0, j), **w_mode)
    else:
        w_spec = pl.BlockSpec((None, k, tn), lambda i, j: (layer, 0, j), **w_mode)
    return pl.pallas_call(
        functools.partial(_mm_kernel, epilogue=epilogue, n_extra=len(extras), norm_x=norm_x),
        grid=(m // tm, n // tn),
        in_specs=[pl.BlockSpec((tm, k), lambda i, j: (i, 0)), w_spec] + list(extra_specs),
        out_specs=out_specs,
        out_shape=out_shapes,
        compiler_params=_params(2),
        name=name,
    )(x, w, *extras)


def _epi_cast(y, extras, outs):
    outs[0][...] = y.astype(outs[0].dtype)


def _epi_residual(y, extras, outs):
    outs[0][...] = extras[0][...] + y


def _rope_slab(a, cos2, sin2):
    return a * cos2 + pltpu.roll(a, QK_ROPE, axis=1) * sin2


def _epi_mla_in(y, extras, outs):
    gq_ref, gkv_ref, cos_ref, sin_ref = extras[:4]
    cq_ref, ckv_ref, ckvb_ref, kr_ref, krb_ref = outs
    cq_ref[...] = _rms(y[:, :Q_LORA], gq_ref[...]).astype(BF16)
    ckv = _rms(y[:, Q_LORA:Q_LORA + KV_LORA], gkv_ref[...])
    ckv_ref[...] = ckv
    ckvb_ref[...] = ckv.astype(BF16)
    kr = _rope_slab(y[:, Q_LORA + KV_LORA:], cos_ref[...], sin_ref[...])
    kr_ref[...] = kr[:, :QK_ROPE]
    krb_ref[...] = kr.astype(BF16)


def _epi_mla_q(y, extras, outs):
    cos_ref, sin_ref = extras
    cos2 = cos_ref[...] * MLA_Q_SCALE
    sin2 = sin_ref[...] * MLA_Q_SCALE
    for s in range(y.shape[1] // MLA_QDIM):
        lo = s * MLA_QDIM
        outs[0][:, lo:lo + LANES] = (y[:, lo:lo + LANES] * MLA_Q_SCALE).astype(BF16)
        outs[0][:, lo + LANES:lo + MLA_QDIM] = _rope_slab(
            y[:, lo + LANES:lo + MLA_QDIM], cos2, sin2).astype(BF16)


def _rope_b(y, c, s1, s2, scale):
    half = ROPE_DIM_B // 2
    for s in range(y.shape[1] // LANES):
        a = y[:, s * LANES:(s + 1) * LANES]
        r = a * c + pltpu.roll(a, LANES - half, axis=1) * s1 + pltpu.roll(a, half, axis=1) * s2
        yield s, (r * scale if scale != 1.0 else r)


def _epi_swa_q(y, extras, outs):
    c_ref, s1_ref, s2_ref = extras
    for s, r in _rope_b(y, c_ref[...], s1_ref[...], s2_ref[...], SWA_SCALE):
        outs[0][:, s * LANES:(s + 1) * LANES] = r.astype(BF16)


def _epi_shared_kv(y, extras, outs):
    c_ref, s1_ref, s2_ref = extras[:3]
    k_ref, v_ref, kb_ref, vtb_ref = outs
    for s, r in _rope_b(y[:, :SWA_KV_DIM], c_ref[...], s1_ref[...], s2_ref[...], 1.0):
        k_ref[:, s * LANES:(s + 1) * LANES] = r
        kb_ref[:, s * LANES:(s + 1) * LANES] = r.astype(BF16)
    v = y[:, SWA_KV_DIM:]
    v_ref[...] = v
    vtb_ref[...] = v.T.astype(BF16)


def _row_spec(tm, width):
    return pl.BlockSpec((tm, width), lambda i, j: (i, 0))


def _tile_spec(tm, tn):
    return pl.BlockSpec((tm, tn), lambda i, j: (i, j))


def _const_spec(width):
    return pl.BlockSpec((1, width), lambda i, j: (0, 0))


def _ffn_a_kernel(x_ref, wg_ref, wu_ref, cw_ref, cb_ref, prev_ref, act_ref, gl_ref, gbuf, carry,
                  *, seg, nseg, tiles_per_seq, sub, tail_chunks):
    i = pl.program_id(0)
    j = pl.program_id(1)
    if tiles_per_seq > 1:
        @pl.when(jnp.logical_and(i == 0, j == 0))
        def _():
            carry[...] = jnp.zeros(carry.shape, F32)

        gbuf[0:SUBLANES] = jnp.where(i % tiles_per_seq == 0, prev_ref[0], carry[j])
    else:
        for s in range(nseg):
            gbuf[s * SUBLANES:(s + 1) * SUBLANES] = prev_ref[s]
    x = x_ref[...]
    n_sub = act_ref.shape[1] // sub
    row = lax.broadcasted_iota(jnp.int32, (SUBLANES, sub), 0)
    for c in range(n_sub):
        cols = slice(c * sub, (c + 1) * sub)
        w0 = cw_ref[0:1, cols]
        w1 = cw_ref[1:2, cols]
        w2 = cw_ref[2:3, cols]
        b = cb_ref[:, cols]
        n_chunk = tail_chunks if (c == n_sub - 1 and nseg == 1) else 1
        rows = seg // n_chunk
        if n_chunk == 1:
            g = jnp.dot(x, wg_ref[:, cols], preferred_element_type=F32)
            u = jnp.dot(x, wu_ref[:, cols], preferred_element_type=F32)
        for s in range(nseg):
            halo = gbuf[s * SUBLANES:(s + 1) * SUBLANES, cols]
            for k in range(n_chunk):
                r0 = s * seg + k * rows
                if n_chunk == 1:
                    gs, us = g[r0:r0 + rows], u[r0:r0 + rows]
                else:
                    gs = jnp.dot(x[r0:r0 + rows], wg_ref[:, cols], preferred_element_type=F32)
                    us = jnp.dot(x[r0:r0 + rows], wu_ref[:, cols], preferred_element_type=F32)
                r1 = pltpu.roll(gs, 1, axis=0)
                r2 = pltpu.roll(gs, 2, axis=0)
                h1 = jnp.where(row == 0, halo[SUBLANES - 1:SUBLANES], r1[:SUBLANES])
                h2 = jnp.where(row == 0, halo[SUBLANES - 2:SUBLANES - 1],
                               jnp.where(row == 1, halo[SUBLANES - 1:SUBLANES], r2[:SUBLANES]))
                g1 = jnp.concatenate([h1, r1[SUBLANES:]], axis=0)
                g2 = jnp.concatenate([h2, r2[SUBLANES:]], axis=0)
                gc = b + w0 * g2 + w1 * g1 + w2 * gs
                a = gc * jax.nn.sigmoid(gc) * us
                act_ref[r0:r0 + rows, cols] = a.astype(BF16)
                halo = gs[rows - SUBLANES:]
            gl_ref[s, :, cols] = halo
        if tiles_per_seq > 1:
            carry[j, :, cols] = halo


def ffn_a(hn, wg, wu, layer, cw, cb, prev, seq_len, *, tm_cap=1024, tf=FFN_TF, sub=2 * LANES):
    m, d = hn.shape
    f = wg.shape[2]
    tm = _tile(m, tm_cap)
    seg = min(seq_len, tm)
    nseg = tm // seg
    tiles_per_seq = seq_len // seg
    nj = f // tf
    if nseg == 1:
        prev_map = lambda i, j: (i // tiles_per_seq, 0, j)
    else:
        prev_map = lambda i, j: (i, 0, j)
    return pl.pallas_call(
        functools.partial(_ffn_a_kernel, seg=seg, nseg=nseg, tiles_per_seq=tiles_per_seq, sub=sub,
                          tail_chunks=4),
        grid=(m // tm, nj),
        in_specs=[pl.BlockSpec((tm, d), lambda i, j: (i, 0)),
                  pl.BlockSpec((None, d, tf), lambda i, j: (layer, 0, j)),
                  pl.BlockSpec((None, d, tf), lambda i, j: (layer, 0, j)),
                  pl.BlockSpec((CONV_W, tf), lambda i, j: (0, j)),
                  pl.BlockSpec((1, tf), lambda i, j: (0, j)),
                  pl.BlockSpec((nseg, SUBLANES, tf), prev_map)],
        out_specs=[pl.BlockSpec((tm, tf), lambda i, j: (i, j)),
                   pl.BlockSpec((nseg, SUBLANES, tf), lambda i, j: (i, 0, j))],
        out_shape=[jax.ShapeDtypeStruct((m, f), BF16),
                   jax.ShapeDtypeStruct((m // seg, SUBLANES, f), F32)],
        scratch_shapes=[pltpu.VMEM((nseg * SUBLANES, tf), F32),
                        pltpu.VMEM((nj, SUBLANES, tf), F32)],
        compiler_params=_params(2),
        name="ffn_gate_up_conv",
    )(hn, wg, wu, cw, cb.reshape(1, f), prev)


def _tile_pair(a, u, nq, paired):
    if not paired:
        return a, u
    first = u <= a
    return jnp.where(first, a, nq - 1 - a), jnp.where(first, u, u - a - 1)


def _mla_attn_kernel(q_ref, kn_ref, kr_ref, vt_ref, o_ref, m_ref, acc_ref, s_ref, p_ref,
                     *, hb, tq, tk, nq, nk, q_off, k_valid, cg, rb, paired):
    qi, ki = _tile_pair(pl.program_id(2), pl.program_id(3), nq, paired)
    q_lo = q_off + qi * tq
    k_lo = ki * tk
    k_last = jnp.minimum(((q_lo + tq - 1) // CHUNK * CHUNK + CHUNK - 1) // tk, nk - 1)

    @pl.when(ki == 0)
    def _():
        m_ref[...] = jnp.full(m_ref.shape, NEG_INF, F32)
        acc_ref[...] = jnp.zeros(acc_ref.shape, F32)

    needed = k_lo // CHUNK <= (q_lo + tq - 1) // CHUNK
    full = jnp.logical_and((k_lo + tk - 1) // CHUNK <= q_lo // CHUNK, k_lo + tk <= k_valid)

    def step(masked):
        kr = kr_ref[0]
        ones = jnp.ones((DENOM_ROWS, tk), BF16)
        diag = masked and paired
        half = tk // 2
        nt = (((1,), (1,)), ((), ()))
        for h in range(hb):
            buf = h % s_ref.shape[0]
            q = q_ref[0, :, h * MLA_QDIM:(h + 1) * MLA_QDIM]
            k = jnp.concatenate([kn_ref[0, :, h * QK_NOPE:(h + 1) * QK_NOPE], kr], axis=1)
            if diag:
                s_ref[buf, :half, :] = lax.dot_general(k[:half], q, nt, preferred_element_type=F32)
                s_ref[buf, half:, half:] = lax.dot_general(k[half:], q[half:], nt,
                                                           preferred_element_type=F32)
            else:
                s_ref[buf] = lax.dot_general(k, q, nt, preferred_element_type=F32)
            alphas = []
            for c in range(tq // cg):
                cols = slice(c * cg, (c + 1) * cg)
                n_blk = (half if diag and (c + 1) * cg <= half else tk) // rb
                if masked:
                    qc = (q_lo + c * cg + lax.broadcasted_iota(jnp.int32, (1, cg), 1)) // CHUNK

                def scores(r):
                    s = s_ref[buf, r * rb:(r + 1) * rb, cols]
                    if masked:
                        kp = k_lo + r * rb + lax.broadcasted_iota(jnp.int32, (rb, 1), 0)
                        s = jnp.where(jnp.logical_and(qc >= kp // CHUNK, kp < k_valid), s, NEG_INF)
                    return s

                part = None
                for r in range(n_blk):
                    blk = jnp.max(scores(r).reshape(rb // SUBLANES, SUBLANES, cg), axis=0)
                    part = blk if part is None else jnp.maximum(part, blk)
                m_prev = m_ref[h, :, cols]
                m_new = jnp.maximum(m_prev, jnp.max(part, axis=0, keepdims=True))
                alphas.append(jnp.exp2(m_prev - m_new))
                m_ref[h, :, cols] = m_new
                for r in range(n_blk):
                    p_ref[buf, r * rb:(r + 1) * rb, cols] = jnp.exp2(scores(r) - m_new).astype(BF16)
            vt = jnp.concatenate([vt_ref[0, h * V_DIM:(h + 1) * V_DIM, :], ones], axis=0)
            if diag:
                pv = jnp.concatenate(
                    [jnp.dot(vt[:, :half], p_ref[buf, :half, :half], preferred_element_type=F32),
                     jnp.dot(vt, p_ref[buf, :, half:], preferred_element_type=F32)], axis=1)
            else:
                pv = jnp.dot(vt, p_ref[buf], preferred_element_type=F32)
            acc_ref[h] = jnp.concatenate(alphas, axis=1) * acc_ref[h] + pv

    @pl.when(jnp.logical_and(needed, full))
    def _():
        step(False)

    @pl.when(jnp.logical_and(needed, jnp.logical_not(full)))
    def _():
        step(True)

    @pl.when(ki == k_last)
    def _():
        for h in range(hb):
            o_t = acc_ref[h, :V_DIM, :] / acc_ref[h, V_DIM:V_DIM + 1, :]
            o_ref[0, :, h * V_DIM:(h + 1) * V_DIM] = o_t.T.astype(BF16)


def mla_attention(q, kn, kr, vt, *, q_off, k_valid, tq, tk, hb=8, n_buf=2):
    bsz, sq, _ = q.shape
    sk = kn.shape[1]
    nq, nk = sq // tq, sk // tk
    n_hg = MLA_HEADS // hb

    paired = q_off == 0 and tq == tk and nq == nk and nq % 2 == 0

    def q_idx(a, u):
        return _tile_pair(a, u, nq, paired)[0]

    def k_idx(a, u):
        qi, ki = _tile_pair(a, u, nq, paired)
        last = ((q_off + (qi + 1) * tq - 1) // CHUNK * CHUNK + CHUNK - 1) // tk
        return jnp.minimum(ki, jnp.minimum(last, nk - 1))

    return pl.pallas_call(
        functools.partial(_mla_attn_kernel, hb=hb, tq=tq, tk=tk, nq=nq, nk=nk, q_off=q_off,
                          k_valid=k_valid, cg=min(tq, 2 * LANES), rb=LANES, paired=paired),
        grid=(bsz, n_hg, nq // 2, nq + 1) if paired else (bsz, n_hg, nq, nk),
        in_specs=[pl.BlockSpec((1, tq, hb * MLA_QDIM), lambda b, g, a, u: (b, q_idx(a, u), g)),
                  pl.BlockSpec((1, tk, hb * QK_NOPE), lambda b, g, a, u: (b, k_idx(a, u), g)),
                  pl.BlockSpec((1, tk, LANES), lambda b, g, a, u: (b, k_idx(a, u), 0)),
                  pl.BlockSpec((1, hb * V_DIM, tk), lambda b, g, a, u: (b, g, k_idx(a, u)))],
        out_specs=pl.BlockSpec((1, tq, hb * V_DIM), lambda b, g, a, u: (b, q_idx(a, u), g)),
        out_shape=jax.ShapeDtypeStruct((bsz, sq, MLA_HEADS * V_DIM), BF16),
        scratch_shapes=[pltpu.VMEM((hb, 1, tq), F32),
                        pltpu.VMEM((hb, V_DIM + DENOM_ROWS, tq), F32),
                        pltpu.VMEM((n_buf, tk, tq), F32),
                        pltpu.VMEM((n_buf, tk, tq), BF16)],
        compiler_params=_params(4),
        name="mla_flash_attention",
    )(q, kn, kr, vt)


def _nt_kernel(w_ref, x_ref, o_ref):
    o_ref[0] = lax.dot_general(w_ref[...], x_ref[0], (((1,), (1,)), ((), ())),
                               preferred_element_type=F32).astype(o_ref.dtype)


def matmul_nt(wt, layer, x, *, tn, ts):
    _, n, k = wt.shape
    bsz, s, _ = x.shape
    return pl.pallas_call(
        _nt_kernel,
        grid=(bsz, s // ts, n // tn),
        in_specs=[pl.BlockSpec((None, tn, k), lambda b, i, j: (layer, j, 0)),
                  pl.BlockSpec((1, ts, k), lambda b, i, j: (b, i, 0))],
        out_specs=pl.BlockSpec((1, tn, ts), lambda b, i, j: (b, j, i)),
        out_shape=jax.ShapeDtypeStruct((bsz, n, s), BF16),
        compiler_params=_params(3),
        name="mla_v_up_proj_t",
    )(wt, x)


def _swa_attn_kernel(sink_ref, q_ref, ka_ref, kb_ref, vta_ref, vtb_ref, o_ref, ot_ref, *, tq, k_valid):
    t = pl.program_id(1)
    k = jnp.concatenate([ka_ref[0], kb_ref[0]], axis=0)
    vt = jnp.concatenate([vta_ref[0], vtb_ref[0]], axis=1)
    tk = k.shape[0]
    kp = lax.broadcasted_iota(jnp.int32, (tk, 1), 0)
    if k_valid is None:
        kp = kp + (t - 1) * tq
        kc = kp // CHUNK
        qc = (t * tq + lax.broadcasted_iota(jnp.int32, (1, tq), 1)) // CHUNK
        valid = jnp.logical_and(kp >= 0, jnp.logical_and(kc >= qc - WINDOW // CHUNK, kc <= qc))
    else:
        valid = jnp.broadcast_to(kp < k_valid, (tk, tq))
    valid = jnp.concatenate([valid] * SWA_GROUP, axis=1)
    lane_head = lax.broadcasted_iota(jnp.int32, (1, HEAD_SLAB), 1) // SWA_HEAD_DIM
    heads_per_slab = HEAD_SLAB // SWA_HEAD_DIM
    for slab in range(SWA_KV_DIM // HEAD_SLAB):
        k_slab = k[:, slab * HEAD_SLAB:(slab + 1) * HEAD_SLAB]
        qs = jnp.concatenate(
            [q_ref[0, :, g * SWA_KV_DIM + slab * HEAD_SLAB:g * SWA_KV_DIM + (slab + 1) * HEAD_SLAB]
             for g in range(SWA_GROUP)], axis=0)
        for hh in range(heads_per_slab):
            kvh = slab * heads_per_slab + hh
            km = k_slab * (lane_head == hh).astype(BF16)
            vth = vt[kvh * SWA_HEAD_DIM:(kvh + 1) * SWA_HEAD_DIM, :]
            sb = jnp.concatenate([jnp.full((1, tq), sink_ref[kvh, g], F32) for g in range(SWA_GROUP)],
                                 axis=1)
            s = lax.dot_general(km, qs, (((1,), (1,)), ((), ())), preferred_element_type=F32)
            s = jnp.where(valid, s, NEG_INF)
            m = jnp.maximum(jnp.max(s, axis=0, keepdims=True), sb)
            e = jnp.exp(s - m)
            inv = 1.0 / (jnp.sum(e, axis=0, keepdims=True) + jnp.exp(sb - m))
            o_t = jnp.dot(vth, (e * inv).astype(BF16), preferred_element_type=F32)
            for g in range(SWA_GROUP):
                row = g * SWA_KV_DIM + kvh * SWA_HEAD_DIM
                ot_ref[row:row + SWA_HEAD_DIM, :] = o_t[:, g * tq:(g + 1) * tq]
    for c in range(D_MODEL // LANES):
        o_ref[0, :, c * LANES:(c + 1) * LANES] = ot_ref[c * LANES:(c + 1) * LANES, :].T.astype(BF16)


def swa_attention(q, k, vt, sink, *, k_valid=None):
    bsz, sq, _ = q.shape
    tq = WINDOW
    if k_valid is None:
        a_idx = lambda t: jnp.maximum(t - 1, 0)
        b_idx = lambda t: t
    else:
        assert sq == tq and k.shape[1] == 2 * WINDOW
        a_idx = lambda t: 0
        b_idx = lambda t: 1
    return pl.pallas_call(
        functools.partial(_swa_attn_kernel, tq=tq, k_valid=k_valid),
        grid=(bsz, sq // tq),
        in_specs=[pl.BlockSpec(memory_space=pltpu.SMEM),
                  pl.BlockSpec((1, tq, D_MODEL), lambda b, t: (b, t, 0)),
                  pl.BlockSpec((1, WINDOW, SWA_KV_DIM), lambda b, t: (b, a_idx(t), 0)),
                  pl.BlockSpec((1, WINDOW, SWA_KV_DIM), lambda b, t: (b, b_idx(t), 0)),
                  pl.BlockSpec((1, SWA_KV_DIM, WINDOW), lambda b, t: (b, 0, a_idx(t))),
                  pl.BlockSpec((1, SWA_KV_DIM, WINDOW), lambda b, t: (b, 0, b_idx(t)))],
        out_specs=pl.BlockSpec((1, tq, D_MODEL), lambda b, t: (b, t, 0)),
        out_shape=jax.ShapeDtypeStruct((bsz, sq, D_MODEL), BF16),
        scratch_shapes=[pltpu.VMEM((D_MODEL, tq), F32)],
        compiler_params=_params(2),
        name="swa_sink_attention",
    )(sink, q, k, k, vt, vt)


def _cast_pad_cols_kernel(x_ref, o_ref):
    n = x_ref.shape[-1]
    o_ref[0, :, :n] = x_ref[0].astype(BF16)
    o_ref[0, :, n:] = jnp.zeros((o_ref.shape[1], o_ref.shape[2] - n), BF16)


def cast_pad_cols(x, n_pad, tr=256):
    nl, r, c = x.shape
    return pl.pallas_call(
        _cast_pad_cols_kernel,
        grid=(nl, r // tr),
        in_specs=[pl.BlockSpec((1, tr, c), lambda l, i: (l, i, 0))],
        out_specs=pl.BlockSpec((1, tr, n_pad), lambda l, i: (l, i, 0)),
        out_shape=jax.ShapeDtypeStruct((nl, r, n_pad), BF16),
        compiler_params=_params(2),
        name="cast_pad_cols",
    )(x)


def _cast_pad_rows_kernel(x_ref, o_ref, *, n_blocks):
    i = pl.program_id(1)

    @pl.when(i < n_blocks)
    def _():
        o_ref[0] = x_ref[0].astype(BF16)

    @pl.when(i >= n_blocks)
    def _():
        o_ref[0] = jnp.zeros(o_ref.shape[1:], BF16)


def cast_pad_rows(x, r_pad, tr=256):
    nl, r, c = x.shape
    n_blocks = r // tr
    return pl.pallas_call(
        functools.partial(_cast_pad_rows_kernel, n_blocks=n_blocks),
        grid=(nl, r_pad // tr),
        in_specs=[pl.BlockSpec((1, tr, c), lambda l, i: (l, jnp.minimum(i, n_blocks - 1), 0))],
        out_specs=pl.BlockSpec((1, tr, c), lambda l, i: (l, i, 0)),
        out_shape=jax.ShapeDtypeStruct((nl, r_pad, c), BF16),
        compiler_params=_params(2),
        name="cast_pad_rows",
    )(x)


def _rope_slab_cols(w_rope):
    half = QK_ROPE // 2
    x1, x2 = w_rope[..., :half], w_rope[..., half:]
    return jnp.concatenate([x1, x2, x2, x1], axis=-1)


def _mla_tables(pos):
    half = QK_ROPE // 2
    inv = jnp.power(jnp.float32(ROPE_THETA), -jnp.arange(half, dtype=F32) * (2.0 / QK_ROPE))
    ang = pos.astype(F32)[:, None] * inv[None, :]
    c, s = jnp.cos(ang), jnp.sin(ang)
    z = jnp.zeros_like(c)
    return jnp.concatenate([c, c, z, z], axis=1), jnp.concatenate([-s, s, z, z], axis=1)


def _swa_tables(pos):
    half = ROPE_DIM_B // 2
    inv = jnp.power(jnp.float32(ROPE_THETA), -jnp.arange(half, dtype=F32) * (2.0 / ROPE_DIM_B))
    ang = pos.astype(F32)[:, None] * inv[None, :]
    c, s = jnp.cos(ang), jnp.sin(ang)
    n = pos.shape[0]
    rest = SWA_HEAD_DIM - ROPE_DIM_B
    c64 = jnp.concatenate([c, c, jnp.ones((n, rest), F32)], axis=1)
    s1 = jnp.concatenate([-s, jnp.zeros((n, SWA_HEAD_DIM - half), F32)], axis=1)
    s2 = jnp.concatenate([jnp.zeros((n, half), F32), s, jnp.zeros((n, rest), F32)], axis=1)
    rep = LANES // SWA_HEAD_DIM
    return jnp.tile(c64, (1, rep)), jnp.tile(s1, (1, rep)), jnp.tile(s2, (1, rep))


def _prep_weights(p):
    w = {}
    wa = p['mla_w_a'].astype(BF16)
    w['mla_w_a'] = jnp.concatenate(
        [wa[..., :Q_LORA + KV_LORA], _rope_slab_cols(wa[..., Q_LORA + KV_LORA:])], axis=-1)
    wuq = p['mla_w_uq'].astype(BF16).reshape(N_A, Q_LORA, MLA_HEADS, QK_NOPE + QK_ROPE)
    w['mla_w_uq'] = jnp.concatenate(
        [wuq[..., :QK_NOPE], _rope_slab_cols(wuq[..., QK_NOPE:])], axis=-1
    ).reshape(N_A, Q_LORA, MLA_HEADS * MLA_QDIM)
    w['mla_w_uk'] = p['mla_w_uk'].reshape(N_A, KV_LORA, MLA_HEADS * QK_NOPE).astype(BF16)
    w['mla_w_uvt'] = p['mla_w_uv'].reshape(N_A, KV_LORA, MLA_HEADS * V_DIM).transpose(0, 2, 1).astype(BF16)
    w['mla_w_o'] = p['mla_w_o'].astype(BF16)
    w['swa_w_kv'] = p['swa_w_kv'].astype(BF16)
    wq = p['swa_w_q'].reshape(N_B, D_MODEL, SWA_KV_HEADS, SWA_GROUP, SWA_HEAD_DIM)
    w['swa_w_q'] = wq.transpose(0, 1, 3, 2, 4).reshape(N_B, D_MODEL, D_MODEL).astype(BF16)
    wo = p['swa_w_o'].reshape(N_B, SWA_KV_HEADS, SWA_GROUP, SWA_HEAD_DIM, D_MODEL)
    w['swa_w_o'] = wo.transpose(0, 2, 1, 3, 4).reshape(N_B, D_MODEL, D_MODEL).astype(BF16)
    w['swa_sinks'] = p['swa_sinks'].reshape(N_B, SWA_KV_HEADS, SWA_GROUP)
    fpad = F_PAD - D_FF
    w['ffn_w_gate'] = cast_pad_cols(p['ffn_w_gate'], F_PAD)
    w['ffn_w_up'] = cast_pad_cols(p['ffn_w_up'], F_PAD)
    w['ffn_w_down'] = cast_pad_rows(p['ffn_w_down'], F_PAD)
    w['ffn_conv_w'] = jnp.pad(p['ffn_conv_w'], ((0, 0), (0, 0), (0, fpad)))
    w['ffn_conv_b'] = jnp.pad(p['ffn_conv_b'], ((0, 0), (0, fpad)))
    return w


def _trunk(x, pos, p, w, cache):
    bsz, seq, _ = x.shape
    m = bsz * seq
    h = x.reshape(m, D_MODEL)
    tm = _tile(m, 512)
    tm_l = _tile(m, 1024)
    past = 0 if cache is None else cache['ckv'].shape[2]

    mla_cos, mla_sin = (jnp.tile(t, (bsz, 1)) for t in _mla_tables(pos))
    swa_c, swa_s1, swa_s2 = (jnp.tile(t, (bsz, 1)) for t in _swa_tables(pos))
    tab = _row_spec(tm, LANES)
    tab_l = _row_spec(tm_l, LANES)

    def out_proj(o, w_o, layer, h_res):
        return matmul(o, w_o, layer=layer, tm=tm_l, tn=1024, epilogue=_epi_residual,
                      extras=(h_res,), extra_specs=(_tile_spec(tm_l, 1024),),
                      out_shapes=jax.ShapeDtypeStruct((m, D_MODEL), F32),
                      out_specs=_tile_spec(tm_l, 1024), name="out_proj_residual")

    ckv_rows, krope_rows, conv_rows = [], [], []
    ks = vs = ks_b = vs_t = None
    for l in range(DEPTH):
        if l < N_A:
            n_a = w['mla_w_a'].shape[2]
            cq, ckv, ckv_b, krope, kr_b = matmul(
                h, w['mla_w_a'], layer=l, tm=tm, tn=n_a, epilogue=_epi_mla_in, norm_x=True,
                extras=(p['mla_g_q'][l].reshape(1, Q_LORA), p['mla_g_kv'][l].reshape(1, KV_LORA),
                        mla_cos, mla_sin, p['norm_attn'][l].reshape(1, D_MODEL)),
                extra_specs=(_const_spec(Q_LORA), _const_spec(KV_LORA), tab, tab,
                             _const_spec(D_MODEL)),
                out_shapes=[jax.ShapeDtypeStruct((m, Q_LORA), BF16),
                            jax.ShapeDtypeStruct((m, KV_LORA), F32),
                            jax.ShapeDtypeStruct((m, KV_LORA), BF16),
                            jax.ShapeDtypeStruct((m, QK_ROPE), F32),
                            jax.ShapeDtypeStruct((m, LANES), BF16)],
                out_specs=[_row_spec(tm, Q_LORA), _row_spec(tm, KV_LORA), _row_spec(tm, KV_LORA),
                           _row_spec(tm, QK_ROPE), _row_spec(tm, LANES)],
                name="mla_in_proj")
            ckv_rows.append(ckv.reshape(bsz, seq, KV_LORA))
            krope_rows.append(krope.reshape(bsz, seq, QK_ROPE))
            tn_q = 8 * MLA_QDIM
            q = matmul(cq, w['mla_w_uq'], layer=l, tm=tm_l, tn=tn_q, epilogue=_epi_mla_q,
                       extras=(mla_cos, mla_sin), extra_specs=(tab_l, tab_l),
                       out_shapes=jax.ShapeDtypeStruct((m, MLA_HEADS * MLA_QDIM), BF16),
                       out_specs=_tile_spec(tm_l, tn_q), name="mla_q_proj")
            if cache is None:
                ckv_all = ckv_b.reshape(bsz, seq, KV_LORA)
                kr_all = kr_b.reshape(bsz, seq, LANES)
                k_valid = seq
                tq = tk = min(seq, 1024)
            else:
                k_valid = past + seq
                sk = -(-k_valid // LANES) * LANES
                ckv_all = jnp.concatenate(
                    [cache['ckv'][l].astype(BF16), ckv_b.reshape(bsz, seq, KV_LORA),
                     jnp.zeros((bsz, sk - k_valid, KV_LORA), BF16)], axis=1)
                kr_cache = jnp.pad(cache['krope'][l], ((0, 0), (0, 0), (0, LANES - QK_ROPE)))
                kr_all = jnp.concatenate(
                    [kr_cache.astype(BF16), kr_b.reshape(bsz, seq, LANES),
                     jnp.zeros((bsz, sk - k_valid, LANES), BF16)], axis=1)
                tq, tk = LANES, sk
            sk = ckv_all.shape[1]
            mk = bsz * sk
            tmk = _tile(mk, 1024)
            kn = matmul(ckv_all.reshape(mk, KV_LORA), w['mla_w_uk'], layer=l, tm=tmk, tn=2048,
                        epilogue=_epi_cast,
                        out_shapes=jax.ShapeDtypeStruct((mk, MLA_HEADS * QK_NOPE), BF16),
                        out_specs=_tile_spec(tmk, 2048), name="mla_k_up_proj")
            if sk % 512 == 0:
                vt = matmul_nt(w['mla_w_uvt'], l, ckv_all, tn=MLA_HEADS * V_DIM, ts=512)
            else:
                vt = matmul_nt(w['mla_w_uvt'], l, ckv_all, tn=1024, ts=sk)
            q3 = q.reshape(bsz, seq, MLA_HEADS * MLA_QDIM)
            if seq < tq:
                q3 = jnp.pad(q3, ((0, 0), (0, tq - seq), (0, 0)))
            o = mla_attention(q3, kn.reshape(bsz, sk, MLA_HEADS * QK_NOPE), kr_all, vt,
                              q_off=past, k_valid=k_valid, tq=tq, tk=tk)
            h = out_proj(o[:, :seq].reshape(m, D_MODEL), w['mla_w_o'], l, h)
        else:
            i = l - N_A
            if l == N_A:
                ks, vs, ks_b, vs_t = matmul(
                    h, w['swa_w_kv'], tm=tm, tn=2 * SWA_KV_DIM, epilogue=_epi_shared_kv,
                    norm_x=True,
                    extras=(swa_c, swa_s1, swa_s2, p['kv_shared_norm'].reshape(1, D_MODEL)),
                    extra_specs=(tab, tab, tab, _const_spec(D_MODEL)),
                    out_shapes=[jax.ShapeDtypeStruct((m, SWA_KV_DIM), F32)] * 2
                    + [jax.ShapeDtypeStruct((m, SWA_KV_DIM), BF16),
                       jax.ShapeDtypeStruct((SWA_KV_DIM, m), BF16)],
                    out_specs=[_row_spec(tm, SWA_KV_DIM)] * 3
                    + [pl.BlockSpec((SWA_KV_DIM, tm), lambda i, j: (0, i))],
                    name="swa_shared_kv_proj")
            hn = rmsnorm(h, p['norm_attn'][l], BF16)
            q = matmul(hn, w['swa_w_q'], layer=i, tm=tm_l, tn=1024, epilogue=_epi_swa_q,
                       extras=(swa_c, swa_s1, swa_s2), extra_specs=(tab_l, tab_l, tab_l),
                       out_shapes=jax.ShapeDtypeStruct((m, D_MODEL), BF16),
                       out_specs=_tile_spec(tm_l, 1024), name="swa_q_proj")
            q3 = q.reshape(bsz, seq, D_MODEL)
            k3 = ks_b.reshape(bsz, seq, SWA_KV_DIM)
            vt3 = vs_t.reshape(SWA_KV_DIM, bsz, seq).transpose(1, 0, 2)
            sink = w['swa_sinks'][i]
            if cache is None:
                o = swa_attention(q3, k3, vt3, sink)
            else:
                win = cache['swa_k'].shape[1]
                npad = 2 * WINDOW - win - seq
                k_all = jnp.concatenate(
                    [cache['swa_k'].reshape(bsz, win, SWA_KV_DIM).astype(BF16), k3,
                     jnp.zeros((bsz, npad, SWA_KV_DIM), BF16)], axis=1)
                vt_cache = cache['swa_v'].reshape(bsz, win, SWA_KV_DIM).transpose(0, 2, 1)
                vt_all = jnp.concatenate(
                    [vt_cache.astype(BF16), vt3, jnp.zeros((bsz, SWA_KV_DIM, npad), BF16)], axis=2)
                q3 = jnp.pad(q3, ((0, 0), (0, WINDOW - seq), (0, 0)))
                o = swa_attention(q3, k_all, vt_all, sink, k_valid=win + seq)[:, :seq]
            h = out_proj(o.reshape(m, D_MODEL), w['swa_w_o'], i, h)

        hn = rmsnorm(h, p['norm_ffn'][l], BF16)
        if cache is None:
            prev = jnp.zeros((bsz, SUBLANES, F_PAD), F32)
        else:
            prev = jnp.pad(cache['conv'][l],
                           ((0, 0), (SUBLANES - (CONV_W - 1), 0), (0, F_PAD - D_FF)))
        act, gl = ffn_a(hn, w['ffn_w_gate'], w['ffn_w_up'], l, w['ffn_conv_w'][l],
                        w['ffn_conv_b'][l], prev, seq)
        gl = gl.reshape(bsz, -1, SUBLANES, F_PAD)
        conv_rows.append(gl[:, -1, SUBLANES - (CONV_W - 1):, :D_FF])
        h = matmul(act, w['ffn_w_down'], layer=l, tm=tm, tn=512, epilogue=_epi_residual,
                   extras=(h,), extra_specs=(_tile_spec(tm, 512),),
                   out_shapes=jax.ShapeDtypeStruct((m, D_MODEL), F32),
                   out_specs=_tile_spec(tm, 512), name="ffn_down_residual")

    y = rmsnorm(h, p['norm_final'], F32).reshape(bsz, seq, D_MODEL)
    win = min(WINDOW, seq) if cache is None else seq
    ks4 = ks.reshape(bsz, seq, SWA_KV_DIM)[:, seq - win:].reshape(bsz, win, SWA_KV_HEADS, SWA_HEAD_DIM)
    vs4 = vs.reshape(bsz, seq, SWA_KV_DIM)[:, seq - win:].reshape(bsz, win, SWA_KV_HEADS, SWA_HEAD_DIM)
    return y, jnp.stack(ckv_rows), jnp.stack(krope_rows), ks4, vs4, jnp.stack(conv_rows)


def kernel(x_prompt, x_sample, cache_mla_ckv, cache_mla_krope, cache_swa_k, cache_swa_v, state_conv,
           norm_attn, norm_ffn, mla_w_a, mla_g_q, mla_g_kv, mla_w_uq, mla_w_uk, mla_w_uv, mla_w_o,
           kv_shared_norm, swa_w_kv, swa_w_q, swa_sinks, swa_w_o,
           ffn_w_gate, ffn_w_up, ffn_conv_w, ffn_conv_b, ffn_w_down, norm_final):
    p = {
        'norm_attn': norm_attn, 'norm_ffn': norm_ffn,
        'mla_w_a': mla_w_a, 'mla_g_q': mla_g_q, 'mla_g_kv': mla_g_kv, 'mla_w_uq': mla_w_uq,
        'mla_w_uk': mla_w_uk, 'mla_w_uv': mla_w_uv, 'mla_w_o': mla_w_o,
        'kv_shared_norm': kv_shared_norm, 'swa_w_kv': swa_w_kv, 'swa_w_q': swa_w_q,
        'swa_sinks': swa_sinks, 'swa_w_o': swa_w_o,
        'ffn_w_gate': ffn_w_gate, 'ffn_w_up': ffn_w_up, 'ffn_conv_w': ffn_conv_w,
        'ffn_conv_b': ffn_conv_b, 'ffn_w_down': ffn_w_down, 'norm_final': norm_final,
    }
    w = _prep_weights(p)
    pos_p = jnp.arange(x_prompt.shape[1])
    out_p = _trunk(x_prompt, pos_p, p, w, None)
    past = cache_mla_ckv.shape[2]
    pos_s = past + jnp.arange(x_sample.shape[1])
    cache = {'ckv': cache_mla_ckv, 'krope': cache_mla_krope, 'swa_k': cache_swa_k,
             'swa_v': cache_swa_v, 'conv': state_conv}
    out_s = _trunk(x_sample, pos_s, p, w, cache)
    return (out_p[0], out_s[0]) + out_p[1:] + out_s[1:]
```

```python
import functools

import jax
import jax.numpy as jnp
from jax import lax
from jax.experimental import pallas as pl
from jax.experimental.pallas import tpu as pltpu

D_MODEL = 4096
DEPTH = 4
CHUNK = 64
N_A = DEPTH // 2
N_B = DEPTH - N_A
ROPE_THETA = 500000.0
EPS = 1e-6
NEG_INF = -1e30
MLA_HEADS = D_MODEL // 128
Q_LORA = D_MODEL // 4
KV_LORA = 512
QK_NOPE = 128
QK_ROPE = 64
V_DIM = 128
MLA_SCALE = (QK_NOPE + QK_ROPE) ** -0.5
LOG2E = 1.4426950408889634
MLA_Q_SCALE = MLA_SCALE * LOG2E
SWA_HEAD_DIM = 64
SWA_HEADS = D_MODEL // SWA_HEAD_DIM
SWA_KV_HEADS = 8
SWA_GROUP = SWA_HEADS // SWA_KV_HEADS
WINDOW = 128
ROPE_DIM_B = SWA_HEAD_DIM // 4
SWA_SCALE = SWA_HEAD_DIM ** -0.5
D_FF = 256 * ((8 * D_MODEL // 3 + 255) // 256)
CONV_W = 3
FFN_TF = 1024
F_PAD = -(-D_FF // FFN_TF) * FFN_TF

LANES = 128
SUBLANES = 8
V7X_VMEM_LIMIT = 56 * 1024 * 1024

MLA_QDIM = 2 * LANES
DENOM_ROWS = 16
SWA_KV_DIM = SWA_KV_HEADS * SWA_HEAD_DIM
HEAD_SLAB = 2 * LANES
BF16 = jnp.bfloat16
F32 = jnp.float32


def _params(n_grid):
    return pltpu.CompilerParams(dimension_semantics=("arbitrary",) * n_grid,
                                vmem_limit_bytes=V7X_VMEM_LIMIT)


def _tile(m, cap):
    for step in (LANES, 16):
        t = (min(cap, m) // step) * step
        while t >= step:
            if m % t == 0:
                return t
            t -= step
    return m


def _rms(x, g):
    return x * lax.rsqrt(jnp.mean(x * x, axis=-1, keepdims=True) + EPS) * g


def _rmsnorm_kernel(x_ref, g_ref, o_ref):
    o_ref[...] = _rms(x_ref[...], g_ref[...]).astype(o_ref.dtype)


def rmsnorm(x, g, out_dtype):
    m, d = x.shape
    tm = _tile(m, 512)
    return pl.pallas_call(
        _rmsnorm_kernel,
        grid=(m // tm,),
        in_specs=[pl.BlockSpec((tm, d), lambda i: (i, 0)),
                  pl.BlockSpec((1, d), lambda i: (0, 0))],
        out_specs=pl.BlockSpec((tm, d), lambda i: (i, 0)),
        out_shape=jax.ShapeDtypeStruct((m, d), out_dtype),
        compiler_params=_params(1),
        name="rmsnorm",
    )(x, g.reshape(1, d))


def _mm_kernel(x_ref, w_ref, *rest, epilogue, n_extra, norm_x):
    extras = rest[:n_extra]
    if norm_x:
        x = _rms(x_ref[...], extras[-1][...]).astype(BF16)
    else:
        x = x_ref[...]
    y = jnp.dot(x, w_ref[...], preferred_element_type=F32)
    epilogue(y, extras, rest[n_extra:])


def matmul(x, w, *, tm, tn, epilogue, extras=(), extra_specs=(), out_shapes, out_specs, name,
           norm_x=False, layer=None):
    m, k = x.shape
    n = w.shape[-1]
    assert not norm_x or tn == n
    w_mode = dict(pipeline_mode=pl.Buffered(1)) if tn == n else {}
    if layer is None:
        w_spec = pl.BlockSpec((k, tn), lambda i, j: (0, j), **w_mode)
    else:
        w_spec = pl.BlockSpec((None, k, tn), lambda i, j: (layer, 0, j), **w_mode)
    return pl.pallas_call(
        functools.partial(_mm_kernel, epilogue=epilogue, n_extra=len(extras), norm_x=norm_x),
        grid=(m // tm, n // tn),
        in_specs=[pl.BlockSpec((tm, k), lambda i, j: (i, 0)), w_spec] + list(extra_specs),
        out_specs=out_specs,
        out_shape=out_shapes,
        compiler_params=_params(2),
        name=name,
    )(x, w, *extras)


def _epi_cast(y, extras, outs):
    outs[0][...] = y.astype(outs[0].dtype)


def _epi_residual(y, extras, outs):
    outs[0][...] = extras[0][...] + y


def _rope_slab(a, cos2, sin2):
    return a * cos2 + pltpu.roll(a, QK_ROPE, axis=1) * sin2


def _epi_mla_in(y, extras, outs):
    gq_ref, gkv_ref, cos_ref, sin_ref = extras[:4]
    cq_ref, ckv_ref, ckvb_ref, kr_ref, krb_ref = outs
    cq_ref[...] = _rms(y[:, :Q_LORA], gq_ref[...]).astype(BF16)
    ckv = _rms(y[:, Q_LORA:Q_LORA + KV_LORA], gkv_ref[...])
    ckv_ref[...] = ckv
    ckvb_ref[...] = ckv.astype(BF16)
    kr = _rope_slab(y[:, Q_LORA + KV_LORA:], cos_ref[...], sin_ref[...])
    kr_ref[...] = kr[:, :QK_ROPE]
    krb_ref[...] = kr.astype(BF16)


def _epi_mla_q(y, extras, outs):
    cos_ref, sin_ref = extras
    cos2 = cos_ref[...] * MLA_Q_SCALE
    sin2 = sin_ref[...] * MLA_Q_SCALE
    for s in range(y.shape[1] // MLA_QDIM):
        lo = s * MLA_QDIM
        outs[0][:, lo:lo + LANES] = (y[:, lo:lo + LANES] * MLA_Q_SCALE).astype(BF16)
        outs[0][:, lo + LANES:lo + MLA_QDIM] = _rope_slab(
            y[:, lo + LANES:lo + MLA_QDIM], cos2, sin2).astype(BF16)


def _rope_b(y, c, s1, s2, scale):
    half = ROPE_DIM_B // 2
    for s in range(y.shape[1] // LANES):
        a = y[:, s * LANES:(s + 1) * LANES]
        r = a * c + pltpu.roll(a, LANES - half, axis=1) * s1 + pltpu.roll(a, half, axis=1) * s2
        yield s, (r * scale if scale != 1.0 else r)


def _epi_swa_q(y, extras, outs):
    c_ref, s1_ref, s2_ref = extras
    for s, r in _rope_b(y, c_ref[...], s1_ref[...], s2_ref[...], SWA_SCALE):
        outs[0][:, s * LANES:(s + 1) * LANES] = r.astype(BF16)


def _epi_shared_kv(y, extras, outs):
    c_ref, s1_ref, s2_ref = extras[:3]
    k_ref, v_ref, kb_ref, vtb_ref = outs
    for s, r in _rope_b(y[:, :SWA_KV_DIM], c_ref[...], s1_ref[...], s2_ref[...], 1.0):
        k_ref[:, s * LANES:(s + 1) * LANES] = r
        kb_ref[:, s * LANES:(s + 1) * LANES] = r.astype(BF16)
    v = y[:, SWA_KV_DIM:]
    v_ref[...] = v
    vtb_ref[...] = v.T.astype(BF16)


def _row_spec(tm, width):
    return pl.BlockSpec((tm, width), lambda i, j: (i, 0))


def _tile_spec(tm, tn):
    return pl.BlockSpec((tm, tn), lambda i, j: (i, j))


def _const_spec(width):
    return pl.BlockSpec((1, width), lambda i, j: (0, 0))


def _ffn_a_kernel(x_ref, wg_ref, wu_ref, cw_ref, cb_ref, prev_ref, act_ref, gl_ref, gbuf, carry,
                  *, seg, nseg, tiles_per_seq, sub, tail_chunks):
    i = pl.program_id(0)
    j = pl.program_id(1)
    if tiles_per_seq > 1:
        @pl.when(jnp.logical_and(i == 0, j == 0))
        def _():
            carry[...] = jnp.zeros(carry.shape, F32)

        gbuf[0:SUBLANES] = jnp.where(i % tiles_per_seq == 0, prev_ref[0], carry[j])
    else:
        for s in range(nseg):
            gbuf[s * SUBLANES:(s + 1) * SUBLANES] = prev_ref[s]
    x = x_ref[...]
    n_sub = act_ref.shape[1] // sub
    row = lax.broadcasted_iota(jnp.int32, (SUBLANES, sub), 0)
    for c in range(n_sub):
        cols = slice(c * sub, (c + 1) * sub)
        w0 = cw_ref[0:1, cols]
        w1 = cw_ref[1:2, cols]
        w2 = cw_ref[2:3, cols]
        b = cb_ref[:, cols]
        n_chunk = tail_chunks if (c == n_sub - 1 and nseg == 1) else 1
        rows = seg // n_chunk
        if n_chunk == 1:
            g = jnp.dot(x, wg_ref[:, cols], preferred_element_type=F32)
            u = jnp.dot(x, wu_ref[:, cols], preferred_element_type=F32)
        for s in range(nseg):
            halo = gbuf[s * SUBLANES:(s + 1) * SUBLANES, cols]
            for k in range(n_chunk):
                r0 = s * seg + k * rows
                if n_chunk == 1:
                    gs, us = g[r0:r0 + rows], u[r0:r0 + rows]
                else:
                    gs = jnp.dot(x[r0:r0 + rows], wg_ref[:, cols], preferred_element_type=F32)
                    us = jnp.dot(x[r0:r0 + rows], wu_ref[:, cols], preferred_element_type=F32)
                r1 = pltpu.roll(gs, 1, axis=0)
                r2 = pltpu.roll(gs, 2, axis=0)
                h1 = jnp.where(row == 0, halo[SUBLANES - 1:SUBLANES], r1[:SUBLANES])
                h2 = jnp.where(row == 0, halo[SUBLANES - 2:SUBLANES - 1],
                               jnp.where(row == 1, halo[SUBLANES - 1:SUBLANES], r2[:SUBLANES]))
                g1 = jnp.concatenate([h1, r1[SUBLANES:]], axis=0)
                g2 = jnp.concatenate([h2, r2[SUBLANES:]], axis=0)
                gc = b + w0 * g2 + w1 * g1 + w2 * gs
                a = gc * jax.nn.sigmoid(gc) * us
                act_ref[r0:r0 + rows, cols] = a.astype(BF16)
                halo = gs[rows - SUBLANES:]
            gl_ref[s, :, cols] = halo
        if tiles_per_seq > 1:
            carry[j, :, cols] = halo


def ffn_a(hn, wg, wu, layer, cw, cb, prev, seq_len, *, tm_cap=1024, tf=FFN_TF, sub=2 * LANES):
    m, d = hn.shape
    f = wg.shape[2]
    tm = _tile(m, tm_cap)
    seg = min(seq_len, tm)
    nseg = tm // seg
    tiles_per_seq = seq_len // seg
    nj = f // tf
    if nseg == 1:
        prev_map = lambda i, j: (i // tiles_per_seq, 0, j)
    else:
        prev_map = lambda i, j: (i, 0, j)
    return pl.pallas_call(
        functools.partial(_ffn_a_kernel, seg=seg, nseg=nseg, tiles_per_seq=tiles_per_seq, sub=sub,
                          tail_chunks=4),
        grid=(m // tm, nj),
        in_specs=[pl.BlockSpec((tm, d), lambda i, j: (i, 0), pipeline_mode=pl.Buffered(1)),
                  pl.BlockSpec((None, d, tf), lambda i, j: (layer, 0, j)),
                  pl.BlockSpec((None, d, tf), lambda i, j: (layer, 0, j)),
                  pl.BlockSpec((CONV_W, tf), lambda i, j: (0, j)),
                  pl.BlockSpec((1, tf), lambda i, j: (0, j)),
                  pl.BlockSpec((nseg, SUBLANES, tf), prev_map)],
        out_specs=[pl.BlockSpec((tm, tf), lambda i, j: (i, j)),
                   pl.BlockSpec((nseg, SUBLANES, tf), lambda i, j: (i, 0, j))],
        out_shape=[jax.ShapeDtypeStruct((m, f), BF16),
                   jax.ShapeDtypeStruct((m // seg, SUBLANES, f), F32)],
        scratch_shapes=[pltpu.VMEM((nseg * SUBLANES, tf), F32),
                        pltpu.VMEM((nj, SUBLANES, tf), F32)],
        compiler_params=_params(2),
        name="ffn_gate_up_conv",
    )(hn, wg, wu, cw, cb.reshape(1, f), prev)


def _tile_pair(a, u, nq, paired):
    if not paired:
        return a, u
    first = u <= a
    return jnp.where(first, a, nq - 1 - a), jnp.where(first, u, u - a - 1)


def _mla_attn_kernel(q_ref, kn_ref, kr_ref, vt_ref, o_ref, m_ref, acc_ref, s_ref, p_ref,
                     *, hb, tq, tk, nq, nk, q_off, k_valid, cg, rb, paired):
    qi, ki = _tile_pair(pl.program_id(2), pl.program_id(3), nq, paired)
    q_lo = q_off + qi * tq
    k_lo = ki * tk
    k_last = jnp.minimum(((q_lo + tq - 1) // CHUNK * CHUNK + CHUNK - 1) // tk, nk - 1)

    @pl.when(ki == 0)
    def _():
        m_ref[...] = jnp.full(m_ref.shape, NEG_INF, F32)
        acc_ref[...] = jnp.zeros(acc_ref.shape, F32)

    needed = k_lo // CHUNK <= (q_lo + tq - 1) // CHUNK
    full = jnp.logical_and((k_lo + tk - 1) // CHUNK <= q_lo // CHUNK, k_lo + tk <= k_valid)

    def step(masked):
        kr = kr_ref[0]
        ones = jnp.ones((DENOM_ROWS, tk), BF16)
        diag = masked and paired
        half = tk // 2
        nt = (((1,), (1,)), ((), ()))
        for h in range(hb):
            buf = h % s_ref.shape[0]
            q = q_ref[0, :, h * MLA_QDIM:(h + 1) * MLA_QDIM]
            k = jnp.concatenate([kn_ref[0, :, h * QK_NOPE:(h + 1) * QK_NOPE], kr], axis=1)
            if diag:
                s_ref[buf, :half, :] = lax.dot_general(k[:half], q, nt, preferred_element_type=F32)
                s_ref[buf, half:, half:] = lax.dot_general(k[half:], q[half:], nt,
                                                           preferred_element_type=F32)
            else:
                s_ref[buf] = lax.dot_general(k, q, nt, preferred_element_type=F32)
            alphas = []
            for c in range(tq // cg):
                cols = slice(c * cg, (c + 1) * cg)
                n_blk = (half if diag and (c + 1) * cg <= half else tk) // rb
                if masked:
                    qc = (q_lo + c * cg + lax.broadcasted_iota(jnp.int32, (1, cg), 1)) // CHUNK

                def scores(r):
                    s = s_ref[buf, r * rb:(r + 1) * rb, cols]
                    if masked:
                        kp = k_lo + r * rb + lax.broadcasted_iota(jnp.int32, (rb, 1), 0)
                        s = jnp.where(jnp.logical_and(qc >= kp // CHUNK, kp < k_valid), s, NEG_INF)
                    return s

                part = None
                for r in range(n_blk):
                    blk = jnp.max(scores(r).reshape(rb // SUBLANES, SUBLANES, cg), axis=0)
                    part = blk if part is None else jnp.maximum(part, blk)
                m_prev = m_ref[h, :, cols]
                m_new = jnp.maximum(m_prev, jnp.max(part, axis=0, keepdims=True))
                alphas.append(jnp.exp2(m_prev - m_new))
                m_ref[h, :, cols] = m_new
                for r in range(n_blk):
                    p_ref[buf, r * rb:(r + 1) * rb, cols] = jnp.exp2(scores(r) - m_new).astype(BF16)
            vt = jnp.concatenate([vt_ref[0, h * V_DIM:(h + 1) * V_DIM, :], ones], axis=0)
            if diag:
                pv = jnp.concatenate(
                    [jnp.dot(vt[:, :half], p_ref[buf, :half, :half], preferred_element_type=F32),
                     jnp.dot(vt, p_ref[buf, :, half:], preferred_element_type=F32)], axis=1)
            else:
                pv = jnp.dot(vt, p_ref[buf], preferred_element_type=F32)
            acc_ref[h] = jnp.concatenate(alphas, axis=1) * acc_ref[h] + pv

    @pl.when(jnp.logical_and(needed, full))
    def _():
        step(False)

    @pl.when(jnp.logical_and(needed, jnp.logical_not(full)))
    def _():
        step(True)

    @pl.when(ki == k_last)
    def _():
        for h in range(hb):
            o_t = acc_ref[h, :V_DIM, :] / acc_ref[h, V_DIM:V_DIM + 1, :]
            o_ref[0, :, h * V_DIM:(h + 1) * V_DIM] = o_t.T.astype(BF16)


def mla_attention(q, kn, kr, vt, *, q_off, k_valid, tq, tk, hb=8, n_buf=2):
    bsz, sq, _ = q.shape
    sk = kn.shape[1]
    nq, nk = sq // tq, sk // tk
    n_hg = MLA_HEADS // hb

    paired = q_off == 0 and tq == tk and nq == nk and nq % 2 == 0

    def q_idx(a, u):
        return _tile_pair(a, u, nq, paired)[0]

    def k_idx(a, u):
        qi, ki = _tile_pair(a, u, nq, paired)
        last = ((q_off + (qi + 1) * tq - 1) // CHUNK * CHUNK + CHUNK - 1) // tk
        return jnp.minimum(ki, jnp.minimum(last, nk - 1))

    return pl.pallas_call(
        functools.partial(_mla_attn_kernel, hb=hb, tq=tq, tk=tk, nq=nq, nk=nk, q_off=q_off,
                          k_valid=k_valid, cg=min(tq, 2 * LANES), rb=LANES, paired=paired),
        grid=(bsz, n_hg, nq // 2, nq + 1) if paired else (bsz, n_hg, nq, nk),
        in_specs=[pl.BlockSpec((1, tq, hb * MLA_QDIM), lambda b, g, a, u: (b, q_idx(a, u), g)),
                  pl.BlockSpec((1, tk, hb * QK_NOPE), lambda b, g, a, u: (b, k_idx(a, u), g)),
                  pl.BlockSpec((1, tk, LANES), lambda b, g, a, u: (b, k_idx(a, u), 0)),
                  pl.BlockSpec((1, hb * V_DIM, tk), lambda b, g, a, u: (b, g, k_idx(a, u)))],
        out_specs=pl.BlockSpec((1, tq, hb * V_DIM), lambda b, g, a, u: (b, q_idx(a, u), g)),
        out_shape=jax.ShapeDtypeStruct((bsz, sq, MLA_HEADS * V_DIM), BF16),
        scratch_shapes=[pltpu.VMEM((hb, 1, tq), F32),
                        pltpu.VMEM((hb, V_DIM + DENOM_ROWS, tq), F32),
                        pltpu.VMEM((n_buf, tk, tq), F32),
                        pltpu.VMEM((n_buf, tk, tq), BF16)],
        compiler_params=_params(4),
        name="mla_flash_attention",
    )(q, kn, kr, vt)


def _nt_kernel(w_ref, x_ref, o_ref):
    o_ref[0] = lax.dot_general(w_ref[...], x_ref[0], (((1,), (1,)), ((), ())),
                               preferred_element_type=F32).astype(o_ref.dtype)


def matmul_nt(wt, layer, x, *, tn, ts):
    _, n, k = wt.shape
    bsz, s, _ = x.shape
    return pl.pallas_call(
        _nt_kernel,
        grid=(bsz, s // ts, n // tn),
        in_specs=[pl.BlockSpec((None, tn, k), lambda b, i, j: (layer, j, 0)),
                  pl.BlockSpec((1, ts, k), lambda b, i, j: (b, i, 0))],
        out_specs=pl.BlockSpec((1, tn, ts), lambda b, i, j: (b, j, i)),
        out_shape=jax.ShapeDtypeStruct((bsz, n, s), BF16),
        compiler_params=_params(3),
        name="mla_v_up_proj_t",
    )(wt, x)


def _swa_attn_kernel(sink_ref, q_ref, ka_ref, kb_ref, vta_ref, vtb_ref, o_ref, ot_ref, *, tq, k_valid):
    t = pl.program_id(1)
    k = jnp.concatenate([ka_ref[0], kb_ref[0]], axis=0)
    vt = jnp.concatenate([vta_ref[0], vtb_ref[0]], axis=1)
    tk = k.shape[0]
    kp = lax.broadcasted_iota(jnp.int32, (tk, 1), 0)
    if k_valid is None:
        kp = kp + (t - 1) * tq
        kc = kp // CHUNK
        qc = (t * tq + lax.broadcasted_iota(jnp.int32, (1, tq), 1)) // CHUNK
        valid = jnp.logical_and(kp >= 0, jnp.logical_and(kc >= qc - WINDOW // CHUNK, kc <= qc))
    else:
        valid = jnp.broadcast_to(kp < k_valid, (tk, tq))
    valid = jnp.concatenate([valid] * SWA_GROUP, axis=1)
    lane_head = lax.broadcasted_iota(jnp.int32, (1, HEAD_SLAB), 1) // SWA_HEAD_DIM
    heads_per_slab = HEAD_SLAB // SWA_HEAD_DIM
    for slab in range(SWA_KV_DIM // HEAD_SLAB):
        k_slab = k[:, slab * HEAD_SLAB:(slab + 1) * HEAD_SLAB]
        qs = jnp.concatenate(
            [q_ref[0, :, g * SWA_KV_DIM + slab * HEAD_SLAB:g * SWA_KV_DIM + (slab + 1) * HEAD_SLAB]
             for g in range(SWA_GROUP)], axis=0)
        for hh in range(heads_per_slab):
            kvh = slab * heads_per_slab + hh
            km = k_slab * (lane_head == hh).astype(BF16)
            vth = vt[kvh * SWA_HEAD_DIM:(kvh + 1) * SWA_HEAD_DIM, :]
            sb = jnp.concatenate([jnp.full((1, tq), sink_ref[kvh, g], F32) for g in range(SWA_GROUP)],
                                 axis=1)
            s = lax.dot_general(km, qs, (((1,), (1,)), ((), ())), preferred_element_type=F32)
            s = jnp.where(valid, s, NEG_INF)
            m = jnp.maximum(jnp.max(s, axis=0, keepdims=True), sb)
            e = jnp.exp(s - m)
            inv = 1.0 / (jnp.sum(e, axis=0, keepdims=True) + jnp.exp(sb - m))
            o_t = jnp.dot(vth, (e * inv).astype(BF16), preferred_element_type=F32)
            for g in range(SWA_GROUP):
                row = g * SWA_KV_DIM + kvh * SWA_HEAD_DIM
                ot_ref[row:row + SWA_HEAD_DIM, :] = o_t[:, g * tq:(g + 1) * tq]
    for c in range(D_MODEL // LANES):
        o_ref[0, :, c * LANES:(c + 1) * LANES] = ot_ref[c * LANES:(c + 1) * LANES, :].T.astype(BF16)


def swa_attention(q, k, vt, sink, *, k_valid=None):
    bsz, sq, _ = q.shape
    tq = WINDOW
    if k_valid is None:
        a_idx = lambda t: jnp.maximum(t - 1, 0)
        b_idx = lambda t: t
    else:
        assert sq == tq and k.shape[1] == 2 * WINDOW
        a_idx = lambda t: 0
        b_idx = lambda t: 1
    return pl.pallas_call(
        functools.partial(_swa_attn_kernel, tq=tq, k_valid=k_valid),
        grid=(bsz, sq // tq),
        in_specs=[pl.BlockSpec(memory_space=pltpu.SMEM),
                  pl.BlockSpec((1, tq, D_MODEL), lambda b, t: (b, t, 0)),
                  pl.BlockSpec((1, WINDOW, SWA_KV_DIM), lambda b, t: (b, a_idx(t), 0)),
                  pl.BlockSpec((1, WINDOW, SWA_KV_DIM), lambda b, t: (b, b_idx(t), 0)),
                  pl.BlockSpec((1, SWA_KV_DIM, WINDOW), lambda b, t: (b, 0, a_idx(t))),
                  pl.BlockSpec((1, SWA_KV_DIM, WINDOW), lambda b, t: (b, 0, b_idx(t)))],
        out_specs=pl.BlockSpec((1, tq, D_MODEL), lambda b, t: (b, t, 0)),
        out_shape=jax.ShapeDtypeStruct((bsz, sq, D_MODEL), BF16),
        scratch_shapes=[pltpu.VMEM((D_MODEL, tq), F32)],
        compiler_params=_params(2),
        name="swa_sink_attention",
    )(sink, q, k, k, vt, vt)


def _cast_pad_cols_kernel(x_ref, o_ref):
    n = x_ref.shape[-1]
    o_ref[0, :, :n] = x_ref[0].astype(BF16)
    o_ref[0, :, n:] = jnp.zeros((o_ref.shape[1], o_ref.shape[2] - n), BF16)


def cast_pad_cols(x, n_pad, tr=256):
    nl, r, c = x.shape
    return pl.pallas_call(
        _cast_pad_cols_kernel,
        grid=(nl, r // tr),
        in_specs=[pl.BlockSpec((1, tr, c), lambda l, i: (l, i, 0))],
        out_specs=pl.BlockSpec((1, tr, n_pad), lambda l, i: (l, i, 0)),
        out_shape=jax.ShapeDtypeStruct((nl, r, n_pad), BF16),
        compiler_params=_params(2),
        name="cast_pad_cols",
    )(x)


def _cast_pad_rows_kernel(x_ref, o_ref, *, n_blocks):
    i = pl.program_id(1)

    @pl.when(i < n_blocks)
    def _():
        o_ref[0] = x_ref[0].astype(BF16)

    @pl.when(i >= n_blocks)
    def _():
        o_ref[0] = jnp.zeros(o_ref.shape[1:], BF16)


def cast_pad_rows(x, r_pad, tr=256):
    nl, r, c = x.shape
    n_blocks = r // tr
    return pl.pallas_call(
        functools.partial(_cast_pad_rows_kernel, n_blocks=n_blocks),
        grid=(nl, r_pad // tr),
        in_specs=[pl.BlockSpec((1, tr, c), lambda l, i: (l, jnp.minimum(i, n_blocks - 1), 0))],
        out_specs=pl.BlockSpec((1, tr, c), lambda l, i: (l, i, 0)),
        out_shape=jax.ShapeDtypeStruct((nl, r_pad, c), BF16),
        compiler_params=_params(2),
        name="cast_pad_rows",
    )(x)


def _rope_slab_cols(w_rope):
    half = QK_ROPE // 2
    x1, x2 = w_rope[..., :half], w_rope[..., half:]
    return jnp.concatenate([x1, x2, x2, x1], axis=-1)


def _mla_tables(pos):
    half = QK_ROPE // 2
    inv = jnp.power(jnp.float32(ROPE_THETA), -jnp.arange(half, dtype=F32) * (2.0 / QK_ROPE))
    ang = pos.astype(F32)[:, None] * inv[None, :]
    c, s = jnp.cos(ang), jnp.sin(ang)
    z = jnp.zeros_like(c)
    return jnp.concatenate([c, c, z, z], axis=1), jnp.concatenate([-s, s, z, z], axis=1)


def _swa_tables(pos):
    half = ROPE_DIM_B // 2
    inv = jnp.power(jnp.float32(ROPE_THETA), -jnp.arange(half, dtype=F32) * (2.0 / ROPE_DIM_B))
    ang = pos.astype(F32)[:, None] * inv[None, :]
    c, s = jnp.cos(ang), jnp.sin(ang)
    n = pos.shape[0]
    rest = SWA_HEAD_DIM - ROPE_DIM_B
    c64 = jnp.concatenate([c, c, jnp.ones((n, rest), F32)], axis=1)
    s1 = jnp.concatenate([-s, jnp.zeros((n, SWA_HEAD_DIM - half), F32)], axis=1)
    s2 = jnp.concatenate([jnp.zeros((n, half), F32), s, jnp.zeros((n, rest), F32)], axis=1)
    rep = LANES // SWA_HEAD_DIM
    return jnp.tile(c64, (1, rep)), jnp.tile(s1, (1, rep)), jnp.tile(s2, (1, rep))


def _prep_weights(p):
    w = {}
    wa = p['mla_w_a'].astype(BF16)
    w['mla_w_a'] = jnp.concatenate(
        [wa[..., :Q_LORA + KV_LORA], _rope_slab_cols(wa[..., Q_LORA + KV_LORA:])], axis=-1)
    wuq = p['mla_w_uq'].astype(BF16).reshape(N_A, Q_LORA, MLA_HEADS, QK_NOPE + QK_ROPE)
    w['mla_w_uq'] = jnp.concatenate(
        [wuq[..., :QK_NOPE], _rope_slab_cols(wuq[..., QK_NOPE:])], axis=-1
    ).reshape(N_A, Q_LORA, MLA_HEADS * MLA_QDIM)
    w['mla_w_uk'] = p['mla_w_uk'].reshape(N_A, KV_LORA, MLA_HEADS * QK_NOPE).astype(BF16)
    w['mla_w_uvt'] = p['mla_w_uv'].reshape(N_A, KV_LORA, MLA_HEADS * V_DIM).transpose(0, 2, 1).astype(BF16)
    w['mla_w_o'] = p['mla_w_o'].astype(BF16)
    w['swa_w_kv'] = p['swa_w_kv'].astype(BF16)
    wq = p['swa_w_q'].reshape(N_B, D_MODEL, SWA_KV_HEADS, SWA_GROUP, SWA_HEAD_DIM)
    w['swa_w_q'] = wq.transpose(0, 1, 3, 2, 4).reshape(N_B, D_MODEL, D_MODEL).astype(BF16)
    wo = p['swa_w_o'].reshape(N_B, SWA_KV_HEADS, SWA_GROUP, SWA_HEAD_DIM, D_MODEL)
    w['swa_w_o'] = wo.transpose(0, 2, 1, 3, 4).reshape(N_B, D_MODEL, D_MODEL).astype(BF16)
    w['swa_sinks'] = p['swa_sinks'].reshape(N_B, SWA_KV_HEADS, SWA_GROUP)
    fpad = F_PAD - D_FF
    w['ffn_w_gate'] = cast_pad_cols(p['ffn_w_gate'], F_PAD)
    w['ffn_w_up'] = cast_pad_cols(p['ffn_w_up'], F_PAD)
    w['ffn_w_down'] = cast_pad_rows(p['ffn_w_down'], F_PAD)
    w['ffn_conv_w'] = jnp.pad(p['ffn_conv_w'], ((0, 0), (0, 0), (0, fpad)))
    w['ffn_conv_b'] = jnp.pad(p['ffn_conv_b'], ((0, 0), (0, fpad)))
    return w


def _trunk(x, pos, p, w, cache):
    bsz, seq, _ = x.shape
    m = bsz * seq
    h = x.reshape(m, D_MODEL)
    tm = _tile(m, 512)
    tm_l = _tile(m, 1024)
    past = 0 if cache is None else cache['ckv'].shape[2]

    mla_cos, mla_sin = (jnp.tile(t, (bsz, 1)) for t in _mla_tables(pos))
    swa_c, swa_s1, swa_s2 = (jnp.tile(t, (bsz, 1)) for t in _swa_tables(pos))
    tab = _row_spec(tm, LANES)
    tab_l = _row_spec(tm_l, LANES)

    def out_proj(o, w_o, layer, h_res):
        return matmul(o, w_o, layer=layer, tm=tm_l, tn=1024, epilogue=_epi_residual,
                      extras=(h_res,), extra_specs=(_tile_spec(tm_l, 1024),),
                      out_shapes=jax.ShapeDtypeStruct((m, D_MODEL), F32),
                      out_specs=_tile_spec(tm_l, 1024), name="out_proj_residual")

    ckv_rows, krope_rows, conv_rows = [], [], []
    ks = vs = ks_b = vs_t = None
    for l in range(DEPTH):
        if l < N_A:
            n_a = w['mla_w_a'].shape[2]
            cq, ckv, ckv_b, krope, kr_b = matmul(
                h, w['mla_w_a'], layer=l, tm=tm, tn=n_a, epilogue=_epi_mla_in, norm_x=True,
                extras=(p['mla_g_q'][l].reshape(1, Q_LORA), p['mla_g_kv'][l].reshape(1, KV_LORA),
                        mla_cos, mla_sin, p['norm_attn'][l].reshape(1, D_MODEL)),
                extra_specs=(_const_spec(Q_LORA), _const_spec(KV_LORA), tab, tab,
                             _const_spec(D_MODEL)),
                out_shapes=[jax.ShapeDtypeStruct((m, Q_LORA), BF16),
                            jax.ShapeDtypeStruct((m, KV_LORA), F32),
                            jax.ShapeDtypeStruct((m, KV_LORA), BF16),
                            jax.ShapeDtypeStruct((m, QK_ROPE), F32),
                            jax.ShapeDtypeStruct((m, LANES), BF16)],
                out_specs=[_row_spec(tm, Q_LORA), _row_spec(tm, KV_LORA), _row_spec(tm, KV_LORA),
                           _row_spec(tm, QK_ROPE), _row_spec(tm, LANES)],
                name="mla_in_proj")
            ckv_rows.append(ckv.reshape(bsz, seq, KV_LORA))
            krope_rows.append(krope.reshape(bsz, seq, QK_ROPE))
            tn_q = 8 * MLA_QDIM
            q = matmul(cq, w['mla_w_uq'], layer=l, tm=tm_l, tn=tn_q, epilogue=_epi_mla_q,
                       extras=(mla_cos, mla_sin), extra_specs=(tab_l, tab_l),
                       out_shapes=jax.ShapeDtypeStruct((m, MLA_HEADS * MLA_QDIM), BF16),
                       out_specs=_tile_spec(tm_l, tn_q), name="mla_q_proj")
            if cache is None:
                ckv_all = ckv_b.reshape(bsz, seq, KV_LORA)
                kr_all = kr_b.reshape(bsz, seq, LANES)
                k_valid = seq
                tq = tk = min(seq, 1024)
            else:
                k_valid = past + seq
                sk = -(-k_valid // LANES) * LANES
                ckv_all = jnp.concatenate(
                    [cache['ckv'][l].astype(BF16), ckv_b.reshape(bsz, seq, KV_LORA),
                     jnp.zeros((bsz, sk - k_valid, KV_LORA), BF16)], axis=1)
                kr_cache = jnp.pad(cache['krope'][l], ((0, 0), (0, 0), (0, LANES - QK_ROPE)))
                kr_all = jnp.concatenate(
                    [kr_cache.astype(BF16), kr_b.reshape(bsz, seq, LANES),
                     jnp.zeros((bsz, sk - k_valid, LANES), BF16)], axis=1)
                tq, tk = LANES, sk
            sk = ckv_all.shape[1]
            mk = bsz * sk
            tmk = _tile(mk, 1024)
            kn = matmul(ckv_all.reshape(mk, KV_LORA), w['mla_w_uk'], layer=l, tm=tmk, tn=2048,
                        epilogue=_epi_cast,
                        out_shapes=jax.ShapeDtypeStruct((mk, MLA_HEADS * QK_NOPE), BF16),
                        out_specs=_tile_spec(tmk, 2048), name="mla_k_up_proj")
            if sk % 512 == 0:
                vt = matmul_nt(w['mla_w_uvt'], l, ckv_all, tn=MLA_HEADS * V_DIM, ts=512)
            else:
                vt = matmul_nt(w['mla_w_uvt'], l, ckv_all, tn=1024, ts=sk)
            q3 = q.reshape(bsz, seq, MLA_HEADS * MLA_QDIM)
            if seq < tq:
                q3 = jnp.pad(q3, ((0, 0), (0, tq - seq), (0, 0)))
            o = mla_attention(q3, kn.reshape(bsz, sk, MLA_HEADS * QK_NOPE), kr_all, vt,
                              q_off=past, k_valid=k_valid, tq=tq, tk=tk)
            h = out_proj(o[:, :seq].reshape(m, D_MODEL), w['mla_w_o'], l, h)
        else:
            i = l - N_A
            if l == N_A:
                ks, vs, ks_b, vs_t = matmul(
                    h, w['swa_w_kv'], tm=tm, tn=2 * SWA_KV_DIM, epilogue=_epi_shared_kv,
                    norm_x=True,
                    extras=(swa_c, swa_s1, swa_s2, p['kv_shared_norm'].reshape(1, D_MODEL)),
                    extra_specs=(tab, tab, tab, _const_spec(D_MODEL)),
                    out_shapes=[jax.ShapeDtypeStruct((m, SWA_KV_DIM), F32)] * 2
                    + [jax.ShapeDtypeStruct((m, SWA_KV_DIM), BF16),
                       jax.ShapeDtypeStruct((SWA_KV_DIM, m), BF16)],
                    out_specs=[_row_spec(tm, SWA_KV_DIM)] * 3
                    + [pl.BlockSpec((SWA_KV_DIM, tm), lambda i, j: (0, i))],
                    name="swa_shared_kv_proj")
            hn = rmsnorm(h, p['norm_attn'][l], BF16)
            q = matmul(hn, w['swa_w_q'], layer=i, tm=tm_l, tn=1024, epilogue=_epi_swa_q,
                       extras=(swa_c, swa_s1, swa_s2), extra_specs=(tab_l, tab_l, tab_l),
                       out_shapes=jax.ShapeDtypeStruct((m, D_MODEL), BF16),
                       out_specs=_tile_spec(tm_l, 1024), name="swa_q_proj")
            q3 = q.reshape(bsz, seq, D_MODEL)
            k3 = ks_b.reshape(bsz, seq, SWA_KV_DIM)
            vt3 = vs_t.reshape(SWA_KV_DIM, bsz, seq).transpose(1, 0, 2)
            sink = w['swa_sinks'][i]
            if cache is None:
                o = swa_attention(q3, k3, vt3, sink)
            else:
                win = cache['swa_k'].shape[1]
                npad = 2 * WINDOW - win - seq
                k_all = jnp.concatenate(
                    [cache['swa_k'].reshape(bsz, win, SWA_KV_DIM).astype(BF16), k3,
                     jnp.zeros((bsz, npad, SWA_KV_DIM), BF16)], axis=1)
                vt_cache = cache['swa_v'].reshape(bsz, win, SWA_KV_DIM).transpose(0, 2, 1)
                vt_all = jnp.concatenate(
                    [vt_cache.astype(BF16), vt3, jnp.zeros((bsz, SWA_KV_DIM, npad), BF16)], axis=2)
                q3 = jnp.pad(q3, ((0, 0), (0, WINDOW - seq), (0, 0)))
                o = swa_attention(q3, k_all, vt_all, sink, k_valid=win + seq)[:, :seq]
            h = out_proj(o.reshape(m, D_MODEL), w['swa_w_o'], i, h)

        hn = rmsnorm(h, p['norm_ffn'][l], BF16)
        if cache is None:
            prev = jnp.zeros((bsz, SUBLANES, F_PAD), F32)
        else:
            prev = jnp.pad(cache['conv'][l],
                           ((0, 0), (SUBLANES - (CONV_W - 1), 0), (0, F_PAD - D_FF)))
        act, gl = ffn_a(hn, w['ffn_w_gate'], w['ffn_w_up'], l, w['ffn_conv_w'][l],
                        w['ffn_conv_b'][l], prev, seq)
        gl = gl.reshape(bsz, -1, SUBLANES, F_PAD)
        conv_rows.append(gl[:, -1, SUBLANES - (CONV_W - 1):, :D_FF])
        h = matmul(act, w['ffn_w_down'], layer=l, tm=tm, tn=512, epilogue=_epi_residual,
                   extras=(h,), extra_specs=(_tile_spec(tm, 512),),
                   out_shapes=jax.ShapeDtypeStruct((m, D_MODEL), F32),
                   out_specs=_tile_spec(tm, 512), name="ffn_down_residual")

    y = rmsnorm(h, p['norm_final'], F32).reshape(bsz, seq, D_MODEL)
    win = min(WINDOW, seq) if cache is None else seq
    ks4 = ks.reshape(bsz, seq, SWA_KV_DIM)[:, seq - win:].reshape(bsz, win, SWA_KV_HEADS, SWA_HEAD_DIM)
    vs4 = vs.reshape(bsz, seq, SWA_KV_DIM)[:, seq - win:].reshape(bsz, win, SWA_KV_HEADS, SWA_HEAD_DIM)
    return y, jnp.stack(ckv_rows), jnp.stack(krope_rows), ks4, vs4, jnp.stack(conv_rows)


def kernel(x_prompt, x_sample, cache_mla_ckv, cache_mla_krope, cache_swa_k, cache_swa_v, state_conv,
           norm_attn, norm_ffn, mla_w_a, mla_g_q, mla_g_kv, mla_w_uq, mla_w_uk, mla_w_uv, mla_w_o,
           kv_shared_norm, swa_w_kv, swa_w_q, swa_sinks, swa_w_o,
           ffn_w_gate, ffn_w_up, ffn_conv_w, ffn_conv_b, ffn_w_down, norm_final):
    p = {
        'norm_attn': norm_attn, 'norm_ffn': norm_ffn,
        'mla_w_a': mla_w_a, 'mla_g_q': mla_g_q, 'mla_g_kv': mla_g_kv, 'mla_w_uq': mla_w_uq,
        'mla_w_uk': mla_w_uk, 'mla_w_uv': mla_w_uv, 'mla_w_o': mla_w_o,
        'kv_shared_norm': kv_shared_norm, 'swa_w_kv': swa_w_kv, 'swa_w_q': swa_w_q,
        'swa_sinks': swa_sinks, 'swa_w_o': swa_w_o,
        'ffn_w_gate': ffn_w_gate, 'ffn_w_up': ffn_w_up, 'ffn_conv_w': ffn_conv_w,
        'ffn_conv_b': ffn_conv_b, 'ffn_w_down': ffn_w_down, 'norm_final': norm_final,
    }
    w = _prep_weights(p)
    pos_p = jnp.arange(x_prompt.shape[1])
    out_p = _trunk(x_prompt, pos_p, p, w, None)
    past = cache_mla_ckv.shape[2]
    pos_s = past + jnp.arange(x_sample.shape[1])
    cache = {'ckv': cache_mla_ckv, 'krope': cache_mla_krope, 'swa_k': cache_swa_k,
             'swa_v': cache_swa_v, 'conv': state_conv}
    out_s = _trunk(x_sample, pos_s, p, w, cache)
    return (out_p[0], out_s[0]) + out_p[1:] + out_s[1:]
```

```python
import functools

import jax
import jax.numpy as jnp
from jax import lax
from jax.experimental import pallas as pl
from jax.experimental.pallas import tpu as pltpu

D_MODEL = 4096
DEPTH = 4
CHUNK = 64
N_A = DEPTH // 2
N_B = DEPTH - N_A
ROPE_THETA = 500000.0
EPS = 1e-6
NEG_INF = -1e30
MLA_HEADS = D_MODEL // 128
Q_LORA = D_MODEL // 4
KV_LORA = 512
QK_NOPE = 128
QK_ROPE = 64
V_DIM = 128
MLA_SCALE = (QK_NOPE + QK_ROPE) ** -0.5
LOG2E = 1.4426950408889634
MLA_Q_SCALE = MLA_SCALE * LOG2E
SWA_HEAD_DIM = 64
SWA_HEADS = D_MODEL // SWA_HEAD_DIM
SWA_KV_HEADS = 8
SWA_GROUP = SWA_HEADS // SWA_KV_HEADS
WINDOW = 128
ROPE_DIM_B = SWA_HEAD_DIM // 4
SWA_SCALE = SWA_HEAD_DIM ** -0.5
D_FF = 256 * ((8 * D_MODEL // 3 + 255) // 256)
CONV_W = 3
FFN_TF = 512
F_PAD = -(-D_FF // FFN_TF) * FFN_TF

LANES = 128
SUBLANES = 8
V7X_VMEM_LIMIT = 56 * 1024 * 1024

MLA_QDIM = 2 * LANES
DENOM_ROWS = 16
SWA_KV_DIM = SWA_KV_HEADS * SWA_HEAD_DIM
HEAD_SLAB = 2 * LANES
BF16 = jnp.bfloat16
F32 = jnp.float32


def _params(n_grid):
    return pltpu.CompilerParams(dimension_semantics=("arbitrary",) * n_grid,
                                vmem_limit_bytes=V7X_VMEM_LIMIT)


def _tile(m, cap):
    for step in (LANES, 16):
        t = (min(cap, m) // step) * step
        while t >= step:
            if m % t == 0:
                return t
            t -= step
    return m


def _rms(x, g):
    return x * lax.rsqrt(jnp.mean(x * x, axis=-1, keepdims=True) + EPS) * g


def _rmsnorm_kernel(x_ref, g_ref, o_ref):
    o_ref[...] = _rms(x_ref[...], g_ref[...]).astype(o_ref.dtype)


def rmsnorm(x, g, out_dtype):
    m, d = x.shape
    tm = _tile(m, 512)
    return pl.pallas_call(
        _rmsnorm_kernel,
        grid=(m // tm,),
        in_specs=[pl.BlockSpec((tm, d), lambda i: (i, 0)),
                  pl.BlockSpec((1, d), lambda i: (0, 0))],
        out_specs=pl.BlockSpec((tm, d), lambda i: (i, 0)),
        out_shape=jax.ShapeDtypeStruct((m, d), out_dtype),
        compiler_params=_params(1),
        name="rmsnorm",
    )(x, g.reshape(1, d))


def _mm_kernel(x_ref, w_ref, *rest, epilogue, n_extra, norm_x):
    extras = rest[:n_extra]
    if norm_x:
        x = _rms(x_ref[...], extras[-1][...]).astype(BF16)
    else:
        x = x_ref[...]
    y = jnp.dot(x, w_ref[...], preferred_element_type=F32)
    epilogue(y, extras, rest[n_extra:])


def matmul(x, w, *, tm, tn, epilogue, extras=(), extra_specs=(), out_shapes, out_specs, name,
           norm_x=False, layer=None):
    m, k = x.shape
    n = w.shape[-1]
    assert not norm_x or tn == n
    w_mode = dict(pipeline_mode=pl.Buffered(1)) if tn == n else {}
    if layer is None:
        w_spec = pl.BlockSpec((k, tn), lambda i, j: (0, j), **w_mode)
    else:
        w_spec = pl.BlockSpec((None, k, tn), lambda i, j: (layer, 0, j), **w_mode)
    return pl.pallas_call(
        functools.partial(_mm_kernel, epilogue=epilogue, n_extra=len(extras), norm_x=norm_x),
        grid=(m // tm, n // tn),
        in_specs=[pl.BlockSpec((tm, k), lambda i, j: (i, 0)), w_spec] + list(extra_specs),
        out_specs=out_specs,
        out_shape=out_shapes,
        compiler_params=_params(2),
        name=name,
    )(x, w, *extras)


def _epi_cast(y, extras, outs):
    outs[0][...] = y.astype(outs[0].dtype)


def _epi_residual(y, extras, outs):
    outs[0][...] = extras[0][...] + y


def _rope_slab(a, cos2, sin2):
    return a * cos2 + pltpu.roll(a, QK_ROPE, axis=1) * sin2


def _epi_mla_in(y, extras, outs):
    gq_ref, gkv_ref, cos_ref, sin_ref = extras[:4]
    cq_ref, ckv_ref, ckvb_ref, kr_ref, krb_ref = outs
    cq_ref[...] = _rms(y[:, :Q_LORA], gq_ref[...]).astype(BF16)
    ckv = _rms(y[:, Q_LORA:Q_LORA + KV_LORA], gkv_ref[...])
    ckv_ref[...] = ckv
    ckvb_ref[...] = ckv.astype(BF16)
    kr = _rope_slab(y[:, Q_LORA + KV_LORA:], cos_ref[...], sin_ref[...])
    kr_ref[...] = kr[:, :QK_ROPE]
    krb_ref[...] = kr.astype(BF16)


def _epi_mla_q(y, extras, outs):
    cos_ref, sin_ref = extras
    cos2 = cos_ref[...] * MLA_Q_SCALE
    sin2 = sin_ref[...] * MLA_Q_SCALE
    for s in range(y.shape[1] // MLA_QDIM):
        lo = s * MLA_QDIM
        outs[0][:, lo:lo + LANES] = (y[:, lo:lo + LANES] * MLA_Q_SCALE).astype(BF16)
        outs[0][:, lo + LANES:lo + MLA_QDIM] = _rope_slab(
            y[:, lo + LANES:lo + MLA_QDIM], cos2, sin2).astype(BF16)


def _rope_b(y, c, s1, s2, scale):
    half = ROPE_DIM_B // 2
    for s in range(y.shape[1] // LANES):
        a = y[:, s * LANES:(s + 1) * LANES]
        r = a * c + pltpu.roll(a, LANES - half, axis=1) * s1 + pltpu.roll(a, half, axis=1) * s2
        yield s, (r * scale if scale != 1.0 else r)


def _epi_swa_q(y, extras, outs):
    c_ref, s1_ref, s2_ref = extras[:3]
    for s, r in _rope_b(y, c_ref[...], s1_ref[...], s2_ref[...], SWA_SCALE):
        outs[0][:, s * LANES:(s + 1) * LANES] = r.astype(BF16)


def _epi_shared_kv(y, extras, outs):
    c_ref, s1_ref, s2_ref = extras[:3]
    k_ref, v_ref, kb_ref, vtb_ref = outs
    for s, r in _rope_b(y[:, :SWA_KV_DIM], c_ref[...], s1_ref[...], s2_ref[...], 1.0):
        k_ref[:, s * LANES:(s + 1) * LANES] = r
        kb_ref[:, s * LANES:(s + 1) * LANES] = r.astype(BF16)
    v = y[:, SWA_KV_DIM:]
    v_ref[...] = v
    vtb_ref[...] = v.T.astype(BF16)


def _row_spec(tm, width):
    return pl.BlockSpec((tm, width), lambda i, j: (i, 0))


def _tile_spec(tm, tn):
    return pl.BlockSpec((tm, tn), lambda i, j: (i, j))


def _const_spec(width):
    return pl.BlockSpec((1, width), lambda i, j: (0, 0))


def _ffn_a_kernel(x_ref, wg_ref, wu_ref, cw_ref, cb_ref, prev_ref, act_ref, gl_ref, gbuf, carry,
                  *, seg, nseg, tiles_per_seq, sub, tail_chunks):
    i = pl.program_id(0)
    j = pl.program_id(1)
    if tiles_per_seq > 1:
        @pl.when(jnp.logical_and(i == 0, j == 0))
        def _():
            carry[...] = jnp.zeros(carry.shape, F32)

        gbuf[0:SUBLANES] = jnp.where(i % tiles_per_seq == 0, prev_ref[0], carry[j])
    else:
        for s in range(nseg):
            gbuf[s * SUBLANES:(s + 1) * SUBLANES] = prev_ref[s]
    x = x_ref[...]
    n_sub = act_ref.shape[1] // sub
    row = lax.broadcasted_iota(jnp.int32, (SUBLANES, sub), 0)
    for c in range(n_sub):
        cols = slice(c * sub, (c + 1) * sub)
        w0 = cw_ref[0:1, cols]
        w1 = cw_ref[1:2, cols]
        w2 = cw_ref[2:3, cols]
        b = cb_ref[:, cols]
        n_chunk = tail_chunks if (c == n_sub - 1 and nseg == 1) else 1
        rows = seg // n_chunk
        if n_chunk == 1:
            g = jnp.dot(x, wg_ref[:, cols], preferred_element_type=F32)
            u = jnp.dot(x, wu_ref[:, cols], preferred_element_type=F32)
        for s in range(nseg):
            halo = gbuf[s * SUBLANES:(s + 1) * SUBLANES, cols]
            for k in range(n_chunk):
                r0 = s * seg + k * rows
                if n_chunk == 1:
                    gs, us = g[r0:r0 + rows], u[r0:r0 + rows]
                else:
                    gs = jnp.dot(x[r0:r0 + rows], wg_ref[:, cols], preferred_element_type=F32)
                    us = jnp.dot(x[r0:r0 + rows], wu_ref[:, cols], preferred_element_type=F32)
                r1 = pltpu.roll(gs, 1, axis=0)
                r2 = pltpu.roll(gs, 2, axis=0)
                h1 = jnp.where(row == 0, halo[SUBLANES - 1:SUBLANES], r1[:SUBLANES])
                h2 = jnp.where(row == 0, halo[SUBLANES - 2:SUBLANES - 1],
                               jnp.where(row == 1, halo[SUBLANES - 1:SUBLANES], r2[:SUBLANES]))
                g1 = jnp.concatenate([h1, r1[SUBLANES:]], axis=0)
                g2 = jnp.concatenate([h2, r2[SUBLANES:]], axis=0)
                gc = b + w0 * g2 + w1 * g1 + w2 * gs
                a = gc * jax.nn.sigmoid(gc) * us
                act_ref[r0:r0 + rows, cols] = a.astype(BF16)
                halo = gs[rows - SUBLANES:]
            gl_ref[s, :, cols] = halo
        if tiles_per_seq > 1:
            carry[j, :, cols] = halo


def ffn_a(hn, wg, wu, layer, cw, cb, prev, seq_len, *, tm_cap=1024, tf=FFN_TF, sub=2 * LANES):
    m, d = hn.shape
    f = wg.shape[2]
    tm = _tile(m, tm_cap)
    seg = min(seq_len, tm)
    nseg = tm // seg
    tiles_per_seq = seq_len // seg
    nj = f // tf
    if nseg == 1:
        prev_map = lambda i, j: (i // tiles_per_seq, 0, j)
    else:
        prev_map = lambda i, j: (i, 0, j)
    return pl.pallas_call(
        functools.partial(_ffn_a_kernel, seg=seg, nseg=nseg, tiles_per_seq=tiles_per_seq, sub=sub,
                          tail_chunks=4),
        grid=(m // tm, nj),
        in_specs=[pl.BlockSpec((tm, d), lambda i, j: (i, 0)),
                  pl.BlockSpec((None, d, tf), lambda i, j: (layer, 0, j)),
                  pl.BlockSpec((None, d, tf), lambda i, j: (layer, 0, j)),
                  pl.BlockSpec((CONV_W, tf), lambda i, j: (0, j)),
                  pl.BlockSpec((1, tf), lambda i, j: (0, j)),
                  pl.BlockSpec((nseg, SUBLANES, tf), prev_map)],
        out_specs=[pl.BlockSpec((tm, tf), lambda i, j: (i, j)),
                   pl.BlockSpec((nseg, SUBLANES, tf), lambda i, j: (i, 0, j))],
        out_shape=[jax.ShapeDtypeStruct((m, f), BF16),
                   jax.ShapeDtypeStruct((m // seg, SUBLANES, f), F32)],
        scratch_shapes=[pltpu.VMEM((nseg * SUBLANES, tf), F32),
                        pltpu.VMEM((nj, SUBLANES, tf), F32)],
        compiler_params=_params(2),
        name="ffn_gate_up_conv",
    )(hn, wg, wu, cw, cb.reshape(1, f), prev)


def _tile_pair(a, u, nq, paired):
    if not paired:
        return a, u
    first = u <= a
    return jnp.where(first, a, nq - 1 - a), jnp.where(first, u, u - a - 1)


def _mla_attn_kernel(q_ref, kn_ref, kr_ref, vt_ref, o_ref, m_ref, acc_ref, s_ref, p_ref,
                     *, hb, tq, tk, nq, nk, q_off, k_valid, cg, rb, paired):
    qi, ki = _tile_pair(pl.program_id(2), pl.program_id(3), nq, paired)
    q_lo = q_off + qi * tq
    k_lo = ki * tk
    k_last = jnp.minimum(((q_lo + tq - 1) // CHUNK * CHUNK + CHUNK - 1) // tk, nk - 1)

    @pl.when(ki == 0)
    def _():
        m_ref[...] = jnp.full(m_ref.shape, NEG_INF, F32)
        acc_ref[...] = jnp.zeros(acc_ref.shape, F32)

    needed = k_lo // CHUNK <= (q_lo + tq - 1) // CHUNK
    full = jnp.logical_and((k_lo + tk - 1) // CHUNK <= q_lo // CHUNK, k_lo + tk <= k_valid)

    def step(masked):
        kr = kr_ref[0]
        ones = jnp.ones((DENOM_ROWS, tk), BF16)
        diag = masked and paired
        half = tk // 2
        nt = (((1,), (1,)), ((), ()))
        for h in range(hb):
            buf = h % s_ref.shape[0]
            q = q_ref[0, :, h * MLA_QDIM:(h + 1) * MLA_QDIM]
            k = jnp.concatenate([kn_ref[0, :, h * QK_NOPE:(h + 1) * QK_NOPE], kr], axis=1)
            if diag:
                s_ref[buf, :half, :] = lax.dot_general(k[:half], q, nt, preferred_element_type=F32)
                s_ref[buf, half:, half:] = lax.dot_general(k[half:], q[half:], nt,
                                                           preferred_element_type=F32)
            else:
                s_ref[buf] = lax.dot_general(k, q, nt, preferred_element_type=F32)
            alphas = []
            for c in range(tq // cg):
                cols = slice(c * cg, (c + 1) * cg)
                n_blk = (half if diag and (c + 1) * cg <= half else tk) // rb
                if masked:
                    qc = (q_lo + c * cg + lax.broadcasted_iota(jnp.int32, (1, cg), 1)) // CHUNK

                def scores(r):
                    s = s_ref[buf, r * rb:(r + 1) * rb, cols]
                    if masked:
                        kp = k_lo + r * rb + lax.broadcasted_iota(jnp.int32, (rb, 1), 0)
                        s = jnp.where(jnp.logical_and(qc >= kp // CHUNK, kp < k_valid), s, NEG_INF)
                    return s

                part = None
                for r in range(n_blk):
                    blk = jnp.max(scores(r).reshape(rb // SUBLANES, SUBLANES, cg), axis=0)
                    part = blk if part is None else jnp.maximum(part, blk)
                m_prev = m_ref[h, :, cols]
                m_new = jnp.maximum(m_prev, jnp.max(part, axis=0, keepdims=True))
                alphas.append(jnp.exp2(m_prev - m_new))
                m_ref[h, :, cols] = m_new
                for r in range(n_blk):
                    p_ref[buf, r * rb:(r + 1) * rb, cols] = jnp.exp2(scores(r) - m_new).astype(BF16)
            vt = jnp.concatenate([vt_ref[0, h * V_DIM:(h + 1) * V_DIM, :], ones], axis=0)
            if diag:
                pv = jnp.concatenate(
                    [jnp.dot(vt[:, :half], p_ref[buf, :half, :half], preferred_element_type=F32),
                     jnp.dot(vt, p_ref[buf, :, half:], preferred_element_type=F32)], axis=1)
            else:
                pv = jnp.dot(vt, p_ref[buf], preferred_element_type=F32)
            acc_ref[h] = jnp.concatenate(alphas, axis=1) * acc_ref[h] + pv

    @pl.when(jnp.logical_and(needed, full))
    def _():
        step(False)

    @pl.when(jnp.logical_and(needed, jnp.logical_not(full)))
    def _():
        step(True)

    @pl.when(ki == k_last)
    def _():
        for h in range(hb):
            o_t = acc_ref[h, :V_DIM, :] / acc_ref[h, V_DIM:V_DIM + 1, :]
            o_ref[0, :, h * V_DIM:(h + 1) * V_DIM] = o_t.T.astype(BF16)


def mla_attention(q, kn, kr, vt, *, q_off, k_valid, tq, tk, hb=8, n_buf=2):
    bsz, sq, _ = q.shape
    sk = kn.shape[1]
    nq, nk = sq // tq, sk // tk
    n_hg = MLA_HEADS // hb

    paired = q_off == 0 and tq == tk and nq == nk and nq % 2 == 0

    def q_idx(a, u):
        return _tile_pair(a, u, nq, paired)[0]

    def k_idx(a, u):
        qi, ki = _tile_pair(a, u, nq, paired)
        last = ((q_off + (qi + 1) * tq - 1) // CHUNK * CHUNK + CHUNK - 1) // tk
        return jnp.minimum(ki, jnp.minimum(last, nk - 1))

    return pl.pallas_call(
        functools.partial(_mla_attn_kernel, hb=hb, tq=tq, tk=tk, nq=nq, nk=nk, q_off=q_off,
                          k_valid=k_valid, cg=min(tq, 2 * LANES), rb=LANES, paired=paired),
        grid=(bsz, n_hg, nq // 2, nq + 1) if paired else (bsz, n_hg, nq, nk),
        in_specs=[pl.BlockSpec((1, tq, hb * MLA_QDIM), lambda b, g, a, u: (b, q_idx(a, u), g)),
                  pl.BlockSpec((1, tk, hb * QK_NOPE), lambda b, g, a, u: (b, k_idx(a, u), g)),
                  pl.BlockSpec((1, tk, LANES), lambda b, g, a, u: (b, k_idx(a, u), 0)),
                  pl.BlockSpec((1, hb * V_DIM, tk), lambda b, g, a, u: (b, g, k_idx(a, u)))],
        out_specs=pl.BlockSpec((1, tq, hb * V_DIM), lambda b, g, a, u: (b, q_idx(a, u), g)),
        out_shape=jax.ShapeDtypeStruct((bsz, sq, MLA_HEADS * V_DIM), BF16),
        scratch_shapes=[pltpu.VMEM((hb, 1, tq), F32),
                        pltpu.VMEM((hb, V_DIM + DENOM_ROWS, tq), F32),
                        pltpu.VMEM((n_buf, tk, tq), F32),
                        pltpu.VMEM((n_buf, tk, tq), BF16)],
        compiler_params=_params(4),
        name="mla_flash_attention",
    )(q, kn, kr, vt)


def _nt_kernel(w_ref, x_ref, o_ref):
    o_ref[0] = lax.dot_general(w_ref[...], x_ref[0], (((1,), (1,)), ((), ())),
                               preferred_element_type=F32).astype(o_ref.dtype)


def matmul_nt(wt, layer, x, *, tn, ts):
    _, n, k = wt.shape
    bsz, s, _ = x.shape
    return pl.pallas_call(
        _nt_kernel,
        grid=(bsz, s // ts, n // tn),
        in_specs=[pl.BlockSpec((None, tn, k), lambda b, i, j: (layer, j, 0)),
                  pl.BlockSpec((1, ts, k), lambda b, i, j: (b, i, 0))],
        out_specs=pl.BlockSpec((1, tn, ts), lambda b, i, j: (b, j, i)),
        out_shape=jax.ShapeDtypeStruct((bsz, n, s), BF16),
        compiler_params=_params(3),
        name="mla_v_up_proj_t",
    )(wt, x)


def _swa_attn_kernel(sink_ref, q_ref, ka_ref, kb_ref, vta_ref, vtb_ref, o_ref, ot_ref, *, tq, k_valid):
    t = pl.program_id(1)
    k = jnp.concatenate([ka_ref[0], kb_ref[0]], axis=0)
    vt = jnp.concatenate([vta_ref[0], vtb_ref[0]], axis=1)
    tk = k.shape[0]
    kp = lax.broadcasted_iota(jnp.int32, (tk, 1), 0)
    if k_valid is None:
        kp = kp + (t - 1) * tq
        kc = kp // CHUNK
        qc = (t * tq + lax.broadcasted_iota(jnp.int32, (1, tq), 1)) // CHUNK
        valid = jnp.logical_and(kp >= 0, jnp.logical_and(kc >= qc - WINDOW // CHUNK, kc <= qc))
    else:
        valid = jnp.broadcast_to(kp < k_valid, (tk, tq))
    valid = jnp.concatenate([valid] * SWA_GROUP, axis=1)
    lane_head = lax.broadcasted_iota(jnp.int32, (1, HEAD_SLAB), 1) // SWA_HEAD_DIM
    heads_per_slab = HEAD_SLAB // SWA_HEAD_DIM
    for slab in range(SWA_KV_DIM // HEAD_SLAB):
        k_slab = k[:, slab * HEAD_SLAB:(slab + 1) * HEAD_SLAB]
        qs = jnp.concatenate(
            [q_ref[0, :, g * SWA_KV_DIM + slab * HEAD_SLAB:g * SWA_KV_DIM + (slab + 1) * HEAD_SLAB]
             for g in range(SWA_GROUP)], axis=0)
        for hh in range(heads_per_slab):
            kvh = slab * heads_per_slab + hh
            km = k_slab * (lane_head == hh).astype(BF16)
            vth = vt[kvh * SWA_HEAD_DIM:(kvh + 1) * SWA_HEAD_DIM, :]
            sb = jnp.concatenate([jnp.full((1, tq), sink_ref[kvh, g], F32) for g in range(SWA_GROUP)],
                                 axis=1)
            s = lax.dot_general(km, qs, (((1,), (1,)), ((), ())), preferred_element_type=F32)
            s = jnp.where(valid, s, NEG_INF)
            m = jnp.maximum(jnp.max(s, axis=0, keepdims=True), sb)
            e = jnp.exp(s - m)
            inv = 1.0 / (jnp.sum(e, axis=0, keepdims=True) + jnp.exp(sb - m))
            o_t = jnp.dot(vth, (e * inv).astype(BF16), preferred_element_type=F32)
            for g in range(SWA_GROUP):
                row = g * SWA_KV_DIM + kvh * SWA_HEAD_DIM
                ot_ref[row:row + SWA_HEAD_DIM, :] = o_t[:, g * tq:(g + 1) * tq]
    for c in range(D_MODEL // LANES):
        o_ref[0, :, c * LANES:(c + 1) * LANES] = ot_ref[c * LANES:(c + 1) * LANES, :].T.astype(BF16)


def swa_attention(q, k, vt, sink, *, k_valid=None):
    bsz, sq, _ = q.shape
    tq = WINDOW
    if k_valid is None:
        a_idx = lambda t: jnp.maximum(t - 1, 0)
        b_idx = lambda t: t
    else:
        assert sq == tq and k.shape[1] == 2 * WINDOW
        a_idx = lambda t: 0
        b_idx = lambda t: 1
    return pl.pallas_call(
        functools.partial(_swa_attn_kernel, tq=tq, k_valid=k_valid),
        grid=(bsz, sq // tq),
        in_specs=[pl.BlockSpec(memory_space=pltpu.SMEM),
                  pl.BlockSpec((1, tq, D_MODEL), lambda b, t: (b, t, 0)),
                  pl.BlockSpec((1, WINDOW, SWA_KV_DIM), lambda b, t: (b, a_idx(t), 0)),
                  pl.BlockSpec((1, WINDOW, SWA_KV_DIM), lambda b, t: (b, b_idx(t), 0)),
                  pl.BlockSpec((1, SWA_KV_DIM, WINDOW), lambda b, t: (b, 0, a_idx(t))),
                  pl.BlockSpec((1, SWA_KV_DIM, WINDOW), lambda b, t: (b, 0, b_idx(t)))],
        out_specs=pl.BlockSpec((1, tq, D_MODEL), lambda b, t: (b, t, 0)),
        out_shape=jax.ShapeDtypeStruct((bsz, sq, D_MODEL), BF16),
        scratch_shapes=[pltpu.VMEM((D_MODEL, tq), F32)],
        compiler_params=_params(2),
        name="swa_sink_attention",
    )(sink, q, k, k, vt, vt)


def _cast_pad_cols_kernel(x_ref, o_ref):
    n = x_ref.shape[-1]
    o_ref[0, :, :n] = x_ref[0].astype(BF16)
    o_ref[0, :, n:] = jnp.zeros((o_ref.shape[1], o_ref.shape[2] - n), BF16)


def cast_pad_cols(x, n_pad, tr=256):
    nl, r, c = x.shape
    return pl.pallas_call(
        _cast_pad_cols_kernel,
        grid=(nl, r // tr),
        in_specs=[pl.BlockSpec((1, tr, c), lambda l, i: (l, i, 0))],
        out_specs=pl.BlockSpec((1, tr, n_pad), lambda l, i: (l, i, 0)),
        out_shape=jax.ShapeDtypeStruct((nl, r, n_pad), BF16),
        compiler_params=_params(2),
        name="cast_pad_cols",
    )(x)


def _cast_pad_rows_kernel(x_ref, o_ref, *, n_blocks):
    i = pl.program_id(1)

    @pl.when(i < n_blocks)
    def _():
        o_ref[0] = x_ref[0].astype(BF16)

    @pl.when(i >= n_blocks)
    def _():
        o_ref[0] = jnp.zeros(o_ref.shape[1:], BF16)


def cast_pad_rows(x, r_pad, tr=256):
    nl, r, c = x.shape
    n_blocks = r // tr
    return pl.pallas_call(
        functools.partial(_cast_pad_rows_kernel, n_blocks=n_blocks),
        grid=(nl, r_pad // tr),
        in_specs=[pl.BlockSpec((1, tr, c), lambda l, i: (l, jnp.minimum(i, n_blocks - 1), 0))],
        out_specs=pl.BlockSpec((1, tr, c), lambda l, i: (l, i, 0)),
        out_shape=jax.ShapeDtypeStruct((nl, r_pad, c), BF16),
        compiler_params=_params(2),
        name="cast_pad_rows",
    )(x)


def _rope_slab_cols(w_rope):
    half = QK_ROPE // 2
    x1, x2 = w_rope[..., :half], w_rope[..., half:]
    return jnp.concatenate([x1, x2, x2, x1], axis=-1)


def _mla_tables(pos):
    half = QK_ROPE // 2
    inv = jnp.power(jnp.float32(ROPE_THETA), -jnp.arange(half, dtype=F32) * (2.0 / QK_ROPE))
    ang = pos.astype(F32)[:, None] * inv[None, :]
    c, s = jnp.cos(ang), jnp.sin(ang)
    z = jnp.zeros_like(c)
    return jnp.concatenate([c, c, z, z], axis=1), jnp.concatenate([-s, s, z, z], axis=1)


def _swa_tables(pos):
    half = ROPE_DIM_B // 2
    inv = jnp.power(jnp.float32(ROPE_THETA), -jnp.arange(half, dtype=F32) * (2.0 / ROPE_DIM_B))
    ang = pos.astype(F32)[:, None] * inv[None, :]
    c, s = jnp.cos(ang), jnp.sin(ang)
    n = pos.shape[0]
    rest = SWA_HEAD_DIM - ROPE_DIM_B
    c64 = jnp.concatenate([c, c, jnp.ones((n, rest), F32)], axis=1)
    s1 = jnp.concatenate([-s, jnp.zeros((n, SWA_HEAD_DIM - half), F32)], axis=1)
    s2 = jnp.concatenate([jnp.zeros((n, half), F32), s, jnp.zeros((n, rest), F32)], axis=1)
    rep = LANES // SWA_HEAD_DIM
    return jnp.tile(c64, (1, rep)), jnp.tile(s1, (1, rep)), jnp.tile(s2, (1, rep))


def _prep_weights(p):
    w = {}
    wa = p['mla_w_a'].astype(BF16)
    w['mla_w_a'] = jnp.concatenate(
        [wa[..., :Q_LORA + KV_LORA], _rope_slab_cols(wa[..., Q_LORA + KV_LORA:])], axis=-1)
    wuq = p['mla_w_uq'].astype(BF16).reshape(N_A, Q_LORA, MLA_HEADS, QK_NOPE + QK_ROPE)
    w['mla_w_uq'] = jnp.concatenate(
        [wuq[..., :QK_NOPE], _rope_slab_cols(wuq[..., QK_NOPE:])], axis=-1
    ).reshape(N_A, Q_LORA, MLA_HEADS * MLA_QDIM)
    w['mla_w_uk'] = p['mla_w_uk'].reshape(N_A, KV_LORA, MLA_HEADS * QK_NOPE).astype(BF16)
    w['mla_w_uvt'] = p['mla_w_uv'].reshape(N_A, KV_LORA, MLA_HEADS * V_DIM).transpose(0, 2, 1).astype(BF16)
    w['mla_w_o'] = p['mla_w_o'].astype(BF16)
    w['swa_w_kv'] = p['swa_w_kv'].astype(BF16)
    wq = p['swa_w_q'].reshape(N_B, D_MODEL, SWA_KV_HEADS, SWA_GROUP, SWA_HEAD_DIM)
    w['swa_w_q'] = wq.transpose(0, 1, 3, 2, 4).reshape(N_B, D_MODEL, D_MODEL).astype(BF16)
    wo = p['swa_w_o'].reshape(N_B, SWA_KV_HEADS, SWA_GROUP, SWA_HEAD_DIM, D_MODEL)
    w['swa_w_o'] = wo.transpose(0, 2, 1, 3, 4).reshape(N_B, D_MODEL, D_MODEL).astype(BF16)
    w['swa_sinks'] = p['swa_sinks'].reshape(N_B, SWA_KV_HEADS, SWA_GROUP)
    fpad = F_PAD - D_FF
    w['ffn_w_gate'] = cast_pad_cols(p['ffn_w_gate'], F_PAD)
    w['ffn_w_up'] = cast_pad_cols(p['ffn_w_up'], F_PAD)
    w['ffn_w_down'] = cast_pad_rows(p['ffn_w_down'], F_PAD)
    w['ffn_conv_w'] = jnp.pad(p['ffn_conv_w'], ((0, 0), (0, 0), (0, fpad)))
    w['ffn_conv_b'] = jnp.pad(p['ffn_conv_b'], ((0, 0), (0, fpad)))
    return w


def _trunk(x, pos, p, w, cache):
    bsz, seq, _ = x.shape
    m = bsz * seq
    h = x.reshape(m, D_MODEL)
    tm = _tile(m, 512)
    tm_l = _tile(m, 1024)
    past = 0 if cache is None else cache['ckv'].shape[2]

    mla_cos, mla_sin = (jnp.tile(t, (bsz, 1)) for t in _mla_tables(pos))
    swa_c, swa_s1, swa_s2 = (jnp.tile(t, (bsz, 1)) for t in _swa_tables(pos))
    tab = _row_spec(tm, LANES)
    tab_l = _row_spec(tm_l, LANES)

    def out_proj(o, w_o, layer, h_res):
        return matmul(o, w_o, layer=layer, tm=tm_l, tn=1024, epilogue=_epi_residual,
                      extras=(h_res,), extra_specs=(_tile_spec(tm_l, 1024),),
                      out_shapes=jax.ShapeDtypeStruct((m, D_MODEL), F32),
                      out_specs=_tile_spec(tm_l, 1024), name="out_proj_residual")

    ckv_rows, krope_rows, conv_rows = [], [], []
    ks = vs = ks_b = vs_t = None
    for l in range(DEPTH):
        if l < N_A:
            n_a = w['mla_w_a'].shape[2]
            cq, ckv, ckv_b, krope, kr_b = matmul(
                h, w['mla_w_a'], layer=l, tm=tm, tn=n_a, epilogue=_epi_mla_in, norm_x=True,
                extras=(p['mla_g_q'][l].reshape(1, Q_LORA), p['mla_g_kv'][l].reshape(1, KV_LORA),
                        mla_cos, mla_sin, p['norm_attn'][l].reshape(1, D_MODEL)),
                extra_specs=(_const_spec(Q_LORA), _const_spec(KV_LORA), tab, tab,
                             _const_spec(D_MODEL)),
                out_shapes=[jax.ShapeDtypeStruct((m, Q_LORA), BF16),
                            jax.ShapeDtypeStruct((m, KV_LORA), F32),
                            jax.ShapeDtypeStruct((m, KV_LORA), BF16),
                            jax.ShapeDtypeStruct((m, QK_ROPE), F32),
                            jax.ShapeDtypeStruct((m, LANES), BF16)],
                out_specs=[_row_spec(tm, Q_LORA), _row_spec(tm, KV_LORA), _row_spec(tm, KV_LORA),
                           _row_spec(tm, QK_ROPE), _row_spec(tm, LANES)],
                name="mla_in_proj")
            ckv_rows.append(ckv.reshape(bsz, seq, KV_LORA))
            krope_rows.append(krope.reshape(bsz, seq, QK_ROPE))
            tn_q = 8 * MLA_QDIM
            q = matmul(cq, w['mla_w_uq'], layer=l, tm=tm_l, tn=tn_q, epilogue=_epi_mla_q,
                       extras=(mla_cos, mla_sin), extra_specs=(tab_l, tab_l),
                       out_shapes=jax.ShapeDtypeStruct((m, MLA_HEADS * MLA_QDIM), BF16),
                       out_specs=_tile_spec(tm_l, tn_q), name="mla_q_proj")
            if cache is None:
                ckv_all = ckv_b.reshape(bsz, seq, KV_LORA)
                kr_all = kr_b.reshape(bsz, seq, LANES)
                k_valid = seq
                tq = tk = min(seq, 1024)
            else:
                k_valid = past + seq
                sk = -(-k_valid // LANES) * LANES
                ckv_all = jnp.concatenate(
                    [cache['ckv'][l].astype(BF16), ckv_b.reshape(bsz, seq, KV_LORA),
                     jnp.zeros((bsz, sk - k_valid, KV_LORA), BF16)], axis=1)
                kr_cache = jnp.pad(cache['krope'][l], ((0, 0), (0, 0), (0, LANES - QK_ROPE)))
                kr_all = jnp.concatenate(
                    [kr_cache.astype(BF16), kr_b.reshape(bsz, seq, LANES),
                     jnp.zeros((bsz, sk - k_valid, LANES), BF16)], axis=1)
                tq, tk = LANES, sk
            sk = ckv_all.shape[1]
            mk = bsz * sk
            tmk = _tile(mk, 1024)
            kn = matmul(ckv_all.reshape(mk, KV_LORA), w['mla_w_uk'], layer=l, tm=tmk, tn=2048,
                        epilogue=_epi_cast,
                        out_shapes=jax.ShapeDtypeStruct((mk, MLA_HEADS * QK_NOPE), BF16),
                        out_specs=_tile_spec(tmk, 2048), name="mla_k_up_proj")
            if sk % 512 == 0:
                vt = matmul_nt(w['mla_w_uvt'], l, ckv_all, tn=MLA_HEADS * V_DIM, ts=512)
            else:
                vt = matmul_nt(w['mla_w_uvt'], l, ckv_all, tn=1024, ts=sk)
            q3 = q.reshape(bsz, seq, MLA_HEADS * MLA_QDIM)
            if seq < tq:
                q3 = jnp.pad(q3, ((0, 0), (0, tq - seq), (0, 0)))
            o = mla_attention(q3, kn.reshape(bsz, sk, MLA_HEADS * QK_NOPE), kr_all, vt,
                              q_off=past, k_valid=k_valid, tq=tq, tk=tk)
            h = out_proj(o[:, :seq].reshape(m, D_MODEL), w['mla_w_o'], l, h)
        else:
            i = l - N_A
            if l == N_A:
                ks, vs, ks_b, vs_t = matmul(
                    h, w['swa_w_kv'], tm=tm, tn=2 * SWA_KV_DIM, epilogue=_epi_shared_kv,
                    norm_x=True,
                    extras=(swa_c, swa_s1, swa_s2, p['kv_shared_norm'].reshape(1, D_MODEL)),
                    extra_specs=(tab, tab, tab, _const_spec(D_MODEL)),
                    out_shapes=[jax.ShapeDtypeStruct((m, SWA_KV_DIM), F32)] * 2
                    + [jax.ShapeDtypeStruct((m, SWA_KV_DIM), BF16),
                       jax.ShapeDtypeStruct((SWA_KV_DIM, m), BF16)],
                    out_specs=[_row_spec(tm, SWA_KV_DIM)] * 3
                    + [pl.BlockSpec((SWA_KV_DIM, tm), lambda i, j: (0, i))],
                    name="swa_shared_kv_proj")
            tm_q = _tile(m, 256)
            tab_q = _row_spec(tm_q, LANES)
            q = matmul(h, w['swa_w_q'], layer=i, tm=tm_q, tn=D_MODEL, epilogue=_epi_swa_q,
                       norm_x=True,
                       extras=(swa_c, swa_s1, swa_s2, p['norm_attn'][l].reshape(1, D_MODEL)),
                       extra_specs=(tab_q, tab_q, tab_q, _const_spec(D_MODEL)),
                       out_shapes=jax.ShapeDtypeStruct((m, D_MODEL), BF16),
                       out_specs=_tile_spec(tm_q, D_MODEL), name="swa_q_proj")
            q3 = q.reshape(bsz, seq, D_MODEL)
            k3 = ks_b.reshape(bsz, seq, SWA_KV_DIM)
            vt3 = vs_t.reshape(SWA_KV_DIM, bsz, seq).transpose(1, 0, 2)
            sink = w['swa_sinks'][i]
            if cache is None:
                o = swa_attention(q3, k3, vt3, sink)
            else:
                win = cache['swa_k'].shape[1]
                npad = 2 * WINDOW - win - seq
                k_all = jnp.concatenate(
                    [cache['swa_k'].reshape(bsz, win, SWA_KV_DIM).astype(BF16), k3,
                     jnp.zeros((bsz, npad, SWA_KV_DIM), BF16)], axis=1)
                vt_cache = cache['swa_v'].reshape(bsz, win, SWA_KV_DIM).transpose(0, 2, 1)
                vt_all = jnp.concatenate(
                    [vt_cache.astype(BF16), vt3, jnp.zeros((bsz, SWA_KV_DIM, npad), BF16)], axis=2)
                q3 = jnp.pad(q3, ((0, 0), (0, WINDOW - seq), (0, 0)))
                o = swa_attention(q3, k_all, vt_all, sink, k_valid=win + seq)[:, :seq]
            h = out_proj(o.reshape(m, D_MODEL), w['swa_w_o'], i, h)

        hn = rmsnorm(h, p['norm_ffn'][l], BF16)
        if cache is None:
            prev = jnp.zeros((bsz, SUBLANES, F_PAD), F32)
        else:
            prev = jnp.pad(cache['conv'][l],
                           ((0, 0), (SUBLANES - (CONV_W - 1), 0), (0, F_PAD - D_FF)))
        act, gl = ffn_a(hn, w['ffn_w_gate'], w['ffn_w_up'], l, w['ffn_conv_w'][l],
                        w['ffn_conv_b'][l], prev, seq)
        gl = gl.reshape(bsz, -1, SUBLANES, F_PAD)
        conv_rows.append(gl[:, -1, SUBLANES - (CONV_W - 1):, :D_FF])
        h = matmul(act, w['ffn_w_down'], layer=l, tm=tm, tn=512, epilogue=_epi_residual,
                   extras=(h,), extra_specs=(_tile_spec(tm, 512),),
                   out_shapes=jax.ShapeDtypeStruct((m, D_MODEL), F32),
                   out_specs=_tile_spec(tm, 512), name="ffn_down_residual")

    y = rmsnorm(h, p['norm_final'], F32).reshape(bsz, seq, D_MODEL)
    win = min(WINDOW, seq) if cache is None else seq
    ks4 = ks.reshape(bsz, seq, SWA_KV_DIM)[:, seq - win:].reshape(bsz, win, SWA_KV_HEADS, SWA_HEAD_DIM)
    vs4 = vs.reshape(bsz, seq, SWA_KV_DIM)[:, seq - win:].reshape(bsz, win, SWA_KV_HEADS, SWA_HEAD_DIM)
    return y, jnp.stack(ckv_rows), jnp.stack(krope_rows), ks4, vs4, jnp.stack(conv_rows)


def kernel(x_prompt, x_sample, cache_mla_ckv, cache_mla_krope, cache_swa_k, cache_swa_v, state_conv,
           norm_attn, norm_ffn, mla_w_a, mla_g_q, mla_g_kv, mla_w_uq, mla_w_uk, mla_w_uv, mla_w_o,
           kv_shared_norm, swa_w_kv, swa_w_q, swa_sinks, swa_w_o,
           ffn_w_gate, ffn_w_up, ffn_conv_w, ffn_conv_b, ffn_w_down, norm_final):
    p = {
        'norm_attn': norm_attn, 'norm_ffn': norm_ffn,
        'mla_w_a': mla_w_a, 'mla_g_q': mla_g_q, 'mla_g_kv': mla_g_kv, 'mla_w_uq': mla_w_uq,
        'mla_w_uk': mla_w_uk, 'mla_w_uv': mla_w_uv, 'mla_w_o': mla_w_o,
        'kv_shared_norm': kv_shared_norm, 'swa_w_kv': swa_w_kv, 'swa_w_q': swa_w_q,
        'swa_sinks': swa_sinks, 'swa_w_o': swa_w_o,
        'ffn_w_gate': ffn_w_gate, 'ffn_w_up': ffn_w_up, 'ffn_conv_w': ffn_conv_w,
        'ffn_conv_b': ffn_conv_b, 'ffn_w_down': ffn_w_down, 'norm_final': norm_final,
    }
    w = _prep_weights(p)
    pos_p = jnp.arange(x_prompt.shape[1])
    out_p = _trunk(x_prompt, pos_p, p, w, None)
    past = cache_mla_ckv.shape[2]
    pos_s = past + jnp.arange(x_sample.shape[1])
    cache = {'ckv': cache_mla_ckv, 'krope': cache_mla_krope, 'swa_k': cache_swa_k,
             'swa_v': cache_swa_v, 'conv': state_conv}
    out_s = _trunk(x_sample, pos_s, p, w, cache)
    return (out_p[0], out_s[0]) + out_p[1:] + out_s[1:]
```
